```python
import math
import jax, jax.numpy as jnp
from jax import lax
import numpy as np

D_MODEL = 1024
BATCH = 8
SEQ = 2048
DEPTH = 1
DEC_BATCH = 128
DEC_SEQ = 4
PAST_LEN = 16384
PAGE_SIZE = 128

H_A = 4
DK = 256
DV = 512
RET_CHUNK = 128
ROPE_BASE = 10000.0
GM_WIDTH = 1024
GM_GROUPS = 8
GM_GC = GM_WIDTH // GM_GROUPS
GM_CHUNK = 128
D_FF = 4 * D_MODEL
PLE_DIM = 256
EPS = 1e-6

Q_W = H_A * DK
V_W = H_A * DV
IN_SPLITS = (Q_W, Q_W, V_W, V_W, GM_WIDTH, GM_WIDTH, D_MODEL, D_MODEL)
IN_W = sum(IN_SPLITS)

kernel_name = "retnet_gmlp_gated_hybrid_step"


def rmsnorm(x, g):
    xf = x.astype(jnp.float32)
    y = xf * lax.rsqrt(jnp.mean(xf * xf, axis=-1, keepdims=True) + EPS)
    return (y * g.astype(jnp.float32)).astype(x.dtype)


def layernorm(x, g):
    xf = x.astype(jnp.float32)
    mu = jnp.mean(xf, axis=-1, keepdims=True)
    xc = xf - mu
    y = xc * lax.rsqrt(jnp.mean(xc * xc, axis=-1, keepdims=True) + EPS)
    return (y * g.astype(jnp.float32)).astype(x.dtype)


def rotary(x, pos):
    half = DK // 2
    inv = ROPE_BASE ** (-jnp.arange(half, dtype=jnp.float32) / half)
    ang = pos[:, None] * inv[None, :]
    cos = jnp.cos(ang)[None, :, None, :]
    sin = jnp.sin(ang)[None, :, None, :]
    xf = x.astype(jnp.float32)
    x1, x2 = xf[..., :half], xf[..., half:]
    return jnp.concatenate([x1 * cos - x2 * sin, x2 * cos + x1 * sin], axis=-1)


def log_decays():
    return jnp.log(1.0 - 2.0 ** (-5.0 - jnp.arange(H_A, dtype=jnp.float32)))


def retention_chunked(q, k, v, s0):
    b, L = q.shape[0], q.shape[1]
    c = min(RET_CHUNK, L)
    nc = L // c
    lg = log_decays()
    idx = jnp.arange(c, dtype=jnp.float32)
    diff = idx[:, None] - idx[None, :]
    dmat = jnp.where(diff[None] >= 0, jnp.exp(jnp.maximum(diff, 0.0)[None] * lg[:, None, None]), 0.0)
    in_decay = jnp.exp((idx[:, None] + 1.0) * lg[None, :])
    st_decay = jnp.exp((c - 1.0 - idx[None, :]) * lg[:, None])
    ch_decay = jnp.exp(c * lg)

    def to_chunks(a):
        return jnp.moveaxis(a.reshape(b, nc, c, *a.shape[2:]), 1, 0)

    qc, kc, vc = to_chunks(q), to_chunks(k), to_chunks(v)

    def step(s, inp):
        qi, ki, vi = inp
        sc = jnp.einsum('bqhd,bkhd->bhqk', qi, ki) * dmat[None]
        inner = jnp.einsum('bhqk,bkhv->bqhv', sc, vi)
        cross = jnp.einsum('bqhd,bhdv->bqhv', qi, s) * in_decay[None, :, :, None]
        s_new = s * ch_decay[None, :, None, None] + jnp.einsum('bkhd,bkhv,hk->bhdv', ki, vi, st_decay)
        return s_new, inner + cross

    s_fin, o = lax.scan(step, s0, (qc, kc, vc))
    o = jnp.moveaxis(o, 0, 1).reshape(b, L, H_A, DV)
    return o, s_fin


def spatial_gating(v, w_s, b_s):
    b, L = v.shape[0], v.shape[1]
    c = min(GM_CHUNK, L)
    nc = L // c
    mask = jnp.tril(jnp.ones((c, c), dtype=bool))
    w = jnp.where(mask[None], w_s[:, :c, :c], 0.0).astype(v.dtype)
    vr = v.reshape(b, nc, c, GM_GROUPS, GM_GC)
    out = jnp.einsum('gnm,bcmgd->bcngd', w, vr) + jnp.transpose(b_s[:, :c])[None, None, :, :, None].astype(v.dtype)
    return out.reshape(b, L, GM_WIDTH)


def layer(h, p_i, s0, pos, g_mix, w_in, g_ret, w_s, b_s, g_gm, w_br_a, w_br_b, w_o,
          g_mlp, w_up, w_down, g_ple, w_pg, w_pp):
    b, L = h.shape[0], h.shape[1]
    n = rmsnorm(h, g_mix)
    z = n @ w_in
    offs = np.cumsum((0,) + IN_SPLITS)
    zq, zk, zv, zg, zu, zgv, zga, zgb = [z[..., offs[j]:offs[j + 1]] for j in range(len(IN_SPLITS))]
    q = rotary(zq.reshape(b, L, H_A, DK), pos)
    k = rotary(zk.reshape(b, L, H_A, DK), pos) * (DK ** -0.5)
    vv = zv.reshape(b, L, H_A, DV).astype(jnp.float32)
    o, s_new = retention_chunked(q, k, vv, s0.astype(jnp.float32))
    o = rmsnorm(o, g_ret.reshape(H_A, DV)).reshape(b, L, V_W).astype(h.dtype)
    y_a = jax.nn.silu(zg) * o
    u = jax.nn.gelu(zu, approximate=False)
    gv = layernorm(jax.nn.gelu(zgv, approximate=False), g_gm)
    y_b = u * spatial_gating(gv, w_s, b_s)
    merged = jax.nn.sigmoid(zga) * (y_a @ w_br_a) + jax.nn.sigmoid(zgb) * (y_b @ w_br_b)
    h = h + merged @ w_o
    n2 = rmsnorm(h, g_mlp)
    h = h + jnp.square(jax.nn.relu(n2 @ w_up)) @ w_down
    n3 = rmsnorm(h, g_ple)
    h = h + jax.nn.sigmoid(n3 @ w_pg) * (p_i @ w_pp)
    return h, s_new.astype(s0.dtype), gv


def setup_inputs(seed: int = 0) -> dict:
    key = jax.random.key(seed)
    ks = jax.random.split(key, 24)
    f = jnp.float32

    def nrm(k, shape, scale):
        return jax.random.normal(k, shape, f) * scale

    def gain(k, shape):
        return 1.0 + 0.05 * jax.random.normal(k, shape, f)

    return {
        "x_prompt": nrm(ks[0], (BATCH, SEQ, D_MODEL), 1.0),
        "x_sample": nrm(ks[1], (DEC_BATCH, DEC_SEQ, D_MODEL), 1.0),
        "p_prompt": nrm(ks[2], (DEPTH, BATCH, SEQ, PLE_DIM), 1.0),
        "p_sample": nrm(ks[3], (DEPTH, DEC_BATCH, DEC_SEQ, PLE_DIM), 1.0),
        "state_ret": nrm(ks[4], (DEPTH, DEC_BATCH, H_A, DK, DV), 0.1),
        "g_mix": gain(ks[5], (DEPTH, D_MODEL)),
        "w_in": nrm(ks[6], (DEPTH, D_MODEL, IN_W), D_MODEL ** -0.5),
        "g_ret": gain(ks[7], (DEPTH, V_W)),
        "w_s": nrm(ks[8], (DEPTH, GM_GROUPS, GM_CHUNK, GM_CHUNK), 0.5 * GM_CHUNK ** -0.5),
        "b_s": 1.0 + 0.01 * jax.random.normal(ks[9], (DEPTH, GM_GROUPS, GM_CHUNK), f),
        "g_gm": gain(ks[10], (DEPTH, GM_WIDTH)),
        "w_br_a": nrm(ks[11], (DEPTH, V_W, D_MODEL), V_W ** -0.5),
        "w_br_b": nrm(ks[12], (DEPTH, GM_WIDTH, D_MODEL), GM_WIDTH ** -0.5),
        "w_o": nrm(ks[13], (DEPTH, D_MODEL, D_MODEL), D_MODEL ** -0.5),
        "g_mlp": gain(ks[14], (DEPTH, D_MODEL)),
        "w_up": nrm(ks[15], (DEPTH, D_MODEL, D_FF), D_MODEL ** -0.5),
        "w_down": nrm(ks[16], (DEPTH, D_FF, D_MODEL), D_FF ** -0.5),
        "g_ple": gain(ks[17], (DEPTH, D_MODEL)),
        "w_pg": nrm(ks[18], (DEPTH, D_MODEL, D_MODEL), D_MODEL ** -0.5),
        "w_pp": nrm(ks[19], (DEPTH, PLE_DIM, D_MODEL), PLE_DIM ** -0.5),
        "g_final": gain(ks[20], (D_MODEL,)),
    }


def reference(x_prompt, x_sample, p_prompt, p_sample, state_ret, g_mix, w_in, g_ret, w_s, b_s,
              g_gm, w_br_a, w_br_b, w_o, g_mlp, w_up, w_down, g_ple, w_pg, w_pp, g_final):
    pos_p = jnp.arange(SEQ, dtype=jnp.float32)
    pos_s = PAST_LEN + jnp.arange(DEC_SEQ, dtype=jnp.float32)
    hp, hs = x_prompt, x_sample
    sp_list, ss_list, gv_list = [], [], []
    for i in range(DEPTH):
        params = (g_mix[i], w_in[i], g_ret[i], w_s[i], b_s[i], g_gm[i], w_br_a[i], w_br_b[i], w_o[i],
                  g_mlp[i], w_up[i], w_down[i], g_ple[i], w_pg[i], w_pp[i])
        s0_p = jnp.zeros((BATCH, H_A, DK, DV), state_ret.dtype)
        hp, sp, _ = layer(hp, p_prompt[i], s0_p, pos_p, *params)
        hs, ss, gv_s = layer(hs, p_sample[i], state_ret[i], pos_s, *params)
        sp_list.append(sp)
        ss_list.append(ss)
        gv_list.append(gv_s)
    y_prompt = rmsnorm(hp, g_final)
    y_sample = rmsnorm(hs, g_final)
    state_ret_prompt = jnp.stack(sp_list, axis=0)
    state_ret_sample = jnp.stack(ss_list, axis=0)
    state_gm_v_sample = jnp.stack(gv_list, axis=0)
    return (y_prompt, y_sample, state_ret_prompt, state_ret_sample, state_gm_v_sample)
```

```python
import functools

import jax
import jax.numpy as jnp
from jax import lax
from jax.experimental import pallas as pl
from jax.experimental.pallas import tpu as pltpu

D_MODEL = 1024
H_A = 4
DK = 256
DV = 512
Q_W = H_A * DK
V_W = H_A * DV
GM_WIDTH = 1024
GM_GROUPS = 8
GM_GC = GM_WIDTH // GM_GROUPS
GM_CHUNK = 128
D_FF = 4 * D_MODEL
PLE_DIM = 256
EPS = 1e-6
ROPE_BASE = 10000.0
PAST_LEN = 16384

OFF_Q = 0
OFF_K = OFF_Q + Q_W
OFF_V = OFF_K + Q_W
OFF_G = OFF_V + V_W
OFF_U = OFF_G + V_W
OFF_GV = OFF_U + GM_WIDTH
OFF_GA = OFF_GV + GM_WIDTH
OFF_GB = OFF_GA + D_MODEL
IN_W = OFF_GB + D_MODEL

TILE = 256
MLP_TILE = 512
SAMPLE_PAIR = 2
VMEM_LIMIT = 56 * 1024 * 1024

BF16 = jnp.bfloat16
F32 = jnp.float32


def _mm(a, b):
    return jnp.dot(a, b, preferred_element_type=F32)


def _mm_nt(a, b):
    return lax.dot_general(a, b, (((1,), (1,)), ((), ())), preferred_element_type=F32)


def _mm_tn(a, b):
    return lax.dot_general(a, b, (((0,), (0,)), ((), ())), preferred_element_type=F32)


def _rms(x, g):
    return x * lax.rsqrt(jnp.mean(x * x, axis=-1, keepdims=True) + EPS) * g


def _sigmoid(x):
    return 1.0 / (1.0 + jnp.exp(-x))


def _gelu(x):
    return 0.5 * x * (1.0 + lax.erf(x * (2.0 ** -0.5)))


def _layernorm(x, g):
    mu = jnp.mean(x, axis=-1, keepdims=True)
    xc = x - mu
    return xc * lax.rsqrt(jnp.mean(xc * xc, axis=-1, keepdims=True) + EPS) * g


def _rotary(z, cos, sin):
    half = DK // 2
    z1, z2 = z[:, :half], z[:, half:]
    return jnp.concatenate([z1 * cos - z2 * sin, z2 * cos + z1 * sin], axis=-1)


def _group_rms(o, g):
    return o * lax.rsqrt(jnp.mean(o * o, axis=-1, keepdims=True) + EPS) * g


def _prompt_mix_kernel(x_ref, cos_ref, sin_ref, dmat_ref, ind_ref, std_ref, gmix_ref, gret_ref,
                       ggm_ref, win_ref, ws_ref, bias_ref, wa_ref, wb_ref, wo_ref,
                       h_ref, s_ref, ya_ref, yb_ref, *, ch_decay):
    j = pl.program_id(1)

    @pl.when(j == 0)
    def _():
        s_ref[...] = jnp.zeros_like(s_ref)

    x = x_ref[0]
    n = _rms(x, gmix_ref[...]).astype(BF16)
    cos = cos_ref[...]
    sin = sin_ref[...]

    zq = _mm(n, win_ref[:, OFF_Q:OFF_Q + Q_W])
    zk = _mm(n, win_ref[:, OFF_K:OFF_K + Q_W])
    zv = _mm(n, win_ref[:, OFF_V:OFF_V + V_W]).astype(BF16)
    zg = _mm(n, win_ref[:, OFF_G:OFF_G + V_W])
    for h in range(H_A):
        qr = _rotary(zq[:, h * DK:(h + 1) * DK], cos, sin)
        kr = _rotary(zk[:, h * DK:(h + 1) * DK], cos, sin) * (DK ** -0.5)
        vb = zv[:, h * DV:(h + 1) * DV]
        sc = _mm_nt(qr.astype(BF16), kr.astype(BF16)) * dmat_ref[h]
        s_old = s_ref[0, h]
        o = _mm(sc.astype(BF16), vb) + _mm((qr * ind_ref[h]).astype(BF16), s_old.astype(BF16))
        s_ref[0, h] = s_old * ch_decay[h] + _mm_tn((kr * std_ref[h]).astype(BF16), vb)
        on = _group_rms(o, gret_ref[:, h * DV:(h + 1) * DV])
        zg_h = zg[:, h * DV:(h + 1) * DV]
        ya_ref[:, h * DV:(h + 1) * DV] = (zg_h * _sigmoid(zg_h) * on).astype(BF16)

    u = _gelu(_mm(n, win_ref[:, OFF_U:OFF_U + GM_WIDTH]))
    gv = _layernorm(_gelu(_mm(n, win_ref[:, OFF_GV:OFF_GV + GM_WIDTH])), ggm_ref[...])
    gvb = gv.astype(BF16)
    for c in range(TILE // GM_CHUNK):
        rows = slice(c * GM_CHUNK, (c + 1) * GM_CHUNK)
        for g in range(GM_GROUPS):
            cols = slice(g * GM_GC, (g + 1) * GM_GC)
            sg = _mm(ws_ref[g], gvb[rows, cols]) + bias_ref[:, cols]
            yb_ref[rows, cols] = (u[rows, cols] * sg).astype(BF16)

    ga = _sigmoid(_mm(n, win_ref[:, OFF_GA:OFF_GA + D_MODEL]))
    gb = _sigmoid(_mm(n, win_ref[:, OFF_GB:OFF_GB + D_MODEL]))
    merged = ga * _mm(ya_ref[...], wa_ref[...]) + gb * _mm(yb_ref[...], wb_ref[...])
    h_ref[0] = x + _mm(merged.astype(BF16), wo_ref[...])


def _resident(shape):
    zeros = (0,) * len(shape)
    return pl.BlockSpec(shape, lambda *_: zeros, pipeline_mode=pl.Buffered(1))


def _prompt_mix(x, cos, sin, dmat, ind, std, ch_decay, g_mix, g_ret, g_gm, w_in, ws_tril, bias, w_a, w_b, w_o):
    b, L, _ = x.shape
    steps = L // TILE
    return pl.pallas_call(
        functools.partial(_prompt_mix_kernel, ch_decay=ch_decay),
        grid=(b, steps),
        in_specs=[
            pl.BlockSpec((1, TILE, D_MODEL), lambda i, j: (i, j, 0)),
            pl.BlockSpec((TILE, DK // 2), lambda i, j: (j, 0)),
            pl.BlockSpec((TILE, DK // 2), lambda i, j: (j, 0)),
            _resident(dmat.shape),
            _resident(ind.shape),
            _resident(std.shape),
            _resident(g_mix.shape),
            _resident(g_ret.shape),
            _resident(g_gm.shape),
            _resident(w_in.shape),
            _resident(ws_tril.shape),
            _resident(bias.shape),
            _resident(w_a.shape),
            _resident(w_b.shape),
            _resident(w_o.shape),
        ],
        out_specs=[
            pl.BlockSpec((1, TILE, D_MODEL), lambda i, j: (i, j, 0)),
            pl.BlockSpec((1, H_A, DK, DV), lambda i, j: (i, 0, 0, 0)),
        ],
        out_shape=[
            jax.ShapeDtypeStruct((b, L, D_MODEL), F32),
            jax.ShapeDtypeStruct((b, H_A, DK, DV), F32),
        ],
        scratch_shapes=[
            pltpu.VMEM((TILE, V_W), BF16),
            pltpu.VMEM((TILE, GM_WIDTH), BF16),
        ],
        compiler_params=pltpu.CompilerParams(
            dimension_semantics=("arbitrary", "arbitrary"), vmem_limit_bytes=VMEM_LIMIT),
        name="prompt_mix",
    )(x, cos, sin, dmat, ind, std, g_mix, g_ret, g_gm, w_in, ws_tril, bias, w_a, w_b, w_o)


def _tail_kernel(h_ref, p_ref, gmlp_ref, gple_ref, gfin_ref, wup_ref, wdown_ref, wpg_ref, wpp_ref, y_ref):
    h = h_ref[...]
    n2 = _rms(h, gmlp_ref[...]).astype(BF16)
    a = jnp.maximum(_mm(n2, wup_ref[...]), 0.0)
    h = h + _mm((a * a).astype(BF16), wdown_ref[...])
    n3 = _rms(h, gple_ref[...]).astype(BF16)
    gate = _sigmoid(_mm(n3, wpg_ref[...]))
    h = h + gate * _mm(p_ref[...].astype(BF16), wpp_ref[...])
    y_ref[...] = _rms(h, gfin_ref[...])


def _tail(h, p, g_mlp, g_ple, g_fin, w_up, w_down, w_pg, w_pp):
    t = h.shape[0]
    tile = min(MLP_TILE, t)
    return pl.pallas_call(
        _tail_kernel,
        grid=(t // tile,),
        in_specs=[
            pl.BlockSpec((tile, D_MODEL), lambda i: (i, 0)),
            pl.BlockSpec((tile, PLE_DIM), lambda i: (i, 0)),
            _resident(g_mlp.shape),
            _resident(g_ple.shape),
            _resident(g_fin.shape),
            _resident(w_up.shape),
            _resident(w_down.shape),
            _resident(w_pg.shape),
            _resident(w_pp.shape),
        ],
        out_specs=pl.BlockSpec((tile, D_MODEL), lambda i: (i, 0)),
        out_shape=jax.ShapeDtypeStruct((t, D_MODEL), F32),
        compiler_params=pltpu.CompilerParams(
            dimension_semantics=("arbitrary",), vmem_limit_bytes=VMEM_LIMIT),
        name="tail",
    )(h, p, g_mlp, g_ple, g_fin, w_up, w_down, w_pg, w_pp)


def _sample_proj_kernel(x_ref, cos_ref, sin_ref, gmix_ref, ggm_ref, win_ref,
                        q_ref, k_ref, v_ref, sg_ref, u_ref, gv_ref, ga_ref, gb_ref):
    n = _rms(x_ref[...], gmix_ref[...]).astype(BF16)
    cos = cos_ref[...]
    sin = sin_ref[...]
    zq = _mm(n, win_ref[:, OFF_Q:OFF_Q + Q_W])
    zk = _mm(n, win_ref[:, OFF_K:OFF_K + Q_W])
    for h in range(H_A):
        q_ref[:, h * DK:(h + 1) * DK] = _rotary(zq[:, h * DK:(h + 1) * DK], cos, sin)
        k_ref[:, h * DK:(h + 1) * DK] = _rotary(zk[:, h * DK:(h + 1) * DK], cos, sin) * (DK ** -0.5)
    v_ref[...] = _mm(n, win_ref[:, OFF_V:OFF_V + V_W])
    zg = _mm(n, win_ref[:, OFF_G:OFF_G + V_W])
    sg_ref[...] = zg * _sigmoid(zg)
    u_ref[...] = _gelu(_mm(n, win_ref[:, OFF_U:OFF_U + GM_WIDTH]))
    gv_ref[...] = _layernorm(_gelu(_mm(n, win_ref[:, OFF_GV:OFF_GV + GM_WIDTH])), ggm_ref[...])
    ga_ref[...] = _sigmoid(_mm(n, win_ref[:, OFF_GA:OFF_GA + D_MODEL]))
    gb_ref[...] = _sigmoid(_mm(n, win_ref[:, OFF_GB:OFF_GB + D_MODEL]))


def _sample_proj(x, cos, sin, g_mix, g_gm, w_in):
    t = x.shape[0]
    tile = TILE
    row = lambda w: pl.BlockSpec((tile, w), lambda i: (i, 0))
    widths = (Q_W, Q_W, V_W, V_W, GM_WIDTH, GM_WIDTH, D_MODEL, D_MODEL)
    return pl.pallas_call(
        _sample_proj_kernel,
        grid=(t // tile,),
        in_specs=[row(D_MODEL), row(DK // 2), row(DK // 2),
                  _resident(g_mix.shape), _resident(g_gm.shape), _resident(w_in.shape)],
        out_specs=[row(w) for w in widths],
        out_shape=[jax.ShapeDtypeStruct((t, w), F32) for w in widths],
        compiler_params=pltpu.CompilerParams(
            dimension_semantics=("arbitrary",), vmem_limit_bytes=VMEM_LIMIT),
        name="sample_proj",
    )(x, cos, sin, g_mix, g_gm, w_in)


def _sample_ret_kernel(q_ref, k_ref, v_ref, s_ref, dm_ref, ind_ref, std_ref, o_ref, snew_ref, *,
                       ch_decay, dec_seq):
    rows = lax.broadcasted_iota(jnp.int32, (SAMPLE_PAIR * dec_seq, 1), 0)
    for h in range(H_A):
        q = q_ref[:, h * DK:(h + 1) * DK]
        k = k_ref[:, h * DK:(h + 1) * DK]
        vb = v_ref[:, h * DV:(h + 1) * DV].astype(BF16)
        sc = _mm_nt(q.astype(BF16), k.astype(BF16)) * dm_ref[h]
        o = _mm(sc.astype(BF16), vb)
        qd = (q * ind_ref[h]).astype(BF16)
        kd = k * std_ref[h]
        for bb in range(SAMPLE_PAIR):
            mine = (rows >= bb * dec_seq) & (rows < (bb + 1) * dec_seq)
            s_old = s_ref[bb, h]
            o = o + jnp.where(mine, _mm(qd, s_old.astype(BF16)), 0.0)
            kd_b = jnp.where(mine, kd, 0.0).astype(BF16)
            snew_ref[bb, h] = s_old * ch_decay[h] + _mm_tn(kd_b, vb)
        o_ref[:, h * DV:(h + 1) * DV] = o


def _sample_ret(q, k, v, state, dm, ind, std, ch_decay, dec_seq):
    nb = state.shape[0]
    rows = SAMPLE_PAIR * dec_seq
    row = lambda w: pl.BlockSpec((rows, w), lambda i: (i, 0))
    st = pl.BlockSpec((SAMPLE_PAIR, H_A, DK, DV), lambda i: (i, 0, 0, 0))
    return pl.pallas_call(
        functools.partial(_sample_ret_kernel, ch_decay=ch_decay, dec_seq=dec_seq),
        grid=(nb // SAMPLE_PAIR,),
        in_specs=[row(Q_W), row(Q_W), row(V_W), st,
                  _resident(dm.shape), _resident(ind.shape), _resident(std.shape)],
        out_specs=[row(V_W), st],
        out_shape=[jax.ShapeDtypeStruct((nb * dec_seq, V_W), F32),
                   jax.ShapeDtypeStruct(state.shape, F32)],
        compiler_params=pltpu.CompilerParams(
            dimension_semantics=("arbitrary",), vmem_limit_bytes=VMEM_LIMIT),
        name="sample_ret",
    )(q, k, v, state, dm, ind, std)


def _sample_merge_kernel(x_ref, o_ref, sg_ref, u_ref, gv_ref, ga_ref, gb_ref, gret_ref, coef_ref,
                         bias_ref, wa_ref, wb_ref, wo_ref, h_ref, *, dec_seq):
    t = x_ref.shape[0]
    o = o_ref[...]
    ya = []
    for h in range(H_A):
        cols = slice(h * DV, (h + 1) * DV)
        ya.append((sg_ref[:, cols] * _group_rms(o[:, cols], gret_ref[:, cols])).astype(BF16))
    ya = jnp.concatenate(ya, axis=-1)

    gv = gv_ref[...]
    sub = coef_ref.shape[1]
    mix = None
    for s in range(dec_seq):
        shifted = gv if s == 0 else pltpu.roll(gv, s, 0)
        term = shifted.reshape(t // sub, sub, GM_WIDTH) * coef_ref[s][None]
        mix = term if mix is None else mix + term
    mix = (mix + bias_ref[...][None]).reshape(t, GM_WIDTH)
    yb = (u_ref[...] * mix).astype(BF16)

    merged = ga_ref[...] * _mm(ya, wa_ref[...]) + gb_ref[...] * _mm(yb, wb_ref[...])
    h_ref[...] = x_ref[...] + _mm(merged.astype(BF16), wo_ref[...])


def _sample_merge(x, o, sg, u, gv, ga, gb, g_ret, coef, bias, w_a, w_b, w_o, dec_seq):
    t = x.shape[0]
    tile = TILE
    row = lambda w: pl.BlockSpec((tile, w), lambda i: (i, 0))
    return pl.pallas_call(
        functools.partial(_sample_merge_kernel, dec_seq=dec_seq),
        grid=(t // tile,),
        in_specs=[row(D_MODEL), row(V_W), row(V_W), row(GM_WIDTH), row(GM_WIDTH), row(D_MODEL),
                  row(D_MODEL), _resident(g_ret.shape), _resident(coef.shape), _resident(bias.shape),
                  _resident(w_a.shape), _resident(w_b.shape), _resident(w_o.shape)],
        out_specs=row(D_MODEL),
        out_shape=jax.ShapeDtypeStruct((t, D_MODEL), F32),
        compiler_params=pltpu.CompilerParams(
            dimension_semantics=("arbitrary",), vmem_limit_bytes=VMEM_LIMIT),
        name="sample_merge",
    )(x, o, sg, u, gv, ga, gb, g_ret, coef, bias, w_a, w_b, w_o)


def _rope_tables(pos):
    half = DK // 2
    inv = ROPE_BASE ** (-jnp.arange(half, dtype=F32) / half)
    ang = pos[:, None] * inv[None, :]
    return jnp.cos(ang), jnp.sin(ang)


def _decay_tables(c):
    lg = jnp.log(1.0 - 2.0 ** (-5.0 - jnp.arange(H_A, dtype=F32)))
    idx = jnp.arange(c, dtype=F32)
    diff = idx[:, None] - idx[None, :]
    dmat = jnp.where(diff[None] >= 0, jnp.exp(jnp.maximum(diff, 0.0)[None] * lg[:, None, None]), 0.0)
    in_decay = jnp.exp((idx[None, :] + 1.0) * lg[:, None])
    st_decay = jnp.exp((c - 1.0 - idx[None, :]) * lg[:, None])
    return dmat, in_decay, st_decay


def _chunk_decay(c):
    return tuple(float((1.0 - 2.0 ** (-5.0 - h)) ** c) for h in range(H_A))


def kernel(x_prompt, x_sample, p_prompt, p_sample, state_ret, g_mix, w_in, g_ret, w_s, b_s, g_gm,
           w_br_a, w_br_b, w_o, g_mlp, w_up, w_down, g_ple, w_pg, w_pp, g_final):
    depth = w_in.shape[0]
    assert depth == 1
    batch, seq, _ = x_prompt.shape
    dec_batch, dec_seq, _ = x_sample.shape
    i = 0

    w_in_b = w_in[i].astype(BF16)
    w_a_b = w_br_a[i].astype(BF16)
    w_b_b = w_br_b[i].astype(BF16)
    w_o_b = w_o[i].astype(BF16)
    w_up_b = w_up[i].astype(BF16)
    w_down_b = w_down[i].astype(BF16)
    w_pg_b = w_pg[i].astype(BF16)
    w_pp_b = w_pp[i].astype(BF16)
    gmix = g_mix[i][None]
    gret = g_ret[i][None]
    ggm = g_gm[i][None]
    gmlp = g_mlp[i][None]
    gple = g_ple[i][None]
    gfin = g_final[None]

    cos_p, sin_p = _rope_tables(jnp.arange(seq, dtype=F32))
    dmat, in_dec, st_dec = _decay_tables(TILE)
    ind_p = jnp.broadcast_to(in_dec[:, :, None], (H_A, TILE, DK))
    std_p = jnp.broadcast_to(st_dec[:, :, None], (H_A, TILE, DK))
    tril = jnp.tril(jnp.ones((GM_CHUNK, GM_CHUNK), dtype=bool))
    ws_tril = jnp.where(tril[None], w_s[i], 0.0).astype(BF16)
    bias_p = jnp.repeat(jnp.transpose(b_s[i]), GM_GC, axis=1)
    h_p, s_p = _prompt_mix(x_prompt, cos_p, sin_p, dmat, ind_p, std_p, _chunk_decay(TILE),
                           gmix, gret, ggm, w_in_b, ws_tril, bias_p, w_a_b, w_b_b, w_o_b)
    y_p = _tail(h_p.reshape(batch * seq, D_MODEL), p_prompt[i].reshape(batch * seq, PLE_DIM),
                gmlp, gple, gfin, w_up_b, w_down_b, w_pg_b, w_pp_b)

    ts = dec_batch * dec_seq
    xs = x_sample.reshape(ts, D_MODEL)
    pos_s = PAST_LEN + jnp.arange(dec_seq, dtype=F32)
    cos_s, sin_s = _rope_tables(pos_s)
    cos_s = jnp.tile(cos_s, (dec_batch, 1))
    sin_s = jnp.tile(sin_s, (dec_batch, 1))
    q_s, k_s, v_s, sg_s, u_s, gv_s, ga_s, gb_s = _sample_proj(xs, cos_s, sin_s, gmix, ggm, w_in_b)

    dm4, in4, st4 = _decay_tables(dec_seq)
    pair_rows = SAMPLE_PAIR * dec_seq
    same = (jnp.arange(pair_rows)[:, None] // dec_seq) == (jnp.arange(pair_rows)[None, :] // dec_seq)
    dm_pair = jnp.where(same[None], jnp.tile(dm4, (1, SAMPLE_PAIR, SAMPLE_PAIR)), 0.0)
    ind_pair = jnp.broadcast_to(jnp.tile(in4, (1, SAMPLE_PAIR))[:, :, None], (H_A, pair_rows, DK))
    std_pair = jnp.broadcast_to(jnp.tile(st4, (1, SAMPLE_PAIR))[:, :, None], (H_A, pair_rows, DK))
    o_s, s_s = _sample_ret(q_s, k_s, v_s, state_ret[i], dm_pair, ind_pair, std_pair,
                           _chunk_decay(dec_seq), dec_seq)

    sub = 2 * dec_seq
    w_small = jnp.where(tril[None, :dec_seq, :dec_seq], w_s[i][:, :dec_seq, :dec_seq], 0.0)
    tpos = jnp.arange(sub) % dec_seq
    coef = []
    for s in range(dec_seq):
        src = tpos - s
        c = jnp.where((src >= 0)[None, :], w_small[:, tpos, jnp.maximum(src, 0)], 0.0)
        coef.append(jnp.repeat(jnp.transpose(c), GM_GC, axis=1))
    coef = jnp.stack(coef, axis=0)
    bias_s = jnp.repeat(jnp.transpose(b_s[i][:, tpos]), GM_GC, axis=1)
    h_s = _sample_merge(xs, o_s, sg_s, u_s, gv_s, ga_s, gb_s, gret, coef, bias_s,
                        w_a_b, w_b_b, w_o_b, dec_seq)
    y_s = _tail(h_s, p_sample[i].reshape(ts, PLE_DIM), gmlp, gple, gfin, w_up_b, w_down_b, w_pg_b, w_pp_b)

    return (y_p.reshape(batch, seq, D_MODEL),
            y_s.reshape(dec_batch, dec_seq, D_MODEL),
            s_p[None],
            s_s[None],
            gv_s.reshape(dec_batch, dec_seq, GM_WIDTH)[None])
```

```python
import functools

import jax
import jax.numpy as jnp
from jax import lax
from jax.experimental import pallas as pl
from jax.experimental.pallas import tpu as pltpu

D_MODEL = 1024
H_A = 4
DK = 256
DV = 512
Q_W = H_A * DK
V_W = H_A * DV
GM_WIDTH = 1024
GM_GROUPS = 8
GM_GC = GM_WIDTH // GM_GROUPS
GM_CHUNK = 128
D_FF = 4 * D_MODEL
PLE_DIM = 256
EPS = 1e-6
ROPE_BASE = 10000.0
PAST_LEN = 16384

OFF_Q = 0
OFF_K = OFF_Q + Q_W
OFF_V = OFF_K + Q_W
OFF_G = OFF_V + V_W
OFF_U = OFF_G + V_W
OFF_GV = OFF_U + GM_WIDTH
OFF_GA = OFF_GV + GM_WIDTH
OFF_GB = OFF_GA + D_MODEL
IN_W = OFF_GB + D_MODEL

TILE = 256
MLP_TILE = 512
SAMPLE_PAIR = 2
VMEM_LIMIT = 56 * 1024 * 1024

BF16 = jnp.bfloat16
F32 = jnp.float32


def _mm(a, b):
    return jnp.dot(a, b, preferred_element_type=F32)


def _mm_nt(a, b):
    return lax.dot_general(a, b, (((1,), (1,)), ((), ())), preferred_element_type=F32)


def _mm_tn(a, b):
    return lax.dot_general(a, b, (((0,), (0,)), ((), ())), preferred_element_type=F32)


def _rms(x, g):
    return x * lax.rsqrt(jnp.mean(x * x, axis=-1, keepdims=True) + EPS) * g


def _sigmoid(x):
    return 1.0 / (1.0 + jnp.exp(-x))


def _gelu(x):
    return 0.5 * x * (1.0 + lax.erf(x * (2.0 ** -0.5)))


def _layernorm(x, g):
    mu = jnp.mean(x, axis=-1, keepdims=True)
    xc = x - mu
    return xc * lax.rsqrt(jnp.mean(xc * xc, axis=-1, keepdims=True) + EPS) * g


def _rotary(z, cos, sin):
    half = DK // 2
    z1, z2 = z[:, :half], z[:, half:]
    return jnp.concatenate([z1 * cos - z2 * sin, z2 * cos + z1 * sin], axis=-1)


def _group_rms(o, g):
    return o * lax.rsqrt(jnp.mean(o * o, axis=-1, keepdims=True) + EPS) * g


def _prompt_mix_kernel(x_ref, cos_ref, sin_ref, dmat_ref, ind_ref, std_ref, gmix_ref, gret_ref,
                       ggm_ref, win_ref, ws_ref, bias_ref, wa_ref, wb_ref, wo_ref,
                       h_ref, s_ref, ya_ref, yb_ref, *, ch_decay):
    j = pl.program_id(1)

    @pl.when(j == 0)
    def _():
        s_ref[...] = jnp.zeros_like(s_ref)

    x = x_ref[0]
    n = _rms(x, gmix_ref[...]).astype(BF16)
    cos = cos_ref[...]
    sin = sin_ref[...]

    zq = _mm(n, win_ref[:, OFF_Q:OFF_Q + Q_W])
    zk = _mm(n, win_ref[:, OFF_K:OFF_K + Q_W])
    zv = _mm(n, win_ref[:, OFF_V:OFF_V + V_W]).astype(BF16)
    zg = _mm(n, win_ref[:, OFF_G:OFF_G + V_W])
    for h in range(H_A):
        qr = _rotary(zq[:, h * DK:(h + 1) * DK], cos, sin)
        kr = _rotary(zk[:, h * DK:(h + 1) * DK], cos, sin) * (DK ** -0.5)
        vb = zv[:, h * DV:(h + 1) * DV]
        sc = _mm_nt(qr.astype(BF16), kr.astype(BF16)) * dmat_ref[h]
        s_old = s_ref[0, h]
        o = _mm(sc.astype(BF16), vb) + _mm((qr * ind_ref[h]).astype(BF16), s_old.astype(BF16))
        s_ref[0, h] = s_old * ch_decay[h] + _mm_tn((kr * std_ref[h]).astype(BF16), vb)
        on = _group_rms(o, gret_ref[:, h * DV:(h + 1) * DV])
        zg_h = zg[:, h * DV:(h + 1) * DV]
        ya_ref[:, h * DV:(h + 1) * DV] = (zg_h * _sigmoid(zg_h) * on).astype(BF16)

    u = _gelu(_mm(n, win_ref[:, OFF_U:OFF_U + GM_WIDTH]))
    gv = _layernorm(_gelu(_mm(n, win_ref[:, OFF_GV:OFF_GV + GM_WIDTH])), ggm_ref[...])
    gvb = gv.astype(BF16)
    for c in range(TILE // GM_CHUNK):
        rows = slice(c * GM_CHUNK, (c + 1) * GM_CHUNK)
        for g in range(GM_GROUPS):
            cols = slice(g * GM_GC, (g + 1) * GM_GC)
            sg = _mm(ws_ref[g], gvb[rows, cols]) + bias_ref[:, cols]
            yb_ref[rows, cols] = (u[rows, cols] * sg).astype(BF16)

    ga = _sigmoid(_mm(n, win_ref[:, OFF_GA:OFF_GA + D_MODEL]))
    gb = _sigmoid(_mm(n, win_ref[:, OFF_GB:OFF_GB + D_MODEL]))
    merged = ga * _mm(ya_ref[...], wa_ref[...]) + gb * _mm(yb_ref[...], wb_ref[...])
    h_ref[0] = x + _mm(merged.astype(BF16), wo_ref[...])


def _resident(shape):
    zeros = (0,) * len(shape)
    return pl.BlockSpec(shape, lambda *_: zeros, pipeline_mode=pl.Buffered(1))


def _prompt_mix(x, cos, sin, dmat, ind, std, ch_decay, g_mix, g_ret, g_gm, w_in, ws_tril, bias, w_a, w_b, w_o):
    b, L, _ = x.shape
    steps = L // TILE
    return pl.pallas_call(
        functools.partial(_prompt_mix_kernel, ch_decay=ch_decay),
        grid=(b, steps),
        in_specs=[
            pl.BlockSpec((1, TILE, D_MODEL), lambda i, j: (i, j, 0)),
            pl.BlockSpec((TILE, DK // 2), lambda i, j: (j, 0)),
            pl.BlockSpec((TILE, DK // 2), lambda i, j: (j, 0)),
            _resident(dmat.shape),
            _resident(ind.shape),
            _resident(std.shape),
            _resident(g_mix.shape),
            _resident(g_ret.shape),
            _resident(g_gm.shape),
            _resident(w_in.shape),
            _resident(ws_tril.shape),
            _resident(bias.shape),
            _resident(w_a.shape),
            _resident(w_b.shape),
            _resident(w_o.shape),
        ],
        out_specs=[
            pl.BlockSpec((1, TILE, D_MODEL), lambda i, j: (i, j, 0)),
            pl.BlockSpec((1, H_A, DK, DV), lambda i, j: (i, 0, 0, 0)),
        ],
        out_shape=[
            jax.ShapeDtypeStruct((b, L, D_MODEL), F32),
            jax.ShapeDtypeStruct((b, H_A, DK, DV), F32),
        ],
        scratch_shapes=[
            pltpu.VMEM((TILE, V_W), BF16),
            pltpu.VMEM((TILE, GM_WIDTH), BF16),
        ],
        compiler_params=pltpu.CompilerParams(
            dimension_semantics=("arbitrary", "arbitrary"), vmem_limit_bytes=VMEM_LIMIT),
        name="prompt_mix",
    )(x, cos, sin, dmat, ind, std, g_mix, g_ret, g_gm, w_in, ws_tril, bias, w_a, w_b, w_o)


def _tail_body(h_ref, p_ref, gmlp_ref, gple_ref, gfin_ref, wup_ref, wdown_ref, wpg_ref, wpp_ref, y_ref):
    h = h_ref[...]
    n2 = _rms(h, gmlp_ref[...]).astype(BF16)
    a = jnp.maximum(_mm(n2, wup_ref[...]), 0.0)
    h = h + _mm((a * a).astype(BF16), wdown_ref[...])
    n3 = _rms(h, gple_ref[...]).astype(BF16)
    gate = _sigmoid(_mm(n3, wpg_ref[...]))
    h = h + gate * _mm(p_ref[...].astype(BF16), wpp_ref[...])
    y_ref[...] = _rms(h, gfin_ref[...])


def _tail_kernel(*refs):
    _tail_body(*refs)


def _tail_specs(tile, g_mlp, g_ple, g_fin, w_up, w_down, w_pg, w_pp):
    return [
        pl.BlockSpec((tile, D_MODEL), lambda i: (i, 0)),
        pl.BlockSpec((tile, PLE_DIM), lambda i: (i, 0)),
        _resident(g_mlp.shape),
        _resident(g_ple.shape),
        _resident(g_fin.shape),
        _resident(w_up.shape),
        _resident(w_down.shape),
        _resident(w_pg.shape),
        _resident(w_pp.shape),
    ]


def _tail(h, p, g_mlp, g_ple, g_fin, w_up, w_down, w_pg, w_pp):
    t = h.shape[0]
    tile = min(MLP_TILE, t)
    return pl.pallas_call(
        _tail_kernel,
        grid=(t // tile,),
        in_specs=_tail_specs(tile, g_mlp, g_ple, g_fin, w_up, w_down, w_pg, w_pp),
        out_specs=pl.BlockSpec((tile, D_MODEL), lambda i: (i, 0)),
        out_shape=jax.ShapeDtypeStruct((t, D_MODEL), F32),
        compiler_params=pltpu.CompilerParams(
            dimension_semantics=("arbitrary",), vmem_limit_bytes=VMEM_LIMIT),
        name="tail",
    )(h, p, g_mlp, g_ple, g_fin, w_up, w_down, w_pg, w_pp)


def _sample_proj_kernel(x_ref, cos_ref, sin_ref, gmix_ref, ggm_ref, win_ref,
                        q_ref, k_ref, v_ref, sg_ref, u_ref, gv_ref, ga_ref, gb_ref):
    n = _rms(x_ref[...], gmix_ref[...]).astype(BF16)
    cos = cos_ref[...]
    sin = sin_ref[...]
    zq = _mm(n, win_ref[:, OFF_Q:OFF_Q + Q_W])
    zk = _mm(n, win_ref[:, OFF_K:OFF_K + Q_W])
    for h in range(H_A):
        q_ref[:, h * DK:(h + 1) * DK] = _rotary(zq[:, h * DK:(h + 1) * DK], cos, sin)
        k_ref[:, h * DK:(h + 1) * DK] = _rotary(zk[:, h * DK:(h + 1) * DK], cos, sin) * (DK ** -0.5)
    v_ref[...] = _mm(n, win_ref[:, OFF_V:OFF_V + V_W])
    zg = _mm(n, win_ref[:, OFF_G:OFF_G + V_W])
    sg_ref[...] = zg * _sigmoid(zg)
    u_ref[...] = _gelu(_mm(n, win_ref[:, OFF_U:OFF_U + GM_WIDTH]))
    gv_ref[...] = _layernorm(_gelu(_mm(n, win_ref[:, OFF_GV:OFF_GV + GM_WIDTH])), ggm_ref[...])
    ga_ref[...] = _sigmoid(_mm(n, win_ref[:, OFF_GA:OFF_GA + D_MODEL]))
    gb_ref[...] = _sigmoid(_mm(n, win_ref[:, OFF_GB:OFF_GB + D_MODEL]))


def _sample_proj(x, cos, sin, g_mix, g_gm, w_in):
    t = x.shape[0]
    tile = TILE
    row = lambda w: pl.BlockSpec((tile, w), lambda i: (i, 0))
    widths = (Q_W, Q_W, V_W, V_W, GM_WIDTH, GM_WIDTH, D_MODEL, D_MODEL)
    return pl.pallas_call(
        _sample_proj_kernel,
        grid=(t // tile,),
        in_specs=[row(D_MODEL), row(DK // 2), row(DK // 2),
                  _resident(g_mix.shape), _resident(g_gm.shape), _resident(w_in.shape)],
        out_specs=[row(w) for w in widths],
        out_shape=[jax.ShapeDtypeStruct((t, w), F32) for w in widths],
        compiler_params=pltpu.CompilerParams(
            dimension_semantics=("arbitrary",), vmem_limit_bytes=VMEM_LIMIT),
        name="sample_proj",
    )(x, cos, sin, g_mix, g_gm, w_in)


def _sample_ret_body(q_ref, k_ref, v_ref, s_ref, dm_ref, ind_ref, std_ref, o_ref, snew_ref, *,
                     ch_decay, dec_seq):
    rows = lax.broadcasted_iota(jnp.int32, (SAMPLE_PAIR * dec_seq, 1), 0)
    for h in range(H_A):
        q = q_ref[:, h * DK:(h + 1) * DK]
        k = k_ref[:, h * DK:(h + 1) * DK]
        vb = v_ref[:, h * DV:(h + 1) * DV].astype(BF16)
        sc = _mm_nt(q.astype(BF16), k.astype(BF16)) * dm_ref[h]
        o = _mm(sc.astype(BF16), vb)
        qd = (q * ind_ref[h]).astype(BF16)
        kd = k * std_ref[h]
        for bb in range(SAMPLE_PAIR):
            mine = (rows >= bb * dec_seq) & (rows < (bb + 1) * dec_seq)
            s_old = s_ref[bb, h]
            o = o + jnp.where(mine, _mm(qd, s_old.astype(BF16)), 0.0)
            kd_b = jnp.where(mine, kd, 0.0).astype(BF16)
            snew_ref[bb, h] = s_old * ch_decay[h] + _mm_tn(kd_b, vb)
        o_ref[:, h * DV:(h + 1) * DV] = o


N_TAIL_IN = 9
N_RET_IN = 7


def _tail_ret_kernel(*refs, ch_decay, dec_seq):
    tail_in = refs[:N_TAIL_IN]
    ret_in = refs[N_TAIL_IN:N_TAIL_IN + N_RET_IN]
    y_ref, o_ref, snew_ref = refs[N_TAIL_IN + N_RET_IN:]
    _sample_ret_body(*ret_in, o_ref, snew_ref, ch_decay=ch_decay, dec_seq=dec_seq)
    _tail_body(*tail_in, y_ref)


def _tail_ret(h, p, g_mlp, g_ple, g_fin, w_up, w_down, w_pg, w_pp,
              q, k, v, state, dm, ind, std, ch_decay, dec_seq):
    t = h.shape[0]
    nb = state.shape[0]
    steps = nb // SAMPLE_PAIR
    tile = t // steps
    assert tile * steps == t and tile % 8 == 0
    rows = SAMPLE_PAIR * dec_seq
    row = lambda w: pl.BlockSpec((rows, w), lambda i: (i, 0))
    st = pl.BlockSpec((SAMPLE_PAIR, H_A, DK, DV), lambda i: (i, 0, 0, 0))
    return pl.pallas_call(
        functools.partial(_tail_ret_kernel, ch_decay=ch_decay, dec_seq=dec_seq),
        grid=(steps,),
        in_specs=_tail_specs(tile, g_mlp, g_ple, g_fin, w_up, w_down, w_pg, w_pp) + [
            row(Q_W), row(Q_W), row(V_W), st,
            _resident(dm.shape), _resident(ind.shape), _resident(std.shape)],
        out_specs=[pl.BlockSpec((tile, D_MODEL), lambda i: (i, 0)), row(V_W), st],
        out_shape=[jax.ShapeDtypeStruct((t, D_MODEL), F32),
                   jax.ShapeDtypeStruct((nb * dec_seq, V_W), F32),
                   jax.ShapeDtypeStruct(state.shape, F32)],
        compiler_params=pltpu.CompilerParams(
            dimension_semantics=("arbitrary",), vmem_limit_bytes=VMEM_LIMIT),
        name="tail_ret",
    )(h, p, g_mlp, g_ple, g_fin, w_up, w_down, w_pg, w_pp, q, k, v, state, dm, ind, std)


def _sample_merge_kernel(x_ref, o_ref, sg_ref, u_ref, gv_ref, ga_ref, gb_ref, gret_ref, coef_ref,
                         bias_ref, wa_ref, wb_ref, wo_ref, h_ref, *, dec_seq):
    t = x_ref.shape[0]
    o = o_ref[...]
    ya = []
    for h in range(H_A):
        cols = slice(h * DV, (h + 1) * DV)
        ya.append((sg_ref[:, cols] * _group_rms(o[:, cols], gret_ref[:, cols])).astype(BF16))
    ya = jnp.concatenate(ya, axis=-1)

    gv = gv_ref[...]
    sub = coef_ref.shape[1]
    mix = None
    for s in range(dec_seq):
        shifted = gv if s == 0 else pltpu.roll(gv, s, 0)
        term = shifted.reshape(t // sub, sub, GM_WIDTH) * coef_ref[s][None]
        mix = term if mix is None else mix + term
    mix = (mix + bias_ref[...][None]).reshape(t, GM_WIDTH)
    yb = (u_ref[...] * mix).astype(BF16)

    merged = ga_ref[...] * _mm(ya, wa_ref[...]) + gb_ref[...] * _mm(yb, wb_ref[...])
    h_ref[...] = x_ref[...] + _mm(merged.astype(BF16), wo_ref[...])


def _sample_merge(x, o, sg, u, gv, ga, gb, g_ret, coef, bias, w_a, w_b, w_o, dec_seq):
    t = x.shape[0]
    tile = TILE
    row = lambda w: pl.BlockSpec((tile, w), lambda i: (i, 0))
    return pl.pallas_call(
        functools.partial(_sample_merge_kernel, dec_seq=dec_seq),
        grid=(t // tile,),
        in_specs=[row(D_MODEL), row(V_W), row(V_W), row(GM_WIDTH), row(GM_WIDTH), row(D_MODEL),
                  row(D_MODEL), _resident(g_ret.shape), _resident(coef.shape), _resident(bias.shape),
                  _resident(w_a.shape), _resident(w_b.shape), _resident(w_o.shape)],
        out_specs=row(D_MODEL),
        out_shape=jax.ShapeDtypeStruct((t, D_MODEL), F32),
        compiler_params=pltpu.CompilerParams(
            dimension_semantics=("arbitrary",), vmem_limit_bytes=VMEM_LIMIT),
        name="sample_merge",
    )(x, o, sg, u, gv, ga, gb, g_ret, coef, bias, w_a, w_b, w_o)


def _rope_tables(pos):
    half = DK // 2
    inv = ROPE_BASE ** (-jnp.arange(half, dtype=F32) / half)
    ang = pos[:, None] * inv[None, :]
    return jnp.cos(ang), jnp.sin(ang)


def _decay_tables(c):
    lg = jnp.log(1.0 - 2.0 ** (-5.0 - jnp.arange(H_A, dtype=F32)))
    idx = jnp.arange(c, dtype=F32)
    diff = idx[:, None] - idx[None, :]
    dmat = jnp.where(diff[None] >= 0, jnp.exp(jnp.maximum(diff, 0.0)[None] * lg[:, None, None]), 0.0)
    in_decay = jnp.exp((idx[None, :] + 1.0) * lg[:, None])
    st_decay = jnp.exp((c - 1.0 - idx[None, :]) * lg[:, None])
    return dmat, in_decay, st_decay


def _chunk_decay(c):
    return tuple(float((1.0 - 2.0 ** (-5.0 - h)) ** c) for h in range(H_A))


def kernel(x_prompt, x_sample, p_prompt, p_sample, state_ret, g_mix, w_in, g_ret, w_s, b_s, g_gm,
           w_br_a, w_br_b, w_o, g_mlp, w_up, w_down, g_ple, w_pg, w_pp, g_final):
    depth = w_in.shape[0]
    assert depth == 1
    batch, seq, _ = x_prompt.shape
    dec_batch, dec_seq, _ = x_sample.shape
    i = 0

    w_in_b = w_in[i].astype(BF16)
    w_a_b = w_br_a[i].astype(BF16)
    w_b_b = w_br_b[i].astype(BF16)
    w_o_b = w_o[i].astype(BF16)
    w_up_b = w_up[i].astype(BF16)
    w_down_b = w_down[i].astype(BF16)
    w_pg_b = w_pg[i].astype(BF16)
    w_pp_b = w_pp[i].astype(BF16)
    gmix = g_mix[i][None]
    gret = g_ret[i][None]
    ggm = g_gm[i][None]
    gmlp = g_mlp[i][None]
    gple = g_ple[i][None]
    gfin = g_final[None]

    cos_p, sin_p = _rope_tables(jnp.arange(seq, dtype=F32))
    dmat, in_dec, st_dec = _decay_tables(TILE)
    ind_p = jnp.broadcast_to(in_dec[:, :, None], (H_A, TILE, DK))
    std_p = jnp.broadcast_to(st_dec[:, :, None], (H_A, TILE, DK))
    tril = jnp.tril(jnp.ones((GM_CHUNK, GM_CHUNK), dtype=bool))
    ws_tril = jnp.where(tril[None], w_s[i], 0.0).astype(BF16)
    bias_p = jnp.repeat(jnp.transpose(b_s[i]), GM_GC, axis=1)
    h_p, s_p = _prompt_mix(x_prompt, cos_p, sin_p, dmat, ind_p, std_p, _chunk_decay(TILE),
                           gmix, gret, ggm, w_in_b, ws_tril, bias_p, w_a_b, w_b_b, w_o_b)

    ts = dec_batch * dec_seq
    xs = x_sample.reshape(ts, D_MODEL)
    pos_s = PAST_LEN + jnp.arange(dec_seq, dtype=F32)
    cos_s, sin_s = _rope_tables(pos_s)
    cos_s = jnp.tile(cos_s, (dec_batch, 1))
    sin_s = jnp.tile(sin_s, (dec_batch, 1))
    q_s, k_s, v_s, sg_s, u_s, gv_s, ga_s, gb_s = _sample_proj(xs, cos_s, sin_s, gmix, ggm, w_in_b)

    dm4, in4, st4 = _decay_tables(dec_seq)
    pair_rows = SAMPLE_PAIR * dec_seq
    same = (jnp.arange(pair_rows)[:, None] // dec_seq) == (jnp.arange(pair_rows)[None, :] // dec_seq)
    dm_pair = jnp.where(same[None], jnp.tile(dm4, (1, SAMPLE_PAIR, SAMPLE_PAIR)), 0.0)
    ind_pair = jnp.broadcast_to(jnp.tile(in4, (1, SAMPLE_PAIR))[:, :, None], (H_A, pair_rows, DK))
    std_pair = jnp.broadcast_to(jnp.tile(st4, (1, SAMPLE_PAIR))[:, :, None], (H_A, pair_rows, DK))
    y_p, o_s, s_s = _tail_ret(
        h_p.reshape(batch * seq, D_MODEL), p_prompt[i].reshape(batch * seq, PLE_DIM),
        gmlp, gple, gfin, w_up_b, w_down_b, w_pg_b, w_pp_b,
        q_s, k_s, v_s, state_ret[i], dm_pair, ind_pair, std_pair, _chunk_decay(dec_seq), dec_seq)

    sub = 2 * dec_seq
    w_small = jnp.where(tril[None, :dec_seq, :dec_seq], w_s[i][:, :dec_seq, :dec_seq], 0.0)
    tpos = jnp.arange(sub) % dec_seq
    coef = []
    for s in range(dec_seq):
        src = tpos - s
        c = jnp.where((src >= 0)[None, :], w_small[:, tpos, jnp.maximum(src, 0)], 0.0)
        coef.append(jnp.repeat(jnp.transpose(c), GM_GC, axis=1))
    coef = jnp.stack(coef, axis=0)
    bias_s = jnp.repeat(jnp.transpose(b_s[i][:, tpos]), GM_GC, axis=1)
    h_s = _sample_merge(xs, o_s, sg_s, u_s, gv_s, ga_s, gb_s, gret, coef, bias_s,
                        w_a_b, w_b_b, w_o_b, dec_seq)
    y_s = _tail(h_s, p_sample[i].reshape(ts, PLE_DIM), gmlp, gple, gfin, w_up_b, w_down_b, w_pg_b, w_pp_b)

    return (y_p.reshape(batch, seq, D_MODEL),
            y_s.reshape(dec_batch, dec_seq, D_MODEL),
            s_p[None],
            s_s[None],
            gv_s.reshape(dec_batch, dec_seq, GM_WIDTH)[None])
```

```python
import functools

import jax
import jax.numpy as jnp
from jax import lax
from jax.experimental import pallas as pl
from jax.experimental.pallas import tpu as pltpu

D_MODEL = 1024
H_A = 4
DK = 256
DV = 512
Q_W = H_A * DK
V_W = H_A * DV
GM_WIDTH = 1024
GM_GROUPS = 8
GM_GC = GM_WIDTH // GM_GROUPS
GM_CHUNK = 128
D_FF = 4 * D_MODEL
PLE_DIM = 256
EPS = 1e-6
ROPE_BASE = 10000.0
PAST_LEN = 16384

OFF_Q = 0
OFF_K = OFF_Q + Q_W
OFF_V = OFF_K + Q_W
OFF_G = OFF_V + V_W
OFF_U = OFF_G + V_W
OFF_GV = OFF_U + GM_WIDTH
OFF_GA = OFF_GV + GM_WIDTH
OFF_GB = OFF_GA + D_MODEL
IN_W = OFF_GB + D_MODEL

TILE = 256
MLP_TILE = 512
SAMPLE_PAIR = 2
VMEM_LIMIT = 56 * 1024 * 1024

BF16 = jnp.bfloat16
F32 = jnp.float32


def _mm(a, b):
    return jnp.dot(a, b, preferred_element_type=F32)


def _mm_nt(a, b):
    return lax.dot_general(a, b, (((1,), (1,)), ((), ())), preferred_element_type=F32)


def _mm_tn(a, b):
    return lax.dot_general(a, b, (((0,), (0,)), ((), ())), preferred_element_type=F32)


def _rms(x, g):
    return x * lax.rsqrt(jnp.mean(x * x, axis=-1, keepdims=True) + EPS) * g


def _sigmoid(x):
    return 1.0 / (1.0 + jnp.exp(-x))


def _gelu(x):
    return 0.5 * x * (1.0 + lax.erf(x * (2.0 ** -0.5)))


def _layernorm(x, g):
    mu = jnp.mean(x, axis=-1, keepdims=True)
    xc = x - mu
    return xc * lax.rsqrt(jnp.mean(xc * xc, axis=-1, keepdims=True) + EPS) * g


def _rotary(z, cos, sin):
    half = DK // 2
    z1, z2 = z[:, :half], z[:, half:]
    return jnp.concatenate([z1 * cos - z2 * sin, z2 * cos + z1 * sin], axis=-1)


def _group_rms(o, g):
    return o * lax.rsqrt(jnp.mean(o * o, axis=-1, keepdims=True) + EPS) * g


def _prompt_mix_kernel(x_ref, cos_ref, sin_ref, dmat_ref, ind_ref, std_ref, gmix_ref, gret_ref,
                       ggm_ref, win_ref, ws_ref, bias_ref, wa_ref, wb_ref, wo_ref,
                       h_ref, s_ref, ya_ref, yb_ref, *, ch_decay):
    j = pl.program_id(1)

    @pl.when(j == 0)
    def _():
        s_ref[...] = jnp.zeros_like(s_ref)

    x = x_ref[0]
    n = _rms(x, gmix_ref[...]).astype(BF16)
    cos = cos_ref[...]
    sin = sin_ref[...]

    u = _gelu(_mm(n, win_ref[:, OFF_U:OFF_U + GM_WIDTH]))
    gv = _layernorm(_gelu(_mm(n, win_ref[:, OFF_GV:OFF_GV + GM_WIDTH])), ggm_ref[...])
    gvb = gv.astype(BF16)

    zq = _mm(n, win_ref[:, OFF_Q:OFF_Q + Q_W])
    zk = _mm(n, win_ref[:, OFF_K:OFF_K + Q_W])
    zv = _mm(n, win_ref[:, OFF_V:OFF_V + V_W]).astype(BF16)
    zg = _mm(n, win_ref[:, OFF_G:OFF_G + V_W])

    for c in range(TILE // GM_CHUNK):
        rows = slice(c * GM_CHUNK, (c + 1) * GM_CHUNK)
        for g in range(GM_GROUPS):
            cols = slice(g * GM_GC, (g + 1) * GM_GC)
            sg = _mm(ws_ref[g], gvb[rows, cols]) + bias_ref[:, cols]
            yb_ref[rows, cols] = (u[rows, cols] * sg).astype(BF16)

    for h in range(H_A):
        qr = _rotary(zq[:, h * DK:(h + 1) * DK], cos, sin)
        kr = _rotary(zk[:, h * DK:(h + 1) * DK], cos, sin) * (DK ** -0.5)
        vb = zv[:, h * DV:(h + 1) * DV]
        sc = _mm_nt(qr.astype(BF16), kr.astype(BF16)) * dmat_ref[h]
        s_old = s_ref[0, h]
        o = _mm(sc.astype(BF16), vb) + _mm((qr * ind_ref[h]).astype(BF16), s_old.astype(BF16))
        s_ref[0, h] = s_old * ch_decay[h] + _mm_tn((kr * std_ref[h]).astype(BF16), vb)
        on = _group_rms(o, gret_ref[:, h * DV:(h + 1) * DV])
        zg_h = zg[:, h * DV:(h + 1) * DV]
        ya_ref[:, h * DV:(h + 1) * DV] = (zg_h * _sigmoid(zg_h) * on).astype(BF16)

    ga =_sigmoid(_mm(n, win_ref[:, OFF_GA:OFF_GA + D_MODEL]))
    gb = _sigmoid(_mm(n, win_ref[:, OFF_GB:OFF_GB + D_MODEL]))
    merged = ga * _mm(ya_ref[...], wa_ref[...]) + gb * _mm(yb_ref[...], wb_ref[...])
    h_ref[0] = x + _mm(merged.astype(BF16), wo_ref[...])


def _resident(shape):
    zeros = (0,) * len(shape)
    return pl.BlockSpec(shape, lambda *_: zeros, pipeline_mode=pl.Buffered(1))


def _prompt_mix(x, cos, sin, dmat, ind, std, ch_decay, g_mix, g_ret, g_gm, w_in, ws_tril, bias, w_a, w_b, w_o):
    b, L, _ = x.shape
    steps = L // TILE
    return pl.pallas_call(
        functools.partial(_prompt_mix_kernel, ch_decay=ch_decay),
        grid=(b, steps),
        in_specs=[
            pl.BlockSpec((1, TILE, D_MODEL), lambda i, j: (i, j, 0)),
            pl.BlockSpec((TILE, DK // 2), lambda i, j: (j, 0)),
            pl.BlockSpec((TILE, DK // 2), lambda i, j: (j, 0)),
            _resident(dmat.shape),
            _resident(ind.shape),
            _resident(std.shape),
            _resident(g_mix.shape),
            _resident(g_ret.shape),
            _resident(g_gm.shape),
            _resident(w_in.shape),
            _resident(ws_tril.shape),
            _resident(bias.shape),
            _resident(w_a.shape),
            _resident(w_b.shape),
            _resident(w_o.shape),
        ],
        out_specs=[
            pl.BlockSpec((1, TILE, D_MODEL), lambda i, j: (i, j, 0)),
            pl.BlockSpec((1, H_A, DK, DV), lambda i, j: (i, 0, 0, 0)),
        ],
        out_shape=[
            jax.ShapeDtypeStruct((b, L, D_MODEL), F32),
            jax.ShapeDtypeStruct((b, H_A, DK, DV), F32),
        ],
        scratch_shapes=[
            pltpu.VMEM((TILE, V_W), BF16),
            pltpu.VMEM((TILE, GM_WIDTH), BF16),
        ],
        compiler_params=pltpu.CompilerParams(
            dimension_semantics=("arbitrary", "arbitrary"), vmem_limit_bytes=VMEM_LIMIT),
        name="prompt_mix",
    )(x, cos, sin, dmat, ind, std, g_mix, g_ret, g_gm, w_in, ws_tril, bias, w_a, w_b, w_o)


def _tail_body(h_ref, p_ref, gmlp_ref, gple_ref, gfin_ref, wup_ref, wdown_ref, wpg_ref, wpp_ref, y_ref,
               side_work=None):
    emb = _mm(p_ref[...].astype(BF16), wpp_ref[...])
    h = h_ref[...]
    n2 = _rms(h, gmlp_ref[...]).astype(BF16)
    a = jnp.maximum(_mm(n2, wup_ref[...]), 0.0)
    h = h + _mm((a * a).astype(BF16), wdown_ref[...])
    if side_work is not None:
        side_work()
    n3 = _rms(h, gple_ref[...]).astype(BF16)
    gate = _sigmoid(_mm(n3, wpg_ref[...]))
    h = h + gate * emb
    y_ref[...] = _rms(h, gfin_ref[...])


def _tail_kernel(*refs):
    _tail_body(*refs)


def _tail_specs(tile, g_mlp, g_ple, g_fin, w_up, w_down, w_pg, w_pp):
    return [
        pl.BlockSpec((tile, D_MODEL), lambda i: (i, 0)),
        pl.BlockSpec((tile, PLE_DIM), lambda i: (i, 0)),
        _resident(g_mlp.shape),
        _resident(g_ple.shape),
        _resident(g_fin.shape),
        _resident(w_up.shape),
        _resident(w_down.shape),
        _resident(w_pg.shape),
        _resident(w_pp.shape),
    ]


def _tail(h, p, g_mlp, g_ple, g_fin, w_up, w_down, w_pg, w_pp):
    t = h.shape[0]
    tile = min(MLP_TILE, t)
    return pl.pallas_call(
        _tail_kernel,
        grid=(t // tile,),
        in_specs=_tail_specs(tile, g_mlp, g_ple, g_fin, w_up, w_down, w_pg, w_pp),
        out_specs=pl.BlockSpec((tile, D_MODEL), lambda i: (i, 0)),
        out_shape=jax.ShapeDtypeStruct((t, D_MODEL), F32),
        compiler_params=pltpu.CompilerParams(
            dimension_semantics=("arbitrary",), vmem_limit_bytes=VMEM_LIMIT),
        name="tail",
    )(h, p, g_mlp, g_ple, g_fin, w_up, w_down, w_pg, w_pp)


def _sample_proj_kernel(x_ref, cos_ref, sin_ref, gmix_ref, ggm_ref, win_ref,
                        q_ref, k_ref, v_ref, sg_ref, u_ref, gv_ref, ga_ref, gb_ref):
    n = _rms(x_ref[...], gmix_ref[...]).astype(BF16)
    cos = cos_ref[...]
    sin = sin_ref[...]
    zq = _mm(n, win_ref[:, OFF_Q:OFF_Q + Q_W])
    zk = _mm(n, win_ref[:, OFF_K:OFF_K + Q_W])
    for h in range(H_A):
        q_ref[:, h * DK:(h + 1) * DK] = _rotary(zq[:, h * DK:(h + 1) * DK], cos, sin)
        k_ref[:, h * DK:(h + 1) * DK] = _rotary(zk[:, h * DK:(h + 1) * DK], cos, sin) * (DK ** -0.5)
    v_ref[...] = _mm(n, win_ref[:, OFF_V:OFF_V + V_W])
    zg = _mm(n, win_ref[:, OFF_G:OFF_G + V_W])
    sg_ref[...] = zg * _sigmoid(zg)
    u_ref[...] = _gelu(_mm(n, win_ref[:, OFF_U:OFF_U + GM_WIDTH]))
    gv_ref[...] = _layernorm(_gelu(_mm(n, win_ref[:, OFF_GV:OFF_GV + GM_WIDTH])), ggm_ref[...])
    ga_ref[...] = _sigmoid(_mm(n, win_ref[:, OFF_GA:OFF_GA + D_MODEL]))
    gb_ref[...] = _sigmoid(_mm(n, win_ref[:, OFF_GB:OFF_GB + D_MODEL]))


def _sample_proj(x, cos, sin, g_mix, g_gm, w_in):
    t = x.shape[0]
    tile = TILE
    row = lambda w: pl.BlockSpec((tile, w), lambda i: (i, 0))
    widths = (Q_W, Q_W, V_W, V_W, GM_WIDTH, GM_WIDTH, D_MODEL, D_MODEL)
    return pl.pallas_call(
        _sample_proj_kernel,
        grid=(t // tile,),
        in_specs=[row(D_MODEL), row(DK // 2), row(DK // 2),
                  _resident(g_mix.shape), _resident(g_gm.shape), _resident(w_in.shape)],
        out_specs=[row(w) for w in widths],
        out_shape=[jax.ShapeDtypeStruct((t, w), F32) for w in widths],
        compiler_params=pltpu.CompilerParams(
            dimension_semantics=("arbitrary",), vmem_limit_bytes=VMEM_LIMIT),
        name="sample_proj",
    )(x, cos, sin, g_mix, g_gm, w_in)


def _sample_ret_body(q_ref, k_ref, v_ref, s_ref, dm_ref, ind_ref, std_ref, o_ref, snew_ref, *,
                     ch_decay, dec_seq):
    rows = lax.broadcasted_iota(jnp.int32, (SAMPLE_PAIR * dec_seq, 1), 0)
    for h in range(H_A):
        q = q_ref[:, h * DK:(h + 1) * DK]
        k = k_ref[:, h * DK:(h + 1) * DK]
        vb = v_ref[:, h * DV:(h + 1) * DV].astype(BF16)
        sc = _mm_nt(q.astype(BF16), k.astype(BF16)) * dm_ref[h]
        o = _mm(sc.astype(BF16), vb)
        qd = (q * ind_ref[h]).astype(BF16)
        kd = k * std_ref[h]
        for bb in range(SAMPLE_PAIR):
            mine = (rows >= bb * dec_seq) & (rows < (bb + 1) * dec_seq)
            s_old = s_ref[bb, h]
            o = o + jnp.where(mine, _mm(qd, s_old.astype(BF16)), 0.0)
            kd_b = jnp.where(mine, kd, 0.0).astype(BF16)
            snew_ref[bb, h] = s_old * ch_decay[h] + _mm_tn(kd_b, vb)
        o_ref[:, h * DV:(h + 1) * DV] = o


N_TAIL_IN = 9
N_RET_IN = 7


def _tail_ret_kernel(*refs, ch_decay, dec_seq):
    tail_in = refs[:N_TAIL_IN]
    ret_in = refs[N_TAIL_IN:N_TAIL_IN + N_RET_IN]
    y_ref, o_ref, snew_ref = refs[N_TAIL_IN + N_RET_IN:]
    side_work = functools.partial(_sample_ret_body, *ret_in, o_ref, snew_ref,
                                  ch_decay=ch_decay, dec_seq=dec_seq)
    _tail_body(*tail_in, y_ref, side_work=side_work)


def _tail_ret(h, p, g_mlp, g_ple, g_fin, w_up, w_down, w_pg, w_pp,
              q, k, v, state, dm, ind, std, ch_decay, dec_seq):
    t = h.shape[0]
    nb = state.shape[0]
    steps = nb // SAMPLE_PAIR
    tile = t // steps
    assert tile * steps == t and tile % 8 == 0
    rows = SAMPLE_PAIR * dec_seq
    row = lambda w: pl.BlockSpec((rows, w), lambda i: (i, 0))
    st = pl.BlockSpec((SAMPLE_PAIR, H_A, DK, DV), lambda i: (i, 0, 0, 0))
    return pl.pallas_call(
        functools.partial(_tail_ret_kernel, ch_decay=ch_decay, dec_seq=dec_seq),
        grid=(steps,),
        in_specs=_tail_specs(tile, g_mlp, g_ple, g_fin, w_up, w_down, w_pg, w_pp) + [
            row(Q_W), row(Q_W), row(V_W), st,
            _resident(dm.shape), _resident(ind.shape), _resident(std.shape)],
        out_specs=[pl.BlockSpec((tile, D_MODEL), lambda i: (i, 0)), row(V_W), st],
        out_shape=[jax.ShapeDtypeStruct((t, D_MODEL), F32),
                   jax.ShapeDtypeStruct((nb * dec_seq, V_W), F32),
                   jax.ShapeDtypeStruct(state.shape, F32)],
        compiler_params=pltpu.CompilerParams(
            dimension_semantics=("arbitrary",), vmem_limit_bytes=VMEM_LIMIT),
        name="tail_ret",
    )(h, p, g_mlp, g_ple, g_fin, w_up, w_down, w_pg, w_pp, q, k, v, state, dm, ind, std)


def _sample_merge_kernel(x_ref, o_ref, sg_ref, u_ref, gv_ref, ga_ref, gb_ref, gret_ref, coef_ref,
                         bias_ref, wa_ref, wb_ref, wo_ref, h_ref, *, dec_seq):
    t = x_ref.shape[0]
    o = o_ref[...]
    ya = []
    for h in range(H_A):
        cols = slice(h * DV, (h + 1) * DV)
        ya.append((sg_ref[:, cols] * _group_rms(o[:, cols], gret_ref[:, cols])).astype(BF16))
    ya = jnp.concatenate(ya, axis=-1)

    gv = gv_ref[...]
    sub = coef_ref.shape[1]
    mix = None
    for s in range(dec_seq):
        shifted = gv if s == 0 else pltpu.roll(gv, s, 0)
        term = shifted.reshape(t // sub, sub, GM_WIDTH) * coef_ref[s][None]
        mix = term if mix is None else mix + term
    mix = (mix + bias_ref[...][None]).reshape(t, GM_WIDTH)
    yb = (u_ref[...] * mix).astype(BF16)

    merged = ga_ref[...] * _mm(ya, wa_ref[...]) + gb_ref[...] * _mm(yb, wb_ref[...])
    h_ref[...] = x_ref[...] + _mm(merged.astype(BF16), wo_ref[...])


def _sample_merge(x, o, sg, u, gv, ga, gb, g_ret, coef, bias, w_a, w_b, w_o, dec_seq):
    t = x.shape[0]
    tile = TILE
    row = lambda w: pl.BlockSpec((tile, w), lambda i: (i, 0))
    return pl.pallas_call(
        functools.partial(_sample_merge_kernel, dec_seq=dec_seq),
        grid=(t // tile,),
        in_specs=[row(D_MODEL), row(V_W), row(V_W), row(GM_WIDTH), row(GM_WIDTH), row(D_MODEL),
                  row(D_MODEL), _resident(g_ret.shape), _resident(coef.shape), _resident(bias.shape),
                  _resident(w_a.shape), _resident(w_b.shape), _resident(w_o.shape)],
        out_specs=row(D_MODEL),
        out_shape=jax.ShapeDtypeStruct((t, D_MODEL), F32),
        compiler_params=pltpu.CompilerParams(
            dimension_semantics=("arbitrary",), vmem_limit_bytes=VMEM_LIMIT),
        name="sample_merge",
    )(x, o, sg, u, gv, ga, gb, g_ret, coef, bias, w_a, w_b, w_o)


def _rope_tables(pos):
    half = DK // 2
    inv = ROPE_BASE ** (-jnp.arange(half, dtype=F32) / half)
    ang = pos[:, None] * inv[None, :]
    return jnp.cos(ang), jnp.sin(ang)


def _decay_tables(c):
    lg = jnp.log(1.0 - 2.0 ** (-5.0 - jnp.arange(H_A, dtype=F32)))
    idx = jnp.arange(c, dtype=F32)
    diff = idx[:, None] - idx[None, :]
    dmat = jnp.where(diff[None] >= 0, jnp.exp(jnp.maximum(diff, 0.0)[None] * lg[:, None, None]), 0.0)
    in_decay = jnp.exp((idx[None, :] + 1.0) * lg[:, None])
    st_decay = jnp.exp((c - 1.0 - idx[None, :]) * lg[:, None])
    return dmat, in_decay, st_decay


def _chunk_decay(c):
    return tuple(float((1.0 - 2.0 ** (-5.0 - h)) ** c) for h in range(H_A))


def kernel(x_prompt, x_sample, p_prompt, p_sample, state_ret, g_mix, w_in, g_ret, w_s, b_s, g_gm,
           w_br_a, w_br_b, w_o, g_mlp, w_up, w_down, g_ple, w_pg, w_pp, g_final):
    depth = w_in.shape[0]
    assert depth == 1
    batch, seq, _ = x_prompt.shape
    dec_batch, dec_seq, _ = x_sample.shape
    i = 0

    w_in_b = w_in[i].astype(BF16)
    w_a_b = w_br_a[i].astype(BF16)
    w_b_b = w_br_b[i].astype(BF16)
    w_o_b = w_o[i].astype(BF16)
    w_up_b = w_up[i].astype(BF16)
    w_down_b = w_down[i].astype(BF16)
    w_pg_b = w_pg[i].astype(BF16)
    w_pp_b = w_pp[i].astype(BF16)
    gmix = g_mix[i][None]
    gret = g_ret[i][None]
    ggm = g_gm[i][None]
    gmlp = g_mlp[i][None]
    gple = g_ple[i][None]
    gfin = g_final[None]

    cos_p, sin_p = _rope_tables(jnp.arange(seq, dtype=F32))
    dmat, in_dec, st_dec = _decay_tables(TILE)
    ind_p = jnp.broadcast_to(in_dec[:, :, None], (H_A, TILE, DK))
    std_p = jnp.broadcast_to(st_dec[:, :, None], (H_A, TILE, DK))
    tril = jnp.tril(jnp.ones((GM_CHUNK, GM_CHUNK), dtype=bool))
    ws_tril = jnp.where(tril[None], w_s[i], 0.0).astype(BF16)
    bias_p = jnp.repeat(jnp.transpose(b_s[i]), GM_GC, axis=1)
    h_p, s_p = _prompt_mix(x_prompt, cos_p, sin_p, dmat, ind_p, std_p, _chunk_decay(TILE),
                           gmix, gret, ggm, w_in_b, ws_tril, bias_p, w_a_b, w_b_b, w_o_b)

    ts = dec_batch * dec_seq
    xs = x_sample.reshape(ts, D_MODEL)
    pos_s = PAST_LEN + jnp.arange(dec_seq, dtype=F32)
    cos_s, sin_s = _rope_tables(pos_s)
    cos_s = jnp.tile(cos_s, (dec_batch, 1))
    sin_s = jnp.tile(sin_s, (dec_batch, 1))
    q_s, k_s, v_s, sg_s, u_s, gv_s, ga_s, gb_s = _sample_proj(xs, cos_s, sin_s, gmix, ggm, w_in_b)

    dm4, in4, st4 = _decay_tables(dec_seq)
    pair_rows = SAMPLE_PAIR * dec_seq
    same = (jnp.arange(pair_rows)[:, None] // dec_seq) == (jnp.arange(pair_rows)[None, :] // dec_seq)
    dm_pair = jnp.where(same[None], jnp.tile(dm4, (1, SAMPLE_PAIR, SAMPLE_PAIR)), 0.0)
    ind_pair = jnp.broadcast_to(jnp.tile(in4, (1, SAMPLE_PAIR))[:, :, None], (H_A, pair_rows, DK))
    std_pair = jnp.broadcast_to(jnp.tile(st4, (1, SAMPLE_PAIR))[:, :, None], (H_A, pair_rows, DK))
    y_p, o_s, s_s = _tail_ret(
        h_p.reshape(batch * seq, D_MODEL), p_prompt[i].reshape(batch * seq, PLE_DIM),
        gmlp, gple, gfin, w_up_b, w_down_b, w_pg_b, w_pp_b,
        q_s, k_s, v_s, state_ret[i], dm_pair, ind_pair, std_pair, _chunk_decay(dec_seq), dec_seq)

    sub = 2 * dec_seq
    w_small = jnp.where(tril[None, :dec_seq, :dec_seq], w_s[i][:, :dec_seq, :dec_seq], 0.0)
    tpos = jnp.arange(sub) % dec_seq
    coef = []
    for s in range(dec_seq):
        src = tpos - s
        c = jnp.where((src >= 0)[None, :], w_small[:, tpos, jnp.maximum(src, 0)], 0.0)
        coef.append(jnp.repeat(jnp.transpose(c), GM_GC, axis=1))
    coef = jnp.stack(coef, axis=0)
    bias_s = jnp.repeat(jnp.transpose(b_s[i][:, tpos]), GM_GC, axis=1)
    h_s = _sample_merge(xs, o_s, sg_s, u_s, gv_s, ga_s, gb_s, gret, coef, bias_s,
                        w_a_b, w_b_b, w_o_b, dec_seq)
    y_s = _tail(h_s, p_sample[i].reshape(ts, PLE_DIM), gmlp, gple, gfin, w_up_b, w_down_b, w_pg_b, w_pp_b)

    return (y_p.reshape(batch, seq, D_MODEL),
            y_s.reshape(dec_batch, dec_seq, D_MODEL),
            s_p[None],
            s_s[None],
            gv_s.reshape(dec_batch, dec_seq, GM_WIDTH)[None])
```

```python
import functools

import jax
import jax.numpy as jnp
from jax import lax
from jax.experimental import pallas as pl
from jax.experimental.pallas import tpu as pltpu

D_MODEL = 1024
H_A = 4
DK = 256
DV = 512
Q_W = H_A * DK
V_W = H_A * DV
GM_WIDTH = 1024
GM_GROUPS = 8
GM_GC = GM_WIDTH // GM_GROUPS
GM_CHUNK = 128
D_FF = 4 * D_MODEL
PLE_DIM = 256
EPS = 1e-6
ROPE_BASE = 10000.0
PAST_LEN = 16384

OFF_Q = 0
OFF_K = OFF_Q + Q_W
OFF_V = OFF_K + Q_W
OFF_G = OFF_V + V_W
OFF_U = OFF_G + V_W
OFF_GV = OFF_U + GM_WIDTH
OFF_GA = OFF_GV + GM_WIDTH
OFF_GB = OFF_GA + D_MODEL
IN_W = OFF_GB + D_MODEL
SECTION = 1024
SEC_K, SEC_V, SEC_G, SEC_U, SEC_GV, SEC_GA = (
    OFF_K // SECTION, OFF_V // SECTION, OFF_G // SECTION, OFF_U // SECTION, OFF_GV // SECTION,
    OFF_GA // SECTION)

TILE = 256
MLP_TILE = 512
SAMPLE_PAIR = 2
VMEM_LIMIT = 56 * 1024 * 1024

BF16 = jnp.bfloat16
F32 = jnp.float32


def _mm(a, b):
    return jnp.dot(a, b, preferred_element_type=F32)


def _mm_nt(a, b):
    return lax.dot_general(a, b, (((1,), (1,)), ((), ())), preferred_element_type=F32)


def _mm_tn(a, b):
    return lax.dot_general(a, b, (((0,), (0,)), ((), ())), preferred_element_type=F32)


def _rms(x, g):
    return x * lax.rsqrt(jnp.mean(x * x, axis=-1, keepdims=True) + EPS) * g


def _sigmoid(x):
    return 1.0 / (1.0 + jnp.exp(-x))


def _gelu(x):
    return 0.5 * x * (1.0 + lax.erf(x * (2.0 ** -0.5)))


def _layernorm(x, g):
    mu = jnp.mean(x, axis=-1, keepdims=True)
    xc = x - mu
    return xc * lax.rsqrt(jnp.mean(xc * xc, axis=-1, keepdims=True) + EPS) * g


def _rotary(z, cos, sin):
    half = DK // 2
    z1, z2 = z[:, :half], z[:, half:]
    return jnp.concatenate([z1 * cos - z2 * sin, z2 * cos + z1 * sin], axis=-1)


def _group_rms(o, g):
    return o * lax.rsqrt(jnp.mean(o * o, axis=-1, keepdims=True) + EPS) * g


def _prompt_mix_kernel(x_ref, cos_ref, sin_ref, dmat_ref, ind_ref, std_ref, gmix_ref, gret_ref,
                       ggm_ref, win_ref, ws_ref, bias_ref, wa_ref, wb_ref, wo_ref,
                       wup_f_ref, wdown_f_ref, wpg_f_ref,
                       h_ref, s_ref, wup_b_ref, wdown_b_ref, wpg_b_ref, ya_ref, yb_ref, *, ch_decay):
    j = pl.program_id(1)

    wup_b_ref[...] = wup_f_ref[...].astype(BF16)
    wdown_b_ref[...] = wdown_f_ref[...].astype(BF16)
    wpg_b_ref[...] = wpg_f_ref[...].astype(BF16)

    @pl.when(j == 0)
    def _():
        s_ref[...] = jnp.zeros_like(s_ref)

    x = x_ref[0]
    n = _rms(x, gmix_ref[...]).astype(BF16)
    cos = cos_ref[...]
    sin = sin_ref[...]

    u = _gelu(_mm(n, win_ref[:, OFF_U:OFF_U + GM_WIDTH]))
    gv = _layernorm(_gelu(_mm(n, win_ref[:, OFF_GV:OFF_GV + GM_WIDTH])), ggm_ref[...])
    gvb = gv.astype(BF16)

    zq = _mm(n, win_ref[:, OFF_Q:OFF_Q + Q_W])
    zk = _mm(n, win_ref[:, OFF_K:OFF_K + Q_W])
    zg = _mm(n, win_ref[:, OFF_G:OFF_G + V_W])

    qb, kb, qd, kd = [], [], [], []
    for h in range(H_A):
        qr = _rotary(zq[:, h * DK:(h + 1) * DK], cos, sin)
        kr = _rotary(zk[:, h * DK:(h + 1) * DK], cos, sin)
        qb.append(qr.astype(BF16))
        kb.append(kr.astype(BF16))
        qd.append((qr * ind_ref[h]).astype(BF16))
        kd.append((kr * std_ref[h]).astype(BF16))

    zv = _mm(n, win_ref[:, OFF_V:OFF_V + V_W]).astype(BF16)
    out_gate = zg * _sigmoid(zg)

    for g in range(GM_GROUPS):
        cols = slice(g * GM_GC, (g + 1) * GM_GC)
        pair = jnp.concatenate([gvb[:GM_CHUNK, cols], gvb[GM_CHUNK:, cols]], axis=-1)
        sg = _mm(ws_ref[g], pair)
        for c in range(TILE // GM_CHUNK):
            rows = slice(c * GM_CHUNK, (c + 1) * GM_CHUNK)
            sg_c = sg[:, c * GM_GC:(c + 1) * GM_GC] + bias_ref[:, cols]
            yb_ref[rows, cols] = (u[rows, cols] * sg_c).astype(BF16)

    ga = gb = None
    for h in range(H_A):
        vb = zv[:, h * DV:(h + 1) * DV]
        sc = _mm_nt(qb[h], kb[h]) * dmat_ref[h]
        s_old = s_ref[0, h]
        o = _mm(sc.astype(BF16), vb) + _mm(qd[h], s_old.astype(BF16))
        s_ref[0, h] = s_old * ch_decay[h] + _mm_tn(kd[h], vb)
        on = _group_rms(o, gret_ref[:, h * DV:(h + 1) * DV])
        ya_ref[:, h * DV:(h + 1) * DV] = (out_gate[:, h * DV:(h + 1) * DV] * on).astype(BF16)
        if h == 1:
            ga = _sigmoid(_mm(n, win_ref[:, OFF_GA:OFF_GA + D_MODEL]))
        if h == 2:
            gb = _sigmoid(_mm(n, win_ref[:, OFF_GB:OFF_GB + D_MODEL]))
    merged = ga * _mm(ya_ref[...], wa_ref[...]) + gb * _mm(yb_ref[...], wb_ref[...])
    h_ref[0] = x + _mm(merged.astype(BF16), wo_ref[...])


def _resident(shape):
    zeros = (0,) * len(shape)
    return pl.BlockSpec(shape, lambda *_: zeros, pipeline_mode=pl.Buffered(1))


def _prompt_mix(x, cos, sin, dmat, ind, std, ch_decay, g_mix, g_ret, g_gm, w_in, ws_tril, bias, w_a, w_b, w_o,
                w_up, w_down, w_pg):
    b, L, _ = x.shape
    steps = L // TILE
    total = b * steps

    def slab(w):
        rows = w.shape[0] // total
        assert rows * total == w.shape[0] and rows % 16 == 0
        return pl.BlockSpec((rows, w.shape[1]), lambda i, j: (i * steps + j, 0))

    return pl.pallas_call(
        functools.partial(_prompt_mix_kernel, ch_decay=ch_decay),
        grid=(b, steps),
        in_specs=[
            pl.BlockSpec((1, TILE, D_MODEL), lambda i, j: (i, j, 0)),
            pl.BlockSpec((TILE, DK // 2), lambda i, j: (j, 0)),
            pl.BlockSpec((TILE, DK // 2), lambda i, j: (j, 0)),
            _resident(dmat.shape),
            _resident(ind.shape),
            _resident(std.shape),
            _resident(g_mix.shape),
            _resident(g_ret.shape),
            _resident(g_gm.shape),
            _resident(w_in.shape),
            _resident(ws_tril.shape),
            _resident(bias.shape),
            _resident(w_a.shape),
            _resident(w_b.shape),
            _resident(w_o.shape),
            slab(w_up),
            slab(w_down),
            slab(w_pg),
        ],
        out_specs=[
            pl.BlockSpec((1, TILE, D_MODEL), lambda i, j: (i, j, 0)),
            pl.BlockSpec((1, H_A, DK, DV), lambda i, j: (i, 0, 0, 0)),
            slab(w_up),
            slab(w_down),
            slab(w_pg),
        ],
        out_shape=[
            jax.ShapeDtypeStruct((b, L, D_MODEL), F32),
            jax.ShapeDtypeStruct((b, H_A, DK, DV), F32),
            jax.ShapeDtypeStruct(w_up.shape, BF16),
            jax.ShapeDtypeStruct(w_down.shape, BF16),
            jax.ShapeDtypeStruct(w_pg.shape, BF16),
        ],
        scratch_shapes=[
            pltpu.VMEM((TILE, V_W), BF16),
            pltpu.VMEM((TILE, GM_WIDTH), BF16),
        ],
        compiler_params=pltpu.CompilerParams(
            dimension_semantics=("arbitrary", "arbitrary"), vmem_limit_bytes=VMEM_LIMIT),
        name="prompt_mix",
    )(x, cos, sin, dmat, ind, std, g_mix, g_ret, g_gm, w_in, ws_tril, bias, w_a, w_b, w_o,
      w_up, w_down, w_pg)


def _tail_body(h_ref, p_ref, gmlp_ref, gple_ref, gfin_ref, wup_ref, wdown_ref, wpg_ref, wpp_ref, y_ref,
               side_work=None):
    emb = _mm(p_ref[...].astype(BF16), wpp_ref[...])
    h = h_ref[...]
    n2 = _rms(h, gmlp_ref[...]).astype(BF16)
    a = jnp.maximum(_mm(n2, wup_ref[...]), 0.0)
    h = h + _mm((a * a).astype(BF16), wdown_ref[...])
    if side_work is not None:
        side_work()
    n3 = _rms(h, gple_ref[...]).astype(BF16)
    gate = _sigmoid(_mm(n3, wpg_ref[...]))
    h = h + gate * emb
    y_ref[...] = _rms(h, gfin_ref[...])


def _tail_kernel(*refs):
    _tail_body(*refs)


def _tail_specs(tile, g_mlp, g_ple, g_fin, w_up, w_down, w_pg, w_pp):
    return [
        pl.BlockSpec((tile, D_MODEL), lambda i: (i, 0)),
        pl.BlockSpec((tile, PLE_DIM), lambda i: (i, 0)),
        _resident(g_mlp.shape),
        _resident(g_ple.shape),
        _resident(g_fin.shape),
        _resident(w_up.shape),
        _resident(w_down.shape),
        _resident(w_pg.shape),
        _resident(w_pp.shape),
    ]


def _tail(h, p, g_mlp, g_ple, g_fin, w_up, w_down, w_pg, w_pp):
    t = h.shape[0]
    tile = min(MLP_TILE, t)
    return pl.pallas_call(
        _tail_kernel,
        grid=(t // tile,),
        in_specs=_tail_specs(tile, g_mlp, g_ple, g_fin, w_up, w_down, w_pg, w_pp),
        out_specs=pl.BlockSpec((tile, D_MODEL), lambda i: (i, 0)),
        out_shape=jax.ShapeDtypeStruct((t, D_MODEL), F32),
        compiler_params=pltpu.CompilerParams(
            dimension_semantics=("arbitrary",), vmem_limit_bytes=VMEM_LIMIT),
        name="tail",
    )(h, p, g_mlp, g_ple, g_fin, w_up, w_down, w_pg, w_pp)


def _sample_proj_kernel(x_ref, cos_ref, sin_ref, gmix_ref, ggm_ref, win_f_ref,
                        z_ref, gv_ref, win_b_ref, n_ref):
    s = pl.program_id(0)

    @pl.when(s == 0)
    def _():
        n_ref[...] = _rms(x_ref[...], gmix_ref[...]).astype(BF16)

    scale = jnp.where(s == SEC_K, DK ** -0.5, 1.0).astype(F32)
    wb = (win_f_ref[...] * scale).astype(BF16)
    win_b_ref[...] = wb
    z = _mm(n_ref[...], wb)

    @pl.when(s < SEC_V)
    def _():
        cos = cos_ref[...]
        sin = sin_ref[...]
        for h in range(H_A):
            z_ref[:, h * DK:(h + 1) * DK] = _rotary(z[:, h * DK:(h + 1) * DK], cos, sin)

    @pl.when((s >= SEC_V) & (s < SEC_G))
    def _():
        z_ref[...] = z

    @pl.when((s >= SEC_G) & (s < SEC_U))
    def _():
        z_ref[...] = z * _sigmoid(z)

    @pl.when(s == SEC_U)
    def _():
        z_ref[...] = _gelu(z)

    @pl.when(s == SEC_GV)
    def _():
        gv = _layernorm(_gelu(z), ggm_ref[...])
        z_ref[...] = gv
        gv_ref[...] = gv

    @pl.when(s >= SEC_GA)
    def _():
        z_ref[...] = _sigmoid(z)


def _sample_proj(x, cos, sin, g_mix, g_gm, w_in):
    t = x.shape[0]
    section = pl.BlockSpec((D_MODEL, SECTION), lambda s: (0, s))
    return pl.pallas_call(
        _sample_proj_kernel,
        grid=(IN_W // SECTION,),
        in_specs=[_resident(x.shape), _resident(cos.shape), _resident(sin.shape),
                  _resident(g_mix.shape), _resident(g_gm.shape), section],
        out_specs=[pl.BlockSpec((t, SECTION), lambda s: (0, s)),
                   pl.BlockSpec((t, GM_WIDTH), lambda s: (0, 0)),
                   section],
        out_shape=[jax.ShapeDtypeStruct((t, IN_W), F32),
                   jax.ShapeDtypeStruct((t, GM_WIDTH), F32),
                   jax.ShapeDtypeStruct(w_in.shape, BF16)],
        scratch_shapes=[pltpu.VMEM((t, D_MODEL), BF16)],
        compiler_params=pltpu.CompilerParams(
            dimension_semantics=("arbitrary",), vmem_limit_bytes=VMEM_LIMIT),
        name="sample_proj",
    )(x, cos, sin, g_mix, g_gm, w_in)


def _sample_ret_body(q_ref, k_ref, v_ref, s_ref, dm_ref, ind_ref, std_ref, o_ref, snew_ref, *,
                     ch_decay, dec_seq):
    rows = lax.broadcasted_iota(jnp.int32, (SAMPLE_PAIR * dec_seq, 1), 0)
    for h in range(H_A):
        q = q_ref[:, h * DK:(h + 1) * DK]
        k = k_ref[:, h * DK:(h + 1) * DK]
        vb = v_ref[:, h * DV:(h + 1) * DV].astype(BF16)
        sc = _mm_nt(q.astype(BF16), k.astype(BF16)) * dm_ref[h]
        o = _mm(sc.astype(BF16), vb)
        qd = (q * ind_ref[h]).astype(BF16)
        kd = k * std_ref[h]
        for bb in range(SAMPLE_PAIR):
            mine = (rows >= bb * dec_seq) & (rows < (bb + 1) * dec_seq)
            s_old = s_ref[bb, h]
            o = o + jnp.where(mine, _mm(qd, s_old.astype(BF16)), 0.0)
            kd_b = jnp.where(mine, kd, 0.0).astype(BF16)
            snew_ref[bb, h] = s_old * ch_decay[h] + _mm_tn(kd_b, vb)
        o_ref[:, h * DV:(h + 1) * DV] = o


N_TAIL_IN = 9
N_RET_IN = 7


def _tail_ret_kernel(*refs, ch_decay, dec_seq):
    tail_in = refs[:N_TAIL_IN]
    ret_in = refs[N_TAIL_IN:N_TAIL_IN + N_RET_IN]
    y_ref, o_ref, snew_ref = refs[N_TAIL_IN + N_RET_IN:]
    side_work = functools.partial(_sample_ret_body, *ret_in, o_ref, snew_ref,
                                  ch_decay=ch_decay, dec_seq=dec_seq)
    _tail_body(*tail_in, y_ref, side_work=side_work)


def _tail_ret(h, p, g_mlp, g_ple, g_fin, w_up, w_down, w_pg, w_pp,
              z_s, state, dm, ind, std, ch_decay, dec_seq):
    t = h.shape[0]
    nb = state.shape[0]
    steps = nb // SAMPLE_PAIR
    tile = t // steps
    assert tile * steps == t and tile % 8 == 0
    rows = SAMPLE_PAIR * dec_seq
    cols = lambda off, w: pl.BlockSpec((rows, w), lambda i: (i, off // w))
    st = pl.BlockSpec((SAMPLE_PAIR, H_A, DK, DV), lambda i: (i, 0, 0, 0))
    return pl.pallas_call(
        functools.partial(_tail_ret_kernel, ch_decay=ch_decay, dec_seq=dec_seq),
        grid=(steps,),
        in_specs=_tail_specs(tile, g_mlp, g_ple, g_fin, w_up, w_down, w_pg, w_pp) + [
            cols(OFF_Q, Q_W), cols(OFF_K, Q_W), cols(OFF_V, V_W), st,
            _resident(dm.shape), _resident(ind.shape), _resident(std.shape)],
        out_specs=[pl.BlockSpec((tile, D_MODEL), lambda i: (i, 0)),
                   pl.BlockSpec((rows, V_W), lambda i: (i, 0)), st],
        out_shape=[jax.ShapeDtypeStruct((t, D_MODEL), F32),
                   jax.ShapeDtypeStruct((nb * dec_seq, V_W), F32),
                   jax.ShapeDtypeStruct(state.shape, F32)],
        compiler_params=pltpu.CompilerParams(
            dimension_semantics=("arbitrary",), vmem_limit_bytes=VMEM_LIMIT),
        name="tail_ret",
    )(h, p, g_mlp, g_ple, g_fin, w_up, w_down, w_pg, w_pp, z_s, z_s, z_s, state, dm, ind, std)


def _sample_merge_kernel(x_ref, o_ref, sg_ref, u_ref, gv_ref, ga_ref, gb_ref, gret_ref, coef_ref,
                         bias_ref, wa_ref, wb_ref, wo_ref, h_ref, *, dec_seq):
    t = x_ref.shape[0]
    o = o_ref[...]
    ya = []
    for h in range(H_A):
        cols = slice(h * DV, (h + 1) * DV)
        ya.append((sg_ref[:, cols] * _group_rms(o[:, cols], gret_ref[:, cols])).astype(BF16))
    ya = jnp.concatenate(ya, axis=-1)

    gv = gv_ref[...]
    sub = coef_ref.shape[1]
    mix = None
    for s in range(dec_seq):
        shifted = gv if s == 0 else pltpu.roll(gv, s, 0)
        term = shifted.reshape(t // sub, sub, GM_WIDTH) * coef_ref[s][None]
        mix = term if mix is None else mix + term
    mix = (mix + bias_ref[...][None]).reshape(t, GM_WIDTH)
    yb = (u_ref[...] * mix).astype(BF16)

    merged = ga_ref[...] * _mm(ya, wa_ref[...]) + gb_ref[...] * _mm(yb, wb_ref[...])
    h_ref[...] = x_ref[...] + _mm(merged.astype(BF16), wo_ref[...])


def _sample_merge(x, o, z_s, g_ret, coef, bias, w_a, w_b, w_o, dec_seq):
    t = x.shape[0]
    tile = TILE
    row = lambda w: pl.BlockSpec((tile, w), lambda i: (i, 0))
    cols = lambda off, w: pl.BlockSpec((tile, w), lambda i: (i, off // w))
    return pl.pallas_call(
        functools.partial(_sample_merge_kernel, dec_seq=dec_seq),
        grid=(t // tile,),
        in_specs=[row(D_MODEL), row(V_W), cols(OFF_G, V_W), cols(OFF_U, GM_WIDTH),
                  cols(OFF_GV, GM_WIDTH), cols(OFF_GA, D_MODEL), cols(OFF_GB, D_MODEL),
                  _resident(g_ret.shape), _resident(coef.shape), _resident(bias.shape),
                  _resident(w_a.shape), _resident(w_b.shape), _resident(w_o.shape)],
        out_specs=row(D_MODEL),
        out_shape=jax.ShapeDtypeStruct((t, D_MODEL), F32),
        compiler_params=pltpu.CompilerParams(
            dimension_semantics=("arbitrary",), vmem_limit_bytes=VMEM_LIMIT),
        name="sample_merge",
    )(x, o, z_s, z_s, z_s, z_s, z_s, g_ret, coef, bias, w_a, w_b, w_o)


def _rope_tables(pos):
    half = DK // 2
    inv = ROPE_BASE ** (-jnp.arange(half, dtype=F32) / half)
    ang = pos[:, None] * inv[None, :]
    return jnp.cos(ang), jnp.sin(ang)


def _decay_tables(c):
    lg = jnp.log(1.0 - 2.0 ** (-5.0 - jnp.arange(H_A, dtype=F32)))
    idx = jnp.arange(c, dtype=F32)
    diff = idx[:, None] - idx[None, :]
    dmat = jnp.where(diff[None] >= 0, jnp.exp(jnp.maximum(diff, 0.0)[None] * lg[:, None, None]), 0.0)
    in_decay = jnp.exp((idx[None, :] + 1.0) * lg[:, None])
    st_decay = jnp.exp((c - 1.0 - idx[None, :]) * lg[:, None])
    return dmat, in_decay, st_decay


def _chunk_decay(c):
    return tuple(float((1.0 - 2.0 ** (-5.0 - h)) ** c) for h in range(H_A))


def kernel(x_prompt, x_sample, p_prompt, p_sample, state_ret, g_mix, w_in, g_ret, w_s, b_s, g_gm,
           w_br_a, w_br_b, w_o, g_mlp, w_up, w_down, g_ple, w_pg, w_pp, g_final):
    depth = w_in.shape[0]
    assert depth == 1
    batch, seq, _ = x_prompt.shape
    dec_batch, dec_seq, _ = x_sample.shape
    i = 0

    w_a_b = w_br_a[i].astype(BF16)
    w_b_b = w_br_b[i].astype(BF16)
    w_o_b = w_o[i].astype(BF16)
    w_pp_b = w_pp[i].astype(BF16)
    gmix = g_mix[i][None]
    gret = g_ret[i][None]
    ggm = g_gm[i][None]
    gmlp = g_mlp[i][None]
    gple = g_ple[i][None]
    gfin = g_final[None]

    ts = dec_batch * dec_seq
    xs = x_sample.reshape(ts, D_MODEL)
    pos_s = PAST_LEN + jnp.arange(dec_seq, dtype=F32)
    cos_s, sin_s = _rope_tables(pos_s)
    cos_s = jnp.tile(cos_s, (dec_batch, 1))
    sin_s = jnp.tile(sin_s, (dec_batch, 1))
    z_s, gv_s, w_in_b = _sample_proj(xs, cos_s, sin_s, gmix, ggm, w_in[i])

    cos_p, sin_p = _rope_tables(jnp.arange(seq, dtype=F32))
    dmat, in_dec, st_dec = _decay_tables(TILE)
    ind_p = jnp.broadcast_to(in_dec[:, :, None], (H_A, TILE, DK))
    std_p = jnp.broadcast_to(st_dec[:, :, None], (H_A, TILE, DK))
    tril = jnp.tril(jnp.ones((GM_CHUNK, GM_CHUNK), dtype=bool))
    ws_tril = jnp.where(tril[None], w_s[i], 0.0).astype(BF16)
    bias_p = jnp.repeat(jnp.transpose(b_s[i]), GM_GC, axis=1)
    h_p, s_p, w_up_b, w_down_b, w_pg_b = _prompt_mix(
        x_prompt, cos_p, sin_p, dmat, ind_p, std_p, _chunk_decay(TILE),
        gmix, gret, ggm, w_in_b, ws_tril, bias_p, w_a_b, w_b_b, w_o_b, w_up[i], w_down[i], w_pg[i])

    dm4, in4, st4 = _decay_tables(dec_seq)
    pair_rows = SAMPLE_PAIR * dec_seq
    same = (jnp.arange(pair_rows)[:, None] // dec_seq) == (jnp.arange(pair_rows)[None, :] // dec_seq)
    dm_pair = jnp.where(same[None], jnp.tile(dm4, (1, SAMPLE_PAIR, SAMPLE_PAIR)), 0.0)
    ind_pair = jnp.broadcast_to(jnp.tile(in4, (1, SAMPLE_PAIR))[:, :, None], (H_A, pair_rows, DK))
    std_pair = jnp.broadcast_to(jnp.tile(st4, (1, SAMPLE_PAIR))[:, :, None], (H_A, pair_rows, DK))
    y_p, o_s, s_s = _tail_ret(
        h_p.reshape(batch * seq, D_MODEL), p_prompt[i].reshape(batch * seq, PLE_DIM),
        gmlp, gple, gfin, w_up_b, w_down_b, w_pg_b, w_pp_b,
        z_s, state_ret[i], dm_pair, ind_pair, std_pair, _chunk_decay(dec_seq), dec_seq)

    sub = 2 * dec_seq
    w_small = jnp.where(tril[None, :dec_seq, :dec_seq], w_s[i][:, :dec_seq, :dec_seq], 0.0)
    tpos = jnp.arange(sub) % dec_seq
    coef = []
    for s in range(dec_seq):
        src = tpos - s
        c = jnp.where((src >= 0)[None, :], w_small[:, tpos, jnp.maximum(src, 0)], 0.0)
        coef.append(jnp.repeat(jnp.transpose(c), GM_GC, axis=1))
    coef = jnp.stack(coef, axis=0)
    bias_s = jnp.repeat(jnp.transpose(b_s[i][:, tpos]), GM_GC, axis=1)
    h_s = _sample_merge(xs, o_s, z_s, gret, coef, bias_s, w_a_b, w_b_b, w_o_b, dec_seq)
    y_s = _tail(h_s, p_sample[i].reshape(ts, PLE_DIM), gmlp, gple, gfin, w_up_b, w_down_b, w_pg_b, w_pp_b)

    return (y_p.reshape(batch, seq, D_MODEL),
            y_s.reshape(dec_batch, dec_seq, D_MODEL),
            s_p[None],
            s_s[None],
            gv_s.reshape(dec_batch, dec_seq, GM_WIDTH)[None])
```

```python
import functools

import numpy as np
import jax
import jax.numpy as jnp
from jax import lax
from jax.experimental import pallas as pl
from jax.experimental.pallas import tpu as pltpu

D_MODEL = 1024
H_A = 4
DK = 256
DV = 512
Q_W = H_A * DK
V_W = H_A * DV
GM_WIDTH = 1024
GM_GROUPS = 8
GM_GC = GM_WIDTH // GM_GROUPS
GM_CHUNK = 128
D_FF = 4 * D_MODEL
PLE_DIM = 256
EPS = 1e-6
ROPE_BASE = 10000.0
PAST_LEN = 16384

OFF_Q = 0
OFF_K = OFF_Q + Q_W
OFF_V = OFF_K + Q_W
OFF_G = OFF_V + V_W
OFF_U = OFF_G + V_W
OFF_GV = OFF_U + GM_WIDTH
OFF_GA = OFF_GV + GM_WIDTH
OFF_GB = OFF_GA + D_MODEL
IN_W = OFF_GB + D_MODEL
SECTION = 1024
SEC_K, SEC_V, SEC_G, SEC_U, SEC_GV, SEC_GA = (
    OFF_K // SECTION, OFF_V // SECTION, OFF_G // SECTION, OFF_U // SECTION, OFF_GV // SECTION,
    OFF_GA // SECTION)

TILE = 256
SAMPLE_PAIR = 2
VMEM_LIMIT = 56 * 1024 * 1024

BF16 = jnp.bfloat16
F32 = jnp.float32


def _mm(a, b):
    return jnp.dot(a, b, preferred_element_type=F32)


def _mm_nt(a, b):
    return lax.dot_general(a, b, (((1,), (1,)), ((), ())), preferred_element_type=F32)


def _mm_tn(a, b):
    return lax.dot_general(a, b, (((0,), (0,)), ((), ())), preferred_element_type=F32)


def _rms(x, g):
    return x * lax.rsqrt(jnp.mean(x * x, axis=-1, keepdims=True) + EPS) * g


def _sigmoid(x):
    return 1.0 / (1.0 + jnp.exp(-x))


def _gelu(x):
    return 0.5 * x * (1.0 + lax.erf(x * (2.0 ** -0.5)))


def _layernorm(x, g):
    mu = jnp.mean(x, axis=-1, keepdims=True)
    xc = x - mu
    return xc * lax.rsqrt(jnp.mean(xc * xc, axis=-1, keepdims=True) + EPS) * g


def _rotary(z, cos, sin):
    half = DK // 2
    z1, z2 = z[:, :half], z[:, half:]
    return jnp.concatenate([z1 * cos - z2 * sin, z2 * cos + z1 * sin], axis=-1)


def _group_rms(o, g):
    return o * lax.rsqrt(jnp.mean(o * o, axis=-1, keepdims=True) + EPS) * g


def _prompt_mix_kernel(x_ref, cos_ref, sin_ref, dmat_ref, ind_ref, std_ref, gmix_ref, gret_ref,
                       ggm_ref, win_ref, ws_ref, bias_ref, wa_ref, wb_ref, wo_ref,
                       wup_f_ref, wdown_f_ref, wpg_f_ref,
                       h_ref, s_ref, wup_b_ref, wdown_b_ref, wpg_b_ref, ya_ref, yb_ref, *, ch_decay):
    j = pl.program_id(1)

    wup_b_ref[...] = wup_f_ref[...].astype(BF16)
    wdown_b_ref[...] = wdown_f_ref[...].astype(BF16)
    wpg_b_ref[...] = wpg_f_ref[...].astype(BF16)

    @pl.when(j == 0)
    def _():
        s_ref[...] = jnp.zeros_like(s_ref)

    x = x_ref[0]
    n = _rms(x, gmix_ref[...]).astype(BF16)
    cos = cos_ref[...]
    sin = sin_ref[...]

    u = _gelu(_mm(n, win_ref[:, OFF_U:OFF_U + GM_WIDTH]))
    gv = _layernorm(_gelu(_mm(n, win_ref[:, OFF_GV:OFF_GV + GM_WIDTH])), ggm_ref[...])
    gvb = gv.astype(BF16)

    zq = _mm(n, win_ref[:, OFF_Q:OFF_Q + Q_W])
    zk = _mm(n, win_ref[:, OFF_K:OFF_K + Q_W])
    zg = _mm(n, win_ref[:, OFF_G:OFF_G + V_W])

    qb, kb, qd, kd = [], [], [], []
    for h in range(H_A):
        qr = _rotary(zq[:, h * DK:(h + 1) * DK], cos, sin)
        kr = _rotary(zk[:, h * DK:(h + 1) * DK], cos, sin)
        qb.append(qr.astype(BF16))
        kb.append(kr.astype(BF16))
        qd.append((qr * ind_ref[h]).astype(BF16))
        kd.append((kr * std_ref[h]).astype(BF16))

    zv = _mm(n, win_ref[:, OFF_V:OFF_V + V_W]).astype(BF16)
    out_gate = zg * _sigmoid(zg)

    for g in range(GM_GROUPS):
        cols = slice(g * GM_GC, (g + 1) * GM_GC)
        pair = jnp.concatenate([gvb[:GM_CHUNK, cols], gvb[GM_CHUNK:, cols]], axis=-1)
        sg = _mm(ws_ref[g], pair)
        for c in range(TILE // GM_CHUNK):
            rows = slice(c * GM_CHUNK, (c + 1) * GM_CHUNK)
            sg_c = sg[:, c * GM_GC:(c + 1) * GM_GC] + bias_ref[:, cols]
            yb_ref[rows, cols] = (u[rows, cols] * sg_c).astype(BF16)

    ga = gb = None
    for h in range(H_A):
        vb = zv[:, h * DV:(h + 1) * DV]
        sc = _mm_nt(qb[h], kb[h]) * dmat_ref[h]
        s_old = s_ref[0, h]
        o = _mm(sc.astype(BF16), vb) + _mm(qd[h], s_old.astype(BF16))
        s_ref[0, h] = s_old * ch_decay[h] + _mm_tn(kd[h], vb)
        on = _group_rms(o, gret_ref[:, h * DV:(h + 1) * DV])
        ya_ref[:, h * DV:(h + 1) * DV] = (out_gate[:, h * DV:(h + 1) * DV] * on).astype(BF16)
        if h == 1:
            ga = _sigmoid(_mm(n, win_ref[:, OFF_GA:OFF_GA + D_MODEL]))
        if h == 2:
            gb = _sigmoid(_mm(n, win_ref[:, OFF_GB:OFF_GB + D_MODEL]))
    merged = ga * _mm(ya_ref[...], wa_ref[...]) + gb * _mm(yb_ref[...], wb_ref[...])
    h_ref[0] = x + _mm(merged.astype(BF16), wo_ref[...])


def _resident(shape):
    zeros = (0,) * len(shape)
    return pl.BlockSpec(shape, lambda *_: zeros, pipeline_mode=pl.Buffered(1))


def _prompt_mix(x, cos, sin, dmat, ind, std, ch_decay, g_mix, g_ret, g_gm, w_in, ws_tril, bias, w_a, w_b, w_o,
                w_up, w_down, w_pg):
    b, L, _ = x.shape
    steps = L // TILE
    total = b * steps

    def slab(w):
        rows = w.shape[0] // total
        assert rows * total == w.shape[0] and rows % 16 == 0
        return pl.BlockSpec((rows, w.shape[1]), lambda i, j: (i * steps + j, 0))

    return pl.pallas_call(
        functools.partial(_prompt_mix_kernel, ch_decay=ch_decay),
        grid=(b, steps),
        in_specs=[
            pl.BlockSpec((1, TILE, D_MODEL), lambda i, j: (i, j, 0)),
            pl.BlockSpec((TILE, DK // 2), lambda i, j: (j, 0)),
            pl.BlockSpec((TILE, DK // 2), lambda i, j: (j, 0)),
            _resident(dmat.shape),
            _resident(ind.shape),
            _resident(std.shape),
            _resident(g_mix.shape),
            _resident(g_ret.shape),
            _resident(g_gm.shape),
            _resident(w_in.shape),
            _resident(ws_tril.shape),
            _resident(bias.shape),
            _resident(w_a.shape),
            _resident(w_b.shape),
            _resident(w_o.shape),
            slab(w_up),
            slab(w_down),
            slab(w_pg),
        ],
        out_specs=[
            pl.BlockSpec((1, TILE, D_MODEL), lambda i, j: (i, j, 0)),
            pl.BlockSpec((1, H_A, DK, DV), lambda i, j: (i, 0, 0, 0)),
            slab(w_up),
            slab(w_down),
            slab(w_pg),
        ],
        out_shape=[
            jax.ShapeDtypeStruct((b, L, D_MODEL), F32),
            jax.ShapeDtypeStruct((b, H_A, DK, DV), F32),
            jax.ShapeDtypeStruct(w_up.shape, BF16),
            jax.ShapeDtypeStruct(w_down.shape, BF16),
            jax.ShapeDtypeStruct(w_pg.shape, BF16),
        ],
        scratch_shapes=[
            pltpu.VMEM((TILE, V_W), BF16),
            pltpu.VMEM((TILE, GM_WIDTH), BF16),
        ],
        compiler_params=pltpu.CompilerParams(
            dimension_semantics=("arbitrary", "arbitrary"), vmem_limit_bytes=VMEM_LIMIT),
        name="prompt_mix",
    )(x, cos, sin, dmat, ind, std, g_mix, g_ret, g_gm, w_in, ws_tril, bias, w_a, w_b, w_o,
      w_up, w_down, w_pg)


def _tail_body(h, p_ref, gmlp_ref, gple_ref, gfin_ref, wup_ref, wdown_ref, wpg_ref, wpp_ref, y_ref,
               side_work=(None, None)):
    emb = _mm(p_ref[...].astype(BF16), wpp_ref[...])
    n2 =_rms(h, gmlp_ref[...]).astype(BF16)
    a = jnp.maximum(_mm(n2, wup_ref[...]), 0.0)
    h = h + _mm((a * a).astype(BF16), wdown_ref[...])
    if side_work[0] is not None:
        side_work[0]()
    n3 = _rms(h, gple_ref[...]).astype(BF16)
    gate = _sigmoid(_mm(n3, wpg_ref[...]))
    if side_work[1] is not None:
        side_work[1]()
    h = h + gate * emb
    y_ref[...] = _rms(h, gfin_ref[...])


def _tail_param_specs(tile, g_mlp, g_ple, g_fin, w_up, w_down, w_pg, w_pp):
    return [
        pl.BlockSpec((tile, PLE_DIM), lambda i: (i, 0)),
        _resident(g_mlp.shape),
        _resident(g_ple.shape),
        _resident(g_fin.shape),
        _resident(w_up.shape),
        _resident(w_down.shape),
        _resident(w_pg.shape),
        _resident(w_pp.shape),
    ]


def _sample_proj_kernel(x_ref, cos_ref, sin_ref, gmix_ref, ggm_ref, win_f_ref,
                        z_ref, gv_ref, win_b_ref, n_ref):
    s = pl.program_id(0)

    @pl.when(s == 0)
    def _():
        n_ref[...] = _rms(x_ref[...], gmix_ref[...]).astype(BF16)

    scale = jnp.where(s == SEC_K, DK ** -0.5, 1.0).astype(F32)
    wb = (win_f_ref[...] * scale).astype(BF16)
    win_b_ref[...] = wb
    z = _mm(n_ref[...], wb)

    @pl.when(s < SEC_V)
    def _():
        cos = cos_ref[...]
        sin = sin_ref[...]
        for h in range(H_A):
            z_ref[:, h * DK:(h + 1) * DK] = _rotary(z[:, h * DK:(h + 1) * DK], cos, sin)

    @pl.when((s >= SEC_V) & (s < SEC_G))
    def _():
        z_ref[...] = z

    @pl.when((s >= SEC_G) & (s < SEC_U))
    def _():
        z_ref[...] = z * _sigmoid(z)

    @pl.when(s == SEC_U)
    def _():
        z_ref[...] = _gelu(z)

    @pl.when(s == SEC_GV)
    def _():
        gv = _layernorm(_gelu(z), ggm_ref[...])
        z_ref[...] = gv
        gv_ref[...] = gv

    @pl.when(s >= SEC_GA)
    def _():
        z_ref[...] = _sigmoid(z)


def _sample_proj(x, cos, sin, g_mix, g_gm, w_in):
    t = x.shape[0]
    section = pl.BlockSpec((D_MODEL, SECTION), lambda s: (0, s))
    return pl.pallas_call(
        _sample_proj_kernel,
        grid=(IN_W // SECTION,),
        in_specs=[_resident(x.shape), _resident(cos.shape), _resident(sin.shape),
                  _resident(g_mix.shape), _resident(g_gm.shape), section],
        out_specs=[pl.BlockSpec((t, SECTION), lambda s: (0, s)),
                   pl.BlockSpec((t, GM_WIDTH), lambda s: (0, 0)),
                   section],
        out_shape=[jax.ShapeDtypeStruct((t, IN_W), F32),
                   jax.ShapeDtypeStruct((t, GM_WIDTH), F32),
                   jax.ShapeDtypeStruct(w_in.shape, BF16)],
        scratch_shapes=[pltpu.VMEM((t, D_MODEL), BF16)],
        compiler_params=pltpu.CompilerParams(
            dimension_semantics=("arbitrary",), vmem_limit_bytes=VMEM_LIMIT),
        name="sample_proj",
    )(x, cos, sin, g_mix, g_gm, w_in)


def _sample_ret_body(q_ref, k_ref, v_ref, s_ref, dm_ref, ind_ref, std_ref, o_ref, snew_ref, *,
                     ch_decay, dec_seq, heads):
    rows = lax.broadcasted_iota(jnp.int32, (SAMPLE_PAIR * dec_seq, 1), 0)
    for h in heads:
        q = q_ref[:, h * DK:(h + 1) * DK]
        k = k_ref[:, h * DK:(h + 1) * DK]
        vb = v_ref[:, h * DV:(h + 1) * DV].astype(BF16)
        sc = _mm_nt(q.astype(BF16), k.astype(BF16)) * dm_ref[h]
        o = _mm(sc.astype(BF16), vb)
        qd = (q * ind_ref[h]).astype(BF16)
        kd = k * std_ref[h]
        for bb in range(SAMPLE_PAIR):
            mine = (rows >= bb * dec_seq) & (rows < (bb + 1) * dec_seq)
            s_old = s_ref[bb, h]
            o = o + jnp.where(mine, _mm(qd, s_old.astype(BF16)), 0.0)
            kd_b = jnp.where(mine, kd, 0.0).astype(BF16)
            snew_ref[bb, h] = s_old * ch_decay[h] + _mm_tn(kd_b, vb)
        o_ref[:, h * DV:(h + 1) * DV] = o


N_TAIL_IN = 9
N_RET_IN = 7


def _tail_ret_kernel(*refs, ch_decay, dec_seq):
    tail_in = refs[:N_TAIL_IN]
    ret_in = refs[N_TAIL_IN:N_TAIL_IN + N_RET_IN]
    y_ref, o_ref, snew_ref = refs[N_TAIL_IN + N_RET_IN:]
    ret = functools.partial(_sample_ret_body, *ret_in, o_ref, snew_ref,
                            ch_decay=ch_decay, dec_seq=dec_seq)
    half = H_A // 2
    side_work = (functools.partial(ret, heads=range(half)),
                 functools.partial(ret, heads=range(half, H_A)))
    _tail_body(tail_in[0][...], *tail_in[1:], y_ref, side_work=side_work)


def _tail_ret(h, p, g_mlp, g_ple, g_fin, w_up, w_down, w_pg, w_pp,
              z_s, state, dm, ind, std, ch_decay, dec_seq):
    t = h.shape[0]
    nb = state.shape[0]
    steps = nb // SAMPLE_PAIR
    tile = t // steps
    assert tile * steps == t and tile % 8 == 0
    rows = SAMPLE_PAIR * dec_seq
    cols = lambda off, w: pl.BlockSpec((rows, w), lambda i: (i, off // w))
    st = pl.BlockSpec((SAMPLE_PAIR, H_A, DK, DV), lambda i: (i, 0, 0, 0))
    return pl.pallas_call(
        functools.partial(_tail_ret_kernel, ch_decay=ch_decay, dec_seq=dec_seq),
        grid=(steps,),
        in_specs=[pl.BlockSpec((tile, D_MODEL), lambda i: (i, 0))] + _tail_param_specs(
            tile, g_mlp, g_ple, g_fin, w_up, w_down, w_pg, w_pp) + [
            cols(OFF_Q, Q_W), cols(OFF_K, Q_W), cols(OFF_V, V_W), st,
            _resident(dm.shape), _resident(ind.shape), _resident(std.shape)],
        out_specs=[pl.BlockSpec((tile, D_MODEL), lambda i: (i, 0)),
                   pl.BlockSpec((rows, V_W), lambda i: (i, 0)), st],
        out_shape=[jax.ShapeDtypeStruct((t, D_MODEL), F32),
                   jax.ShapeDtypeStruct((nb * dec_seq, V_W), F32),
                   jax.ShapeDtypeStruct(state.shape, F32)],
        compiler_params=pltpu.CompilerParams(
            dimension_semantics=("arbitrary",), vmem_limit_bytes=VMEM_LIMIT),
        name="tail_ret",
    )(h, p, g_mlp, g_ple, g_fin, w_up, w_down, w_pg, w_pp, z_s, z_s, z_s, state, dm, ind, std)


N_MERGE_IN = 13


def _sample_tail_kernel(*refs, dec_seq):
    (x_ref, o_ref, sg_ref, u_ref, gv_ref, ga_ref, gb_ref, gret_ref, coef_ref, bias_ref,
     wa_ref, wb_ref, wo_ref) = refs[:N_MERGE_IN]
    t = x_ref.shape[0]
    o = o_ref[...]
    ya = []
    for h in range(H_A):
        cols = slice(h * DV, (h + 1) * DV)
        ya.append((sg_ref[:, cols] * _group_rms(o[:, cols], gret_ref[:, cols])).astype(BF16))
    ya = jnp.concatenate(ya, axis=-1)

    gv = gv_ref[...]
    sub = coef_ref.shape[1]
    mix = None
    for s in range(dec_seq):
        shifted = gv if s == 0 else pltpu.roll(gv, s, 0)
        term = shifted.reshape(t // sub, sub, GM_WIDTH) * coef_ref[s][None]
        mix = term if mix is None else mix + term
    mix = (mix + bias_ref[...][None]).reshape(t, GM_WIDTH)
    yb = (u_ref[...] * mix).astype(BF16)

    merged = ga_ref[...] * _mm(ya, wa_ref[...]) + gb_ref[...] * _mm(yb, wb_ref[...])
    h_mix = x_ref[...] + _mm(merged.astype(BF16), wo_ref[...])
    _tail_body(h_mix, *refs[N_MERGE_IN:])


def _sample_tail(x, o, z_s, g_ret, coef, bias, w_a, w_b, w_o, p, g_mlp, g_ple, g_fin,
                 w_up, w_down, w_pg, w_pp, dec_seq):
    t = x.shape[0]
    tile = TILE
    row = lambda w: pl.BlockSpec((tile, w), lambda i: (i, 0))
    cols = lambda off, w: pl.BlockSpec((tile, w), lambda i: (i, off // w))
    return pl.pallas_call(
        functools.partial(_sample_tail_kernel, dec_seq=dec_seq),
        grid=(t // tile,),
        in_specs=[row(D_MODEL), row(V_W), cols(OFF_G, V_W), cols(OFF_U, GM_WIDTH),
                  cols(OFF_GV, GM_WIDTH), cols(OFF_GA, D_MODEL), cols(OFF_GB, D_MODEL),
                  _resident(g_ret.shape), _resident(coef.shape), _resident(bias.shape),
                  _resident(w_a.shape), _resident(w_b.shape), _resident(w_o.shape)]
                 + _tail_param_specs(tile, g_mlp, g_ple, g_fin, w_up, w_down, w_pg, w_pp),
        out_specs=row(D_MODEL),
        out_shape=jax.ShapeDtypeStruct((t, D_MODEL), F32),
        compiler_params=pltpu.CompilerParams(
            dimension_semantics=("arbitrary",), vmem_limit_bytes=VMEM_LIMIT),
        name="sample_tail",
    )(x, o, z_s, z_s, z_s, z_s, z_s, g_ret, coef, bias, w_a, w_b, w_o,
      p, g_mlp, g_ple, g_fin, w_up, w_down, w_pg, w_pp)


def _rope_tables(pos):
    half = DK // 2
    inv = ROPE_BASE ** (-jnp.arange(half, dtype=F32) / half)
    ang = pos[:, None] * inv[None, :]
    return jnp.cos(ang), jnp.sin(ang)


def _decay_tables(c, groups=1):
    lg = np.log(1.0 - 2.0 ** (-5.0 - np.arange(H_A, dtype=np.float64)))
    idx = np.arange(c, dtype=np.float64)
    diff = idx[:, None] - idx[None, :]
    dmat = np.where(diff[None] >= 0, np.exp(np.maximum(diff, 0.0)[None] * lg[:, None, None]), 0.0)
    in_decay = np.exp((idx[None, :] + 1.0) * lg[:, None])
    st_decay = np.exp((c - 1.0 - idx[None, :]) * lg[:, None])
    n = groups * c
    same = (np.arange(n)[:, None] // c) == (np.arange(n)[None, :] // c)
    mask = np.where(same[None], np.tile(dmat, (1, groups, groups)), 0.0)
    rows = lambda v: np.broadcast_to(np.tile(v, (1, groups))[:, :, None], (H_A, n, DK))
    as_f32 = lambda a: np.ascontiguousarray(a, dtype=np.float32)
    return as_f32(mask), as_f32(rows(in_decay)), as_f32(rows(st_decay))


def _chunk_decay(c):
    return tuple(float((1.0 - 2.0 ** (-5.0 - h)) ** c) for h in range(H_A))


def kernel(x_prompt, x_sample, p_prompt, p_sample, state_ret, g_mix, w_in, g_ret, w_s, b_s, g_gm,
           w_br_a, w_br_b, w_o, g_mlp, w_up, w_down, g_ple, w_pg, w_pp, g_final):
    depth = w_in.shape[0]
    assert depth == 1
    batch, seq, _ = x_prompt.shape
    dec_batch, dec_seq, _ = x_sample.shape
    i = 0

    w_a_b = w_br_a[i].astype(BF16)
    w_b_b = w_br_b[i].astype(BF16)
    w_o_b = w_o[i].astype(BF16)
    w_pp_b = w_pp[i].astype(BF16)
    gmix = g_mix[i][None]
    gret = g_ret[i][None]
    ggm = g_gm[i][None]
    gmlp = g_mlp[i][None]
    gple = g_ple[i][None]
    gfin = g_final[None]

    ts = dec_batch * dec_seq
    xs = x_sample.reshape(ts, D_MODEL)
    pos_s = PAST_LEN + jnp.arange(dec_seq, dtype=F32)
    cos_s, sin_s = _rope_tables(pos_s)
    cos_s = jnp.tile(cos_s, (dec_batch, 1))
    sin_s = jnp.tile(sin_s, (dec_batch, 1))
    z_s, gv_s, w_in_b = _sample_proj(xs, cos_s, sin_s, gmix, ggm, w_in[i])

    cos_p, sin_p = _rope_tables(jnp.arange(seq, dtype=F32))
    dmat, ind_p, std_p = _decay_tables(TILE)
    tril = np.tril(np.ones((GM_CHUNK, GM_CHUNK), dtype=bool))
    ws_tril = jnp.where(tril[None], w_s[i], 0.0).astype(BF16)
    bias_p = jnp.repeat(jnp.transpose(b_s[i]), GM_GC, axis=1)
    h_p, s_p, w_up_b, w_down_b, w_pg_b = _prompt_mix(
        x_prompt, cos_p, sin_p, dmat, ind_p, std_p, _chunk_decay(TILE),
        gmix, gret, ggm, w_in_b, ws_tril, bias_p, w_a_b, w_b_b, w_o_b, w_up[i], w_down[i], w_pg[i])

    dm_pair, ind_pair, std_pair = _decay_tables(dec_seq, groups=SAMPLE_PAIR)
    y_p, o_s, s_s = _tail_ret(
        h_p.reshape(batch * seq, D_MODEL), p_prompt[i].reshape(batch * seq, PLE_DIM),
        gmlp, gple, gfin, w_up_b, w_down_b, w_pg_b, w_pp_b,
        z_s, state_ret[i], dm_pair, ind_pair, std_pair, _chunk_decay(dec_seq), dec_seq)

    sub = 2 * dec_seq
    w_small = jnp.where(tril[None, :dec_seq, :dec_seq], w_s[i][:, :dec_seq, :dec_seq], 0.0)
    tpos = jnp.arange(sub) % dec_seq
    coef = []
    for s in range(dec_seq):
        src = tpos - s
        c = jnp.where((src >= 0)[None, :], w_small[:, tpos, jnp.maximum(src, 0)], 0.0)
        coef.append(jnp.repeat(jnp.transpose(c), GM_GC, axis=1))
    coef = jnp.stack(coef, axis=0)
    bias_s = jnp.repeat(jnp.transpose(b_s[i][:, tpos]), GM_GC, axis=1)
    y_s = _sample_tail(xs, o_s, z_s, gret, coef, bias_s, w_a_b, w_b_b, w_o_b,
                       p_sample[i].reshape(ts, PLE_DIM), gmlp, gple, gfin,
                       w_up_b, w_down_b, w_pg_b, w_pp_b, dec_seq)

    return (y_p.reshape(batch, seq, D_MODEL),
            y_s.reshape(dec_batch, dec_seq, D_MODEL),
            s_p[None],
            s_s[None],
            gv_s.reshape(dec_batch, dec_seq, GM_WIDTH)[None])
```

```python
import functools

import numpy as np
import jax
import jax.numpy as jnp
from jax import lax
from jax.experimental import pallas as pl
from jax.experimental.pallas import tpu as pltpu

D_MODEL = 1024
H_A = 4
DK = 256
DV = 512
Q_W = H_A * DK
V_W = H_A * DV
GM_WIDTH = 1024
GM_GROUPS = 8
GM_GC = GM_WIDTH // GM_GROUPS
GM_CHUNK = 128
D_FF = 4 * D_MODEL
PLE_DIM = 256
EPS = 1e-6
ROPE_BASE = 10000.0
PAST_LEN = 16384

OFF_Q = 0
OFF_K = OFF_Q + Q_W
OFF_V = OFF_K + Q_W
OFF_G = OFF_V + V_W
OFF_U = OFF_G + V_W
OFF_GV = OFF_U + GM_WIDTH
OFF_GA = OFF_GV + GM_WIDTH
OFF_GB = OFF_GA + D_MODEL
IN_W = OFF_GB + D_MODEL
SECTION = 1024
SEC_K, SEC_V, SEC_G, SEC_U, SEC_GV, SEC_GA = (
    OFF_K // SECTION, OFF_V // SECTION, OFF_G // SECTION, OFF_U // SECTION, OFF_GV // SECTION,
    OFF_GA // SECTION)

TILE = 256
SAMPLE_PAIR = 2
VMEM_LIMIT = 56 * 1024 * 1024

BF16 = jnp.bfloat16
F32 = jnp.float32


def _mm(a, b):
    return jnp.dot(a, b, preferred_element_type=F32)


def _mm_nt(a, b):
    return lax.dot_general(a, b, (((1,), (1,)), ((), ())), preferred_element_type=F32)


def _mm_tn(a, b):
    return lax.dot_general(a, b, (((0,), (0,)), ((), ())), preferred_element_type=F32)


def _rms(x, g):
    return x * lax.rsqrt(jnp.mean(x * x, axis=-1, keepdims=True) + EPS) * g


def _sigmoid(x):
    return 1.0 / (1.0 + jnp.exp(-x))


def _gelu(x):
    return 0.5 * x * (1.0 + lax.erf(x * (2.0 ** -0.5)))


def _layernorm(x, g):
    mu = jnp.mean(x, axis=-1, keepdims=True)
    xc = x - mu
    return xc * lax.rsqrt(jnp.mean(xc * xc, axis=-1, keepdims=True) + EPS) * g


def _rotary(z, cos, sin):
    half = DK // 2
    z1, z2 = z[:, :half], z[:, half:]
    return jnp.concatenate([z1 * cos - z2 * sin, z2 * cos + z1 * sin], axis=-1)


def _group_rms(o, g):
    return o * lax.rsqrt(jnp.mean(o * o, axis=-1, keepdims=True) + EPS) * g


def _prompt_mix_kernel(x_ref, xnext_ref, cos_ref, sin_ref, dmat_ref, ind_ref, std_ref, gmix_ref, gret_ref,
                       ggm_ref, win_ref, ws_ref, bias_ref, wa_ref, wb_ref, wo_ref,
                       wup_f_ref, wdown_f_ref, wpg_f_ref,
                       h_ref, s_ref, wup_b_ref, wdown_b_ref, wpg_b_ref, ya_ref, yb_ref, n_ref, *, ch_decay):
    j = pl.program_id(1)

    wup_b_ref[...] = wup_f_ref[...].astype(BF16)
    wdown_b_ref[...] = wdown_f_ref[...].astype(BF16)
    wpg_b_ref[...] = wpg_f_ref[...].astype(BF16)

    @pl.when(j == 0)
    def _():
        s_ref[...] = jnp.zeros_like(s_ref)

    @pl.when((pl.program_id(0) == 0) & (j == 0))
    def _():
        n_ref[...] = _rms(x_ref[0], gmix_ref[...]).astype(BF16)

    x = x_ref[0]
    n = n_ref[...]
    cos = cos_ref[...]
    sin = sin_ref[...]

    u = _gelu(_mm(n, win_ref[:, OFF_U:OFF_U + GM_WIDTH]))
    gv = _layernorm(_gelu(_mm(n, win_ref[:, OFF_GV:OFF_GV + GM_WIDTH])), ggm_ref[...])
    gvb = gv.astype(BF16)

    zq = _mm(n, win_ref[:, OFF_Q:OFF_Q + Q_W])
    zk = _mm(n, win_ref[:, OFF_K:OFF_K + Q_W])
    zg = _mm(n, win_ref[:, OFF_G:OFF_G + V_W])

    qb, kb, qd, kd = [], [], [], []
    for h in range(H_A):
        qr = _rotary(zq[:, h * DK:(h + 1) * DK], cos, sin)
        kr = _rotary(zk[:, h * DK:(h + 1) * DK], cos, sin)
        qb.append(qr.astype(BF16))
        kb.append(kr.astype(BF16))
        qd.append((qr * ind_ref[h]).astype(BF16))
        kd.append((kr * std_ref[h]).astype(BF16))

    zv = _mm(n, win_ref[:, OFF_V:OFF_V + V_W]).astype(BF16)
    out_gate = zg * _sigmoid(zg)

    for g in range(GM_GROUPS):
        cols = slice(g * GM_GC, (g + 1) * GM_GC)
        pair = jnp.concatenate([gvb[:GM_CHUNK, cols], gvb[GM_CHUNK:, cols]], axis=-1)
        sg = _mm(ws_ref[g], pair)
        for c in range(TILE // GM_CHUNK):
            rows = slice(c * GM_CHUNK, (c + 1) * GM_CHUNK)
            sg_c = sg[:, c * GM_GC:(c + 1) * GM_GC] + bias_ref[:, cols]
            yb_ref[rows, cols] = (u[rows, cols] * sg_c).astype(BF16)

    ga = gb = None
    for h in range(H_A):
        vb = zv[:, h * DV:(h + 1) * DV]
        sc = _mm_nt(qb[h], kb[h]) * dmat_ref[h]
        s_old = s_ref[0, h]
        o = _mm(sc.astype(BF16), vb) + _mm(qd[h], s_old.astype(BF16))
        s_ref[0, h] = s_old * ch_decay[h] + _mm_tn(kd[h], vb)
        on = _group_rms(o, gret_ref[:, h * DV:(h + 1) * DV])
        ya_ref[:, h * DV:(h + 1) * DV] = (out_gate[:, h * DV:(h + 1) * DV] * on).astype(BF16)
        if h == 1:
            ga = _sigmoid(_mm(n, win_ref[:, OFF_GA:OFF_GA + D_MODEL]))
        if h == 2:
            gb = _sigmoid(_mm(n, win_ref[:, OFF_GB:OFF_GB + D_MODEL]))
    n_next = _rms(xnext_ref[0], gmix_ref[...]).astype(BF16)
    merged = ga * _mm(ya_ref[...], wa_ref[...]) + gb * _mm(yb_ref[...], wb_ref[...])
    merged = merged.astype(BF16)
    half = D_MODEL // 2
    for c in range(2):
        cols = slice(c * half, (c + 1) * half)
        h_ref[0, :, cols] = x[:, cols] + _mm(merged, wo_ref[:, cols])
    n_ref[...] = n_next


def _resident(shape):
    zeros = (0,) * len(shape)
    return pl.BlockSpec(shape, lambda *_: zeros, pipeline_mode=pl.Buffered(1))


def _prompt_mix(x, cos, sin, dmat, ind, std, ch_decay, g_mix, g_ret, g_gm, w_in, ws_tril, bias, w_a, w_b, w_o,
                w_up, w_down, w_pg):
    b, L, _ = x.shape
    steps = L // TILE
    total = b * steps

    def next_tile(i, j):
        t = jnp.minimum(i * steps + j + 1, total - 1)
        return (t // steps, t % steps, 0)

    def slab(w):
        rows = w.shape[0] // total
        assert rows * total == w.shape[0] and rows % 16 == 0
        return pl.BlockSpec((rows, w.shape[1]), lambda i, j: (i * steps + j, 0))

    return pl.pallas_call(
        functools.partial(_prompt_mix_kernel, ch_decay=ch_decay),
        grid=(b, steps),
        in_specs=[
            pl.BlockSpec((1, TILE, D_MODEL), lambda i, j: (i, j, 0)),
            pl.BlockSpec((1, TILE, D_MODEL), next_tile),
            pl.BlockSpec((TILE, DK // 2), lambda i, j: (j, 0)),
            pl.BlockSpec((TILE, DK // 2), lambda i, j: (j, 0)),
            _resident(dmat.shape),
            _resident(ind.shape),
            _resident(std.shape),
            _resident(g_mix.shape),
            _resident(g_ret.shape),
            _resident(g_gm.shape),
            _resident(w_in.shape),
            _resident(ws_tril.shape),
            _resident(bias.shape),
            _resident(w_a.shape),
            _resident(w_b.shape),
            _resident(w_o.shape),
            slab(w_up),
            slab(w_down),
            slab(w_pg),
        ],
        out_specs=[
            pl.BlockSpec((1, TILE, D_MODEL), lambda i, j: (i, j, 0)),
            pl.BlockSpec((1, H_A, DK, DV), lambda i, j: (i, 0, 0, 0)),
            slab(w_up),
            slab(w_down),
            slab(w_pg),
        ],
        out_shape=[
            jax.ShapeDtypeStruct((b, L, D_MODEL), F32),
            jax.ShapeDtypeStruct((b, H_A, DK, DV), F32),
            jax.ShapeDtypeStruct(w_up.shape, BF16),
            jax.ShapeDtypeStruct(w_down.shape, BF16),
            jax.ShapeDtypeStruct(w_pg.shape, BF16),
        ],
        scratch_shapes=[
            pltpu.VMEM((TILE, V_W), BF16),
            pltpu.VMEM((TILE, GM_WIDTH), BF16),
            pltpu.VMEM((TILE, D_MODEL), BF16),
        ],
        compiler_params=pltpu.CompilerParams(
            dimension_semantics=("arbitrary", "arbitrary"), vmem_limit_bytes=VMEM_LIMIT),
        name="prompt_mix",
    )(x, x, cos, sin, dmat, ind, std, g_mix, g_ret, g_gm, w_in, ws_tril, bias, w_a, w_b, w_o,
      w_up, w_down, w_pg)


def _tail_body(h, p_ref, gmlp_ref, gple_ref, gfin_ref, wup_ref, wdown_ref, wpg_ref, wpp_ref, y_ref,
               side_work=(None, None)):
    emb = _mm(p_ref[...].astype(BF16), wpp_ref[...])
    n2 =_rms(h, gmlp_ref[...]).astype(BF16)
    a = jnp.maximum(_mm(n2, wup_ref[...]), 0.0)
    h = h + _mm((a * a).astype(BF16), wdown_ref[...])
    if side_work[0] is not None:
        side_work[0]()
    n3 = _rms(h, gple_ref[...]).astype(BF16)
    gate = _sigmoid(_mm(n3, wpg_ref[...]))
    if side_work[1] is not None:
        side_work[1]()
    h = h + gate * emb
    y_ref[...] = _rms(h, gfin_ref[...])


def _tail_param_specs(tile, g_mlp, g_ple, g_fin, w_up, w_down, w_pg, w_pp):
    return [
        pl.BlockSpec((tile, PLE_DIM), lambda i: (i, 0)),
        _resident(g_mlp.shape),
        _resident(g_ple.shape),
        _resident(g_fin.shape),
        _resident(w_up.shape),
        _resident(w_down.shape),
        _resident(w_pg.shape),
        _resident(w_pp.shape),
    ]


def _sample_proj_kernel(x_ref, cos_ref, sin_ref, gmix_ref, ggm_ref, win_f_ref,
                        z_ref, gv_ref, win_b_ref, n_ref):
    s = pl.program_id(0)

    @pl.when(s == 0)
    def _():
        n_ref[...] = _rms(x_ref[...], gmix_ref[...]).astype(BF16)

    scale = jnp.where(s == SEC_K, DK ** -0.5, 1.0).astype(F32)
    wb = (win_f_ref[...] * scale).astype(BF16)
    win_b_ref[...] = wb
    z = _mm(n_ref[...], wb)

    @pl.when(s < SEC_V)
    def _():
        cos = cos_ref[...]
        sin = sin_ref[...]
        for h in range(H_A):
            z_ref[:, h * DK:(h + 1) * DK] = _rotary(z[:, h * DK:(h + 1) * DK], cos, sin)

    @pl.when((s >= SEC_V) & (s < SEC_G))
    def _():
        z_ref[...] = z

    @pl.when((s >= SEC_G) & (s < SEC_U))
    def _():
        z_ref[...] = z * _sigmoid(z)

    @pl.when(s == SEC_U)
    def _():
        z_ref[...] = _gelu(z)

    @pl.when(s == SEC_GV)
    def _():
        gv = _layernorm(_gelu(z), ggm_ref[...])
        z_ref[...] = gv
        gv_ref[...] = gv

    @pl.when(s >= SEC_GA)
    def _():
        z_ref[...] = _sigmoid(z)


def _sample_proj(x, cos, sin, g_mix, g_gm, w_in):
    t = x.shape[0]
    section = pl.BlockSpec((D_MODEL, SECTION), lambda s: (0, s))
    return pl.pallas_call(
        _sample_proj_kernel,
        grid=(IN_W // SECTION,),
        in_specs=[_resident(x.shape), _resident(cos.shape), _resident(sin.shape),
                  _resident(g_mix.shape), _resident(g_gm.shape), section],
        out_specs=[pl.BlockSpec((t, SECTION), lambda s: (0, s)),
                   pl.BlockSpec((t, GM_WIDTH), lambda s: (0, 0)),
                   section],
        out_shape=[jax.ShapeDtypeStruct((t, IN_W), F32),
                   jax.ShapeDtypeStruct((t, GM_WIDTH), F32),
                   jax.ShapeDtypeStruct(w_in.shape, BF16)],
        scratch_shapes=[pltpu.VMEM((t, D_MODEL), BF16)],
        compiler_params=pltpu.CompilerParams(
            dimension_semantics=("arbitrary",), vmem_limit_bytes=VMEM_LIMIT),
        name="sample_proj",
    )(x, cos, sin, g_mix, g_gm, w_in)


def _sample_ret_body(q_ref, k_ref, v_ref, s_ref, dm_ref, ind_ref, std_ref, o_ref, snew_ref, *,
                     ch_decay, dec_seq, heads):
    rows = lax.broadcasted_iota(jnp.int32, (SAMPLE_PAIR * dec_seq, 1), 0)
    for h in heads:
        q = q_ref[:, h * DK:(h + 1) * DK]
        k = k_ref[:, h * DK:(h + 1) * DK]
        vb = v_ref[:, h * DV:(h + 1) * DV].astype(BF16)
        sc = _mm_nt(q.astype(BF16), k.astype(BF16)) * dm_ref[h]
        o = _mm(sc.astype(BF16), vb)
        qd = (q * ind_ref[h]).astype(BF16)
        kd = k * std_ref[h]
        for bb in range(SAMPLE_PAIR):
            mine = (rows >= bb * dec_seq) & (rows < (bb + 1) * dec_seq)
            s_old = s_ref[bb, h]
            o = o + jnp.where(mine, _mm(qd, s_old.astype(BF16)), 0.0)
            kd_b = jnp.where(mine, kd, 0.0).astype(BF16)
            snew_ref[bb, h] = s_old * ch_decay[h] + _mm_tn(kd_b, vb)
        o_ref[:, h * DV:(h + 1) * DV] = o


N_TAIL_IN = 9
N_RET_IN = 7


def _tail_ret_kernel(*refs, ch_decay, dec_seq):
    tail_in = refs[:N_TAIL_IN]
    ret_in = refs[N_TAIL_IN:N_TAIL_IN + N_RET_IN]
    y_ref, o_ref, snew_ref = refs[N_TAIL_IN + N_RET_IN:]
    ret = functools.partial(_sample_ret_body, *ret_in, o_ref, snew_ref,
                            ch_decay=ch_decay, dec_seq=dec_seq)
    half = H_A // 2
    side_work = (functools.partial(ret, heads=range(half)),
                 functools.partial(ret, heads=range(half, H_A)))
    _tail_body(tail_in[0][...], *tail_in[1:], y_ref, side_work=side_work)


def _tail_ret(h, p, g_mlp, g_ple, g_fin, w_up, w_down, w_pg, w_pp,
              z_s, state, dm, ind, std, ch_decay, dec_seq):
    t = h.shape[0]
    nb = state.shape[0]
    steps = nb // SAMPLE_PAIR
    tile = t // steps
    assert tile * steps == t and tile % 8 == 0
    rows = SAMPLE_PAIR * dec_seq
    cols = lambda off, w: pl.BlockSpec((rows, w), lambda i: (i, off // w))
    st = pl.BlockSpec((SAMPLE_PAIR, H_A, DK, DV), lambda i: (i, 0, 0, 0))
    return pl.pallas_call(
        functools.partial(_tail_ret_kernel, ch_decay=ch_decay, dec_seq=dec_seq),
        grid=(steps,),
        in_specs=[pl.BlockSpec((tile, D_MODEL), lambda i: (i, 0))] + _tail_param_specs(
            tile, g_mlp, g_ple, g_fin, w_up, w_down, w_pg, w_pp) + [
            cols(OFF_Q, Q_W), cols(OFF_K, Q_W), cols(OFF_V, V_W), st,
            _resident(dm.shape), _resident(ind.shape), _resident(std.shape)],
        out_specs=[pl.BlockSpec((tile, D_MODEL), lambda i: (i, 0)),
                   pl.BlockSpec((rows, V_W), lambda i: (i, 0)), st],
        out_shape=[jax.ShapeDtypeStruct((t, D_MODEL), F32),
                   jax.ShapeDtypeStruct((nb * dec_seq, V_W), F32),
                   jax.ShapeDtypeStruct(state.shape, F32)],
        compiler_params=pltpu.CompilerParams(
            dimension_semantics=("arbitrary",), vmem_limit_bytes=VMEM_LIMIT),
        name="tail_ret",
    )(h, p, g_mlp, g_ple, g_fin, w_up, w_down, w_pg, w_pp, z_s, z_s, z_s, state, dm, ind, std)


N_MERGE_IN = 13


def _sample_tail_kernel(*refs, dec_seq):
    (x_ref, o_ref, sg_ref, u_ref, gv_ref, ga_ref, gb_ref, gret_ref, coef_ref, bias_ref,
     wa_ref, wb_ref, wo_ref) = refs[:N_MERGE_IN]
    t = x_ref.shape[0]
    o = o_ref[...]
    ya = []
    for h in range(H_A):
        cols = slice(h * DV, (h + 1) * DV)
        ya.append((sg_ref[:, cols] * _group_rms(o[:, cols], gret_ref[:, cols])).astype(BF16))
    ya = jnp.concatenate(ya, axis=-1)

    gv = gv_ref[...]
    sub = coef_ref.shape[1]
    mix = None
    for s in range(dec_seq):
        shifted = gv if s == 0 else pltpu.roll(gv, s, 0)
        term = shifted.reshape(t // sub, sub, GM_WIDTH) * coef_ref[s][None]
        mix = term if mix is None else mix + term
    mix = (mix + bias_ref[...][None]).reshape(t, GM_WIDTH)
    yb = (u_ref[...] * mix).astype(BF16)

    merged = ga_ref[...] * _mm(ya, wa_ref[...]) + gb_ref[...] * _mm(yb, wb_ref[...])
    h_mix = x_ref[...] + _mm(merged.astype(BF16), wo_ref[...])
    _tail_body(h_mix, *refs[N_MERGE_IN:])


def _sample_tail(x, o, z_s, g_ret, coef, bias, w_a, w_b, w_o, p, g_mlp, g_ple, g_fin,
                 w_up, w_down, w_pg, w_pp, dec_seq):
    t = x.shape[0]
    tile = TILE
    row = lambda w: pl.BlockSpec((tile, w), lambda i: (i, 0))
    cols = lambda off, w: pl.BlockSpec((tile, w), lambda i: (i, off // w))
    return pl.pallas_call(
        functools.partial(_sample_tail_kernel, dec_seq=dec_seq),
        grid=(t // tile,),
        in_specs=[row(D_MODEL), row(V_W), cols(OFF_G, V_W), cols(OFF_U, GM_WIDTH),
                  cols(OFF_GV, GM_WIDTH), cols(OFF_GA, D_MODEL), cols(OFF_GB, D_MODEL),
                  _resident(g_ret.shape), _resident(coef.shape), _resident(bias.shape),
                  _resident(w_a.shape), _resident(w_b.shape), _resident(w_o.shape)]
                 + _tail_param_specs(tile, g_mlp, g_ple, g_fin, w_up, w_down, w_pg, w_pp),
        out_specs=row(D_MODEL),
        out_shape=jax.ShapeDtypeStruct((t, D_MODEL), F32),
        compiler_params=pltpu.CompilerParams(
            dimension_semantics=("arbitrary",), vmem_limit_bytes=VMEM_LIMIT),
        name="sample_tail",
    )(x, o, z_s, z_s, z_s, z_s, z_s, g_ret, coef, bias, w_a, w_b, w_o,
      p, g_mlp, g_ple, g_fin, w_up, w_down, w_pg, w_pp)


def _rope_tables(pos):
    half = DK // 2
    inv = ROPE_BASE ** (-jnp.arange(half, dtype=F32) / half)
    ang = pos[:, None] * inv[None, :]
    return jnp.cos(ang), jnp.sin(ang)


def _decay_tables(c, groups=1):
    lg = np.log(1.0 - 2.0 ** (-5.0 - np.arange(H_A, dtype=np.float64)))
    idx = np.arange(c, dtype=np.float64)
    diff = idx[:, None] - idx[None, :]
    dmat = np.where(diff[None] >= 0, np.exp(np.maximum(diff, 0.0)[None] * lg[:, None, None]), 0.0)
    in_decay = np.exp((idx[None, :] + 1.0) * lg[:, None])
    st_decay = np.exp((c - 1.0 - idx[None, :]) * lg[:, None])
    n = groups * c
    same = (np.arange(n)[:, None] // c) == (np.arange(n)[None, :] // c)
    mask = np.where(same[None], np.tile(dmat, (1, groups, groups)), 0.0)
    rows = lambda v: np.broadcast_to(np.tile(v, (1, groups))[:, :, None], (H_A, n, DK))
    as_f32 = lambda a: np.ascontiguousarray(a, dtype=np.float32)
    return as_f32(mask), as_f32(rows(in_decay)), as_f32(rows(st_decay))


def _chunk_decay(c):
    return tuple(float((1.0 - 2.0 ** (-5.0 - h)) ** c) for h in range(H_A))


def kernel(x_prompt, x_sample, p_prompt, p_sample, state_ret, g_mix, w_in, g_ret, w_s, b_s, g_gm,
           w_br_a, w_br_b, w_o, g_mlp, w_up, w_down, g_ple, w_pg, w_pp, g_final):
    depth = w_in.shape[0]
    assert depth == 1
    batch, seq, _ = x_prompt.shape
    dec_batch, dec_seq, _ = x_sample.shape
    i = 0

    w_a_b = w_br_a[i].astype(BF16)
    w_b_b = w_br_b[i].astype(BF16)
    w_o_b = w_o[i].astype(BF16)
    w_pp_b = w_pp[i].astype(BF16)
    gmix = g_mix[i][None]
    gret = g_ret[i][None]
    ggm = g_gm[i][None]
    gmlp = g_mlp[i][None]
    gple = g_ple[i][None]
    gfin = g_final[None]

    ts = dec_batch * dec_seq
    xs = x_sample.reshape(ts, D_MODEL)
    pos_s = PAST_LEN + jnp.arange(dec_seq, dtype=F32)
    cos_s, sin_s = _rope_tables(pos_s)
    cos_s = jnp.tile(cos_s, (dec_batch, 1))
    sin_s = jnp.tile(sin_s, (dec_batch, 1))
    z_s, gv_s, w_in_b = _sample_proj(xs, cos_s, sin_s, gmix, ggm, w_in[i])

    cos_p, sin_p = _rope_tables(jnp.arange(seq, dtype=F32))
    dmat, ind_p, std_p = _decay_tables(TILE)
    tril = np.tril(np.ones((GM_CHUNK, GM_CHUNK), dtype=bool))
    ws_tril = jnp.where(tril[None], w_s[i], 0.0).astype(BF16)
    bias_p = jnp.repeat(jnp.transpose(b_s[i]), GM_GC, axis=1)
    h_p, s_p, w_up_b, w_down_b, w_pg_b = _prompt_mix(
        x_prompt, cos_p, sin_p, dmat, ind_p, std_p, _chunk_decay(TILE),
        gmix, gret, ggm, w_in_b, ws_tril, bias_p, w_a_b, w_b_b, w_o_b, w_up[i], w_down[i], w_pg[i])

    dm_pair, ind_pair, std_pair = _decay_tables(dec_seq, groups=SAMPLE_PAIR)
    y_p, o_s, s_s = _tail_ret(
        h_p.reshape(batch * seq, D_MODEL), p_prompt[i].reshape(batch * seq, PLE_DIM),
        gmlp, gple, gfin, w_up_b, w_down_b, w_pg_b, w_pp_b,
        z_s, state_ret[i], dm_pair, ind_pair, std_pair, _chunk_decay(dec_seq), dec_seq)

    sub = 2 * dec_seq
    w_small = jnp.where(tril[None, :dec_seq, :dec_seq], w_s[i][:, :dec_seq, :dec_seq], 0.0)
    tpos = jnp.arange(sub) % dec_seq
    coef = []
    for s in range(dec_seq):
        src = tpos - s
        c = jnp.where((src >= 0)[None, :], w_small[:, tpos, jnp.maximum(src, 0)], 0.0)
        coef.append(jnp.repeat(jnp.transpose(c), GM_GC, axis=1))
    coef = jnp.stack(coef, axis=0)
    bias_s = jnp.repeat(jnp.transpose(b_s[i][:, tpos]), GM_GC, axis=1)
    y_s = _sample_tail(xs, o_s, z_s, gret, coef, bias_s, w_a_b, w_b_b, w_o_b,
                       p_sample[i].reshape(ts, PLE_DIM), gmlp, gple, gfin,
                       w_up_b, w_down_b, w_pg_b, w_pp_b, dec_seq)

    return (y_p.reshape(batch, seq, D_MODEL),
            y_s.reshape(dec_batch, dec_seq, D_MODEL),
            s_p[None],
            s_s[None],
            gv_s.reshape(dec_batch, dec_seq, GM_WIDTH)[None])
```

```python
import functools

import numpy as np
import jax
import jax.numpy as jnp
from jax import lax
from jax.experimental import pallas as pl
from jax.experimental.pallas import tpu as pltpu
from jax.experimental.pallas import tpu_sc as plsc

D_MODEL = 1024
H_A = 4
DK = 256
DV = 512
Q_W = H_A * DK
V_W = H_A * DV
GM_WIDTH = 1024
GM_GROUPS = 8
GM_GC = GM_WIDTH // GM_GROUPS
GM_CHUNK = 128
D_FF = 4 * D_MODEL
PLE_DIM = 256
EPS = 1e-6
ROPE_BASE = 10000.0
PAST_LEN = 16384

OFF_Q = 0
OFF_K = OFF_Q + Q_W
OFF_V = OFF_K + Q_W
OFF_G = OFF_V + V_W
OFF_U = OFF_G + V_W
OFF_GV = OFF_U + GM_WIDTH
OFF_GA = OFF_GV + GM_WIDTH
OFF_GB = OFF_GA + D_MODEL
IN_W = OFF_GB + D_MODEL
SECTION = 1024
SEC_K, SEC_V, SEC_G, SEC_U, SEC_GV, SEC_GA = (
    OFF_K // SECTION, OFF_V // SECTION, OFF_G // SECTION, OFF_U // SECTION, OFF_GV // SECTION,
    OFF_GA // SECTION)

TILE = 256
TAIL_TILE = 512
VMEM_LIMIT = 56 * 1024 * 1024

SC_CORES = 2
SC_SUBCORES = 16
SC_LANES = 16
SC_ROWS = 16
SC_COLS = 4
SC_TOKENS = 8

BF16 = jnp.bfloat16
F32 = jnp.float32


def _mm(a, b):
    return jnp.dot(a, b, preferred_element_type=F32)


def _mm_nt(a, b):
    return lax.dot_general(a, b, (((1,), (1,)), ((), ())), preferred_element_type=F32)


def _mm_tn(a, b):
    return lax.dot_general(a, b, (((0,), (0,)), ((), ())), preferred_element_type=F32)


def _rms(x, g):
    return x * lax.rsqrt(jnp.mean(x * x, axis=-1, keepdims=True) + EPS) * g


def _sigmoid(x):
    return 1.0 / (1.0 + jnp.exp(-x))


def _gelu(x):
    return 0.5 * x * (1.0 + lax.erf(x * (2.0 ** -0.5)))


def _layernorm(x, g):
    mu = jnp.mean(x, axis=-1, keepdims=True)
    xc = x - mu
    return xc * lax.rsqrt(jnp.mean(xc * xc, axis=-1, keepdims=True) + EPS) * g


def _rotary(z, cos, sin):
    half = DK // 2
    z1, z2 = z[:, :half], z[:, half:]
    return jnp.concatenate([z1 * cos - z2 * sin, z2 * cos + z1 * sin], axis=-1)


def _group_rms(o, g):
    return o * lax.rsqrt(jnp.mean(o * o, axis=-1, keepdims=True) + EPS) * g


def _times_row_pattern(x, pattern):
    t, w = x.shape
    p = pattern.shape[0]
    return (x.reshape(t // p, p, w) * pattern[None]).reshape(t, w)


def _prompt_mix_kernel(x_ref, cos_ref, sin_ref, dmat_ref, ind_ref, std_ref, gmix_ref, gret_ref,
                       ggm_ref, win_ref, ws_ref, bias_ref, wa_ref, wb_ref, wo_ref,
                       wup_f_ref, wdown_f_ref, wpg_f_ref,
                       h_ref, s_ref, wup_b_ref, wdown_b_ref, wpg_b_ref, ya_ref, yb_ref, *, ch_decay):
    j = pl.program_id(1)

    wup_b_ref[...] = wup_f_ref[...].astype(BF16)
    wdown_b_ref[...] = wdown_f_ref[...].astype(BF16)
    wpg_b_ref[...] = wpg_f_ref[...].astype(BF16)

    @pl.when(j == 0)
    def _():
        s_ref[...] = jnp.zeros_like(s_ref)

    x = x_ref[0]
    n = _rms(x, gmix_ref[...]).astype(BF16)
    cos = cos_ref[...]
    sin = sin_ref[...]

    u = _gelu(_mm(n, win_ref[:, OFF_U:OFF_U + GM_WIDTH]))
    gv = _layernorm(_gelu(_mm(n, win_ref[:, OFF_GV:OFF_GV + GM_WIDTH])), ggm_ref[...])
    gvb = gv.astype(BF16)

    zq = _mm(n, win_ref[:, OFF_Q:OFF_Q + Q_W])
    zk = _mm(n, win_ref[:, OFF_K:OFF_K + Q_W])
    zg = _mm(n, win_ref[:, OFF_G:OFF_G + V_W])

    qb, kb, qd, kd = [], [], [], []
    for h in range(H_A):
        qr = _rotary(zq[:, h * DK:(h + 1) * DK], cos, sin)
        kr = _rotary(zk[:, h * DK:(h + 1) * DK], cos, sin)
        qb.append(qr.astype(BF16))
        kb.append(kr.astype(BF16))
        qd.append((qr * ind_ref[h]).astype(BF16))
        kd.append((kr * std_ref[h]).astype(BF16))

    zv = _mm(n, win_ref[:, OFF_V:OFF_V + V_W]).astype(BF16)
    out_gate = zg * _sigmoid(zg)

    for g in range(GM_GROUPS):
        cols = slice(g * GM_GC, (g + 1) * GM_GC)
        pair = jnp.concatenate([gvb[:GM_CHUNK, cols], gvb[GM_CHUNK:, cols]], axis=-1)
        sg = _mm(ws_ref[g], pair)
        for c in range(TILE // GM_CHUNK):
            rows = slice(c * GM_CHUNK, (c + 1) * GM_CHUNK)
            sg_c = sg[:, c * GM_GC:(c + 1) * GM_GC] + bias_ref[:, cols]
            yb_ref[rows, cols] = (u[rows, cols] * sg_c).astype(BF16)

    ga = gb = None
    for h in range(H_A):
        vb = zv[:, h * DV:(h + 1) * DV]
        sc = _mm_nt(qb[h], kb[h]) * dmat_ref[h]
        s_old = s_ref[0, h]
        o = _mm(sc.astype(BF16), vb) + _mm(qd[h], s_old.astype(BF16))
        s_ref[0, h] = s_old * ch_decay[h] + _mm_tn(kd[h], vb)
        on = _group_rms(o, gret_ref[:, h * DV:(h + 1) * DV])
        ya_ref[:, h * DV:(h + 1) * DV] = (out_gate[:, h * DV:(h + 1) * DV] * on).astype(BF16)
        if h == 1:
            ga = _sigmoid(_mm(n, win_ref[:, OFF_GA:OFF_GA + D_MODEL]))
        if h == 2:
            gb = _sigmoid(_mm(n, win_ref[:, OFF_GB:OFF_GB + D_MODEL]))
    merged = ga * _mm(ya_ref[...], wa_ref[...]) + gb * _mm(yb_ref[...], wb_ref[...])
    h_ref[0] = x + _mm(merged.astype(BF16), wo_ref[...])


def _resident(shape):
    zeros = (0,) * len(shape)
    return pl.BlockSpec(shape, lambda *_: zeros, pipeline_mode=pl.Buffered(1))


def _prompt_mix(x, cos, sin, dmat, ind, std, ch_decay, g_mix, g_ret, g_gm, w_in, ws_tril, bias, w_a, w_b, w_o,
                w_up, w_down, w_pg):
    b, L, _ = x.shape
    steps = L // TILE
    total = b * steps

    def slab(w):
        rows = w.shape[0] // total
        assert rows * total == w.shape[0] and rows % 16 == 0
        return pl.BlockSpec((rows, w.shape[1]), lambda i, j: (i * steps + j, 0))

    return pl.pallas_call(
        functools.partial(_prompt_mix_kernel, ch_decay=ch_decay),
        grid=(b, steps),
        in_specs=[
            pl.BlockSpec((1, TILE, D_MODEL), lambda i, j: (i, j, 0)),
            pl.BlockSpec((TILE, DK // 2), lambda i, j: (j, 0)),
            pl.BlockSpec((TILE, DK // 2), lambda i, j: (j, 0)),
            _resident(dmat.shape),
            _resident(ind.shape),
            _resident(std.shape),
            _resident(g_mix.shape),
            _resident(g_ret.shape),
            _resident(g_gm.shape),
            _resident(w_in.shape),
            _resident(ws_tril.shape),
            _resident(bias.shape),
            _resident(w_a.shape),
            _resident(w_b.shape),
            _resident(w_o.shape),
            slab(w_up),
            slab(w_down),
            slab(w_pg),
        ],
        out_specs=[
            pl.BlockSpec((1, TILE, D_MODEL), lambda i, j: (i, j, 0)),
            pl.BlockSpec((1, H_A, DK, DV), lambda i, j: (i, 0, 0, 0)),
            slab(w_up),
            slab(w_down),
            slab(w_pg),
        ],
        out_shape=[
            jax.ShapeDtypeStruct((b, L, D_MODEL), F32),
            jax.ShapeDtypeStruct((b, H_A, DK, DV), F32),
            jax.ShapeDtypeStruct(w_up.shape, BF16),
            jax.ShapeDtypeStruct(w_down.shape, BF16),
            jax.ShapeDtypeStruct(w_pg.shape, BF16),
        ],
        scratch_shapes=[
            pltpu.VMEM((TILE, V_W), BF16),
            pltpu.VMEM((TILE, GM_WIDTH), BF16),
        ],
        compiler_params=pltpu.CompilerParams(
            dimension_semantics=("arbitrary", "arbitrary"), vmem_limit_bytes=VMEM_LIMIT),
        name="prompt_mix",
    )(x, cos, sin, dmat, ind, std, g_mix, g_ret, g_gm, w_in, ws_tril, bias, w_a, w_b, w_o,
      w_up, w_down, w_pg)


def _tail_body(h, p_ref, gmlp_ref, gple_ref, gfin_ref, wup_ref, wdown_ref, wpg_ref, wpp_ref, y_ref):
    emb = _mm(p_ref[...].astype(BF16), wpp_ref[...])
    n2 = _rms(h, gmlp_ref[...]).astype(BF16)
    a = jnp.maximum(_mm(n2, wup_ref[...]), 0.0)
    h = h + _mm((a * a).astype(BF16), wdown_ref[...])
    n3 = _rms(h, gple_ref[...]).astype(BF16)
    gate = _sigmoid(_mm(n3, wpg_ref[...]))
    h = h + gate * emb
    y_ref[...] = _rms(h, gfin_ref[...])


def _tail_kernel(h_ref, *refs):
    _tail_body(h_ref[...], *refs)


def _tail_param_specs(tile, g_mlp, g_ple, g_fin, w_up, w_down, w_pg, w_pp):
    return [
        pl.BlockSpec((tile, PLE_DIM), lambda i: (i, 0)),
        _resident(g_mlp.shape),
        _resident(g_ple.shape),
        _resident(g_fin.shape),
        _resident(w_up.shape),
        _resident(w_down.shape),
        _resident(w_pg.shape),
        _resident(w_pp.shape),
    ]


def _tail(h, p, g_mlp, g_ple, g_fin, w_up, w_down, w_pg, w_pp):
    t = h.shape[0]
    tile = TAIL_TILE
    assert t % tile == 0
    return pl.pallas_call(
        _tail_kernel,
        grid=(t // tile,),
        in_specs=[pl.BlockSpec((tile, D_MODEL), lambda i: (i, 0))] + _tail_param_specs(
            tile, g_mlp, g_ple, g_fin, w_up, w_down, w_pg, w_pp),
        out_specs=pl.BlockSpec((tile, D_MODEL), lambda i: (i, 0)),
        out_shape=jax.ShapeDtypeStruct((t, D_MODEL), F32),
        compiler_params=pltpu.CompilerParams(
            dimension_semantics=("arbitrary",), vmem_limit_bytes=VMEM_LIMIT),
        name="tail",
    )(h, p, g_mlp, g_ple, g_fin, w_up, w_down, w_pg, w_pp)


def _sample_proj_kernel(x_ref, cos_ref, sin_ref, kdec_ref, gmix_ref, ggm_ref, win_f_ref,
                        z_ref, zb_ref, gv_ref, win_b_ref, n_ref):
    s = pl.program_id(0)

    @pl.when(s == 0)
    def _():
        n_ref[...] = _rms(x_ref[...], gmix_ref[...]).astype(BF16)

    scale = jnp.where(s == SEC_K, DK ** -0.5, 1.0).astype(F32)
    wb = (win_f_ref[...] * scale).astype(BF16)
    win_b_ref[...] = wb
    z = _mm(n_ref[...], wb)

    def rotated():
        cos = cos_ref[...]
        sin = sin_ref[...]
        return jnp.concatenate(
            [_rotary(z[:, h * DK:(h + 1) * DK], cos, sin) for h in range(H_A)], axis=-1)

    @pl.when(s < SEC_K)
    def _():
        q = rotated()
        z_ref[...] = q
        zb_ref[...] = q.astype(BF16)

    @pl.when(s == SEC_K)
    def _():
        kd = _times_row_pattern(rotated(), kdec_ref[...])
        z_ref[...] = kd
        zb_ref[...] = kd.astype(BF16)

    @pl.when((s >= SEC_V) & (s < SEC_G))
    def _():
        z_ref[...] = z
        zb_ref[...] = z.astype(BF16)

    @pl.when((s >= SEC_G) & (s < SEC_U))
    def _():
        z_ref[...] = z * _sigmoid(z)

    @pl.when(s == SEC_U)
    def _():
        z_ref[...] = _gelu(z)

    @pl.when(s == SEC_GV)
    def _():
        gv = _layernorm(_gelu(z), ggm_ref[...])
        z_ref[...] = gv
        gv_ref[...] = gv

    @pl.when(s >= SEC_GA)
    def _():
        z_ref[...] = _sigmoid(z)


def _sample_proj(x, cos, sin, kdec, g_mix, g_gm, w_in):
    t = x.shape[0]
    section = pl.BlockSpec((D_MODEL, SECTION), lambda s: (0, s))
    return pl.pallas_call(
        _sample_proj_kernel,
        grid=(IN_W // SECTION,),
        in_specs=[_resident(x.shape), _resident(cos.shape), _resident(sin.shape),
                  _resident(kdec.shape), _resident(g_mix.shape), _resident(g_gm.shape), section],
        out_specs=[pl.BlockSpec((t, SECTION), lambda s: (0, s)),
                   pl.BlockSpec((t, SECTION), lambda s: (0, jnp.minimum(s, SEC_G - 1))),
                   pl.BlockSpec((t, GM_WIDTH), lambda s: (0, 0)),
                   section],
        out_shape=[jax.ShapeDtypeStruct((t, IN_W), F32),
                   jax.ShapeDtypeStruct((t, OFF_G), BF16),
                   jax.ShapeDtypeStruct((t, GM_WIDTH), F32),
                   jax.ShapeDtypeStruct(w_in.shape, BF16)],
        scratch_shapes=[pltpu.VMEM((t, D_MODEL), BF16)],
        compiler_params=pltpu.CompilerParams(
            dimension_semantics=("arbitrary",), vmem_limit_bytes=VMEM_LIMIT),
        name="sample_proj",
    )(x, cos, sin, kdec, g_mix, g_gm, w_in)


def _sample_ret_sc(q_u, kd_u, v_u, state_u, ch_decay):
    n_units = state_u.shape[0]
    n_tok = SC_TOKENS // 2
    mesh = plsc.VectorSubcoreMesh(core_axis_name="core", subcore_axis_name="subcore",
                                  num_cores=SC_CORES, num_subcores=SC_SUBCORES)
    group = SC_LANES * SC_COLS

    def lanes(x, dtype):
        return jnp.full((SC_LANES,), x, dtype)

    def body(q, kd, v, s_in, s_out, cross):
        unit = pl.program_id(0)
        rc = pl.program_id(1)
        head = unit % H_A
        decay = jnp.float32(ch_decay[H_A - 1])
        for h in range(H_A - 2, -1, -1):
            decay = jnp.where(head == h, jnp.float32(ch_decay[h]), decay)
        dvec = lanes(decay, F32)
        tvecs = [lanes(t, jnp.int32) for t in range(n_tok)]

        @pl.when(rc == 0)
        def _():
            @pl.loop(0, DV, step=SC_LANES)
            def _(c):
                for t in range(SC_TOKENS):
                    cross.at[0, t, pl.ds(c, SC_LANES)][...] = jnp.zeros((SC_LANES,), F32)

        @pl.loop(0, DV, step=group)
        def _(c0):
            cols = [pl.ds(c0 + SC_LANES * j, SC_LANES) for j in range(SC_COLS)]
            vrow = [[v.at[0, t, cols[j]][...] for j in range(SC_COLS)] for t in range(n_tok)]
            acc = [cross.at[0, t, cols[j]][...] for t in range(n_tok) for j in range(SC_COLS)]
            d0 = lanes(rc * SC_ROWS, jnp.int32)
            for r in range(SC_ROWS):
                d = d0 + r
                qs = [plsc.load_gather(q.at[0], [tvecs[t], d]) for t in range(n_tok)]
                ks = [plsc.load_gather(kd.at[0], [tvecs[t], d]) for t in range(n_tok)]
                for j in range(SC_COLS):
                    s = s_in.at[0, r, cols[j]][...]
                    new = s * dvec
                    for t in range(n_tok):
                        new = new + ks[t] * vrow[t][j]
                        acc[t * SC_COLS + j] = acc[t * SC_COLS + j] + qs[t] * s
                    s_out.at[0, r, cols[j]][...] = new
            for t in range(n_tok):
                for j in range(SC_COLS):
                    cross.at[0, t, cols[j]][...] = acc[t * SC_COLS + j]

    @functools.partial(
        pl.kernel, mesh=mesh,
        out_type=[jax.ShapeDtypeStruct(state_u.shape, F32),
                  jax.ShapeDtypeStruct((n_units, SC_TOKENS, DV), F32)],
        compiler_params=pltpu.CompilerParams(needs_layout_passes=False),
        name="sample_ret")
    def call(q_hbm, kd_hbm, v_hbm, s_hbm, snew_hbm, cross_hbm):
        tok = lambda w: pl.BlockSpec((1, SC_TOKENS, w), lambda u, r: (u, 0, 0))
        rows = pl.BlockSpec((1, SC_ROWS, DV), lambda u, r: (u, r, 0))
        pltpu.emit_pipeline(
            body,
            grid=(n_units, DK // SC_ROWS),
            in_specs=[tok(DK), tok(DK), tok(DV), rows],
            out_specs=[rows, tok(DV)],
            core_axis_name=("core", "subcore"),
            dimension_semantics=(pltpu.PARALLEL, pltpu.ARBITRARY),
        )(q_hbm, kd_hbm, v_hbm, s_hbm, snew_hbm, cross_hbm)

    return call(q_u, kd_u, v_u, state_u)


N_MERGE_IN = 18


def _sample_tail_kernel(*refs, dec_seq):
    (x_ref, cross_ref, q_ref, kd_ref, v_ref, sg_ref, u_ref, gv_ref, ga_ref, gb_ref,
     dm_ref, ind_ref, gret_ref, coef_ref, bias_ref, wa_ref, wb_ref, wo_ref) = refs[:N_MERGE_IN]
    t = x_ref.shape[0]
    ya = []
    for h in range(H_A):
        cols = slice(h * DV, (h + 1) * DV)
        qh = q_ref[:, h * DK:(h + 1) * DK]
        kh = kd_ref[:, h * DK:(h + 1) * DK]
        sc = _mm_nt(qh, kh) * dm_ref[h]
        o = _mm(sc.astype(BF16), v_ref[:, cols])
        o = o + _times_row_pattern(cross_ref[:, cols], ind_ref[:, cols])
        ya.append((sg_ref[:, cols] * _group_rms(o, gret_ref[:, cols])).astype(BF16))
    ya = jnp.concatenate(ya, axis=-1)

    gv = gv_ref[...]
    sub = coef_ref.shape[1]
    mix = None
    for s in range(dec_seq):
        shifted = gv if s == 0 else pltpu.roll(gv, s, 0)
        term = shifted.reshape(t // sub, sub, GM_WIDTH) * coef_ref[s][None]
        mix = term if mix is None else mix + term
    mix = (mix + bias_ref[...][None]).reshape(t, GM_WIDTH)
    yb = (u_ref[...] * mix).astype(BF16)

    merged = ga_ref[...] * _mm(ya, wa_ref[...]) + gb_ref[...] * _mm(yb, wb_ref[...])
    h_mix = x_ref[...] + _mm(merged.astype(BF16), wo_ref[...])
    _tail_body(h_mix, *refs[N_MERGE_IN:])


def _sample_tail(x, cross, z_s, zb_s, dm, ind, g_ret, coef, bias, w_a, w_b, w_o, p, g_mlp, g_ple, g_fin,
                 w_up, w_down, w_pg, w_pp, dec_seq):
    t = x.shape[0]
    tile = TILE
    row = lambda w: pl.BlockSpec((tile, w), lambda i: (i, 0))
    cols = lambda off, w: pl.BlockSpec((tile, w), lambda i: (i, off // w))
    once = lambda off, w: pl.BlockSpec((tile, w), lambda i: (i, off // w), pipeline_mode=pl.Buffered(1))
    return pl.pallas_call(
        functools.partial(_sample_tail_kernel, dec_seq=dec_seq),
        grid=(t // tile,),
        in_specs=[row(D_MODEL), row(V_W), cols(OFF_Q, Q_W), cols(OFF_K, Q_W), cols(OFF_V, V_W),
                  once(OFF_G, V_W), once(OFF_U, GM_WIDTH),
                  once(OFF_GV, GM_WIDTH), once(OFF_GA, D_MODEL), once(OFF_GB, D_MODEL),
                  _resident(dm.shape), _resident(ind.shape),
                  _resident(g_ret.shape), _resident(coef.shape), _resident(bias.shape),
                  _resident(w_a.shape), _resident(w_b.shape), _resident(w_o.shape)]
                 + _tail_param_specs(tile, g_mlp, g_ple, g_fin, w_up, w_down, w_pg, w_pp),
        out_specs=row(D_MODEL),
        out_shape=jax.ShapeDtypeStruct((t, D_MODEL), F32),
        compiler_params=pltpu.CompilerParams(
            dimension_semantics=("arbitrary",), vmem_limit_bytes=VMEM_LIMIT),
        name="sample_tail",
    )(x, cross, zb_s, zb_s, zb_s, z_s, z_s, z_s, z_s, z_s, dm, ind, g_ret, coef, bias, w_a, w_b, w_o,
      p, g_mlp, g_ple, g_fin, w_up, w_down, w_pg, w_pp)


def _rope_tables(pos):
    half = DK // 2
    inv = ROPE_BASE ** (-jnp.arange(half, dtype=F32) / half)
    ang = pos[:, None] * inv[None, :]
    return jnp.cos(ang), jnp.sin(ang)


def _log_decay():
    return np.log(1.0 - 2.0 ** (-5.0 - np.arange(H_A, dtype=np.float64)))


def _as_f32(a):
    return np.ascontiguousarray(a, dtype=np.float32)


def _decay_tables(c):
    lg = _log_decay()
    idx = np.arange(c, dtype=np.float64)
    diff = idx[:, None] - idx[None, :]
    dmat = np.where(diff[None] >= 0, np.exp(np.maximum(diff, 0.0)[None] * lg[:, None, None]), 0.0)
    in_decay = np.exp((idx[None, :] + 1.0) * lg[:, None])
    st_decay = np.exp((c - 1.0 - idx[None, :]) * lg[:, None])
    rows = lambda v: np.broadcast_to(v[:, :, None], (H_A, c, DK))
    return _as_f32(dmat), _as_f32(rows(in_decay)), _as_f32(rows(st_decay))


def _sample_decay_tables(c, rows, period):
    lg = _log_decay()
    t_of = np.arange(rows) % c
    same = (np.arange(rows)[:, None] // c) == (np.arange(rows)[None, :] // c)
    causal = t_of[:, None] >= t_of[None, :]
    mask = np.where((same & causal)[None],
                    np.exp((t_of[:, None] - (c - 1.0))[None] * lg[:, None, None]), 0.0)
    tp = np.arange(period) % c
    key = np.exp((c - 1.0 - tp)[:, None] * lg[None, :])
    cross = np.exp((tp + 1.0)[:, None] * lg[None, :])
    return (_as_f32(mask), _as_f32(np.repeat(key, DK, axis=1)), _as_f32(np.repeat(cross, DV, axis=1)))


def _chunk_decay(c):
    return tuple(float((1.0 - 2.0 ** (-5.0 - h)) ** c) for h in range(H_A))


def kernel(x_prompt, x_sample, p_prompt, p_sample, state_ret, g_mix, w_in, g_ret, w_s, b_s, g_gm,
           w_br_a, w_br_b, w_o, g_mlp, w_up, w_down, g_ple, w_pg, w_pp, g_final):
    depth = w_in.shape[0]
    assert depth == 1
    batch, seq, _ = x_prompt.shape
    dec_batch, dec_seq, _ = x_sample.shape
    assert 2 * dec_seq == SC_TOKENS
    i = 0

    w_a_b = w_br_a[i].astype(BF16)
    w_b_b = w_br_b[i].astype(BF16)
    w_o_b = w_o[i].astype(BF16)
    w_pp_b = w_pp[i].astype(BF16)
    gmix = g_mix[i][None]
    gret = g_ret[i][None]
    ggm = g_gm[i][None]
    gmlp = g_mlp[i][None]
    gple = g_ple[i][None]
    gfin = g_final[None]

    ts = dec_batch * dec_seq
    sub = 2 * dec_seq
    xs = x_sample.reshape(ts, D_MODEL)
    pos_s = PAST_LEN + jnp.arange(dec_seq, dtype=F32)
    cos_s, sin_s = _rope_tables(pos_s)
    cos_s = jnp.tile(cos_s, (dec_batch, 1))
    sin_s = jnp.tile(sin_s, (dec_batch, 1))
    dm_s, kdec_s, ind_s = _sample_decay_tables(dec_seq, TILE, sub)
    z_s, zb_s, gv_s, w_in_b = _sample_proj(xs, cos_s, sin_s, kdec_s, gmix, ggm, w_in[i])

    def per_unit(off, width):
        a = z_s[:, off:off + H_A * width].reshape(dec_batch, dec_seq, H_A, width)
        a = jnp.transpose(a, (0, 2, 1, 3)).reshape(dec_batch * H_A, dec_seq, width)
        return jnp.pad(a, ((0, 0), (0, SC_TOKENS - dec_seq), (0, 0)))

    s_s, cross_u = _sample_ret_sc(
        per_unit(OFF_Q, DK), per_unit(OFF_K, DK), per_unit(OFF_V, DV),
        state_ret[i].reshape(dec_batch * H_A, DK, DV), _chunk_decay(dec_seq))
    cross_s = jnp.transpose(
        cross_u[:, :dec_seq].reshape(dec_batch, H_A, dec_seq, DV), (0, 2, 1, 3)).reshape(ts, V_W)

    cos_p, sin_p = _rope_tables(jnp.arange(seq, dtype=F32))
    dmat, ind_p, std_p = _decay_tables(TILE)
    tril = np.tril(np.ones((GM_CHUNK, GM_CHUNK), dtype=bool))
    ws_tril = jnp.where(tril[None], w_s[i], 0.0).astype(BF16)
    bias_p = jnp.repeat(jnp.transpose(b_s[i]), GM_GC, axis=1)
    h_p, s_p, w_up_b, w_down_b, w_pg_b = _prompt_mix(
        x_prompt, cos_p, sin_p, dmat, ind_p, std_p, _chunk_decay(TILE),
        gmix, gret, ggm, w_in_b, ws_tril, bias_p, w_a_b, w_b_b, w_o_b, w_up[i], w_down[i], w_pg[i])
    y_p = _tail(h_p.reshape(batch * seq, D_MODEL), p_prompt[i].reshape(batch * seq, PLE_DIM),
                gmlp, gple, gfin, w_up_b, w_down_b, w_pg_b, w_pp_b)

    w_small = jnp.where(tril[None, :dec_seq, :dec_seq], w_s[i][:, :dec_seq, :dec_seq], 0.0)
    tpos = jnp.arange(sub) % dec_seq
    coef = []
    for s in range(dec_seq):
        src = tpos - s
        c = jnp.where((src >= 0)[None, :], w_small[:, tpos, jnp.maximum(src, 0)], 0.0)
        coef.append(jnp.repeat(jnp.transpose(c), GM_GC, axis=1))
    coef = jnp.stack(coef, axis=0)
    bias_s = jnp.repeat(jnp.transpose(b_s[i][:, tpos]), GM_GC, axis=1)
    y_s = _sample_tail(xs, cross_s, z_s, zb_s, dm_s, ind_s, gret, coef, bias_s, w_a_b, w_b_b, w_o_b,
                       p_sample[i].reshape(ts, PLE_DIM), gmlp, gple, gfin,
                       w_up_b, w_down_b, w_pg_b, w_pp_b, dec_seq)

    return (y_p.reshape(batch, seq, D_MODEL),
            y_s.reshape(dec_batch, dec_seq, D_MODEL),
            s_p[None],
            s_s.reshape(dec_batch, H_A, DK, DV)[None],
            gv_s.reshape(dec_batch, dec_seq, GM_WIDTH)[None])
```

```python
import functools

import numpy as np
import jax
import jax.numpy as jnp
from jax import lax
from jax.experimental import pallas as pl
from jax.experimental.pallas import tpu as pltpu
from jax.experimental.pallas import tpu_sc as plsc

D_MODEL = 1024
H_A = 4
DK = 256
DV = 512
Q_W = H_A * DK
V_W = H_A * DV
GM_WIDTH = 1024
GM_GROUPS = 8
GM_GC = GM_WIDTH // GM_GROUPS
GM_CHUNK = 128
D_FF = 4 * D_MODEL
PLE_DIM = 256
EPS = 1e-6
ROPE_BASE = 10000.0
PAST_LEN = 16384

OFF_Q = 0
OFF_K = OFF_Q + Q_W
OFF_V = OFF_K + Q_W
OFF_G = OFF_V + V_W
OFF_U = OFF_G + V_W
OFF_GV = OFF_U + GM_WIDTH
OFF_GA = OFF_GV + GM_WIDTH
OFF_GB = OFF_GA + D_MODEL
IN_W = OFF_GB + D_MODEL
SECTION = 1024
SEC_K, SEC_V, SEC_G, SEC_U, SEC_GV, SEC_GA = (
    OFF_K // SECTION, OFF_V // SECTION, OFF_G // SECTION, OFF_U // SECTION, OFF_GV // SECTION,
    OFF_GA // SECTION)

TILE = 256
TAIL_TILE = 512
VMEM_LIMIT = 56 * 1024 * 1024

SC_CORES = 2
SC_SUBCORES = 16
SC_LANES = 16
SC_ROWS = 16
SC_COLS = 4
SC_UNROLL = 2
SC_ROW_BATCH = 2
SC_TOKENS = 8

BF16 = jnp.bfloat16
F32 = jnp.float32


def _mm(a, b):
    return jnp.dot(a, b, preferred_element_type=F32)


def _mm_nt(a, b):
    return lax.dot_general(a, b, (((1,), (1,)), ((), ())), preferred_element_type=F32)


def _mm_tn(a, b):
    return lax.dot_general(a, b, (((0,), (0,)), ((), ())), preferred_element_type=F32)


def _rms(x, g):
    return x * lax.rsqrt(jnp.mean(x * x, axis=-1, keepdims=True) + EPS) * g


def _sigmoid(x):
    return 1.0 / (1.0 + jnp.exp(-x))


def _gelu(x):
    return 0.5 * x * (1.0 + lax.erf(x * (2.0 ** -0.5)))


def _layernorm(x, g):
    mu = jnp.mean(x, axis=-1, keepdims=True)
    xc = x - mu
    return xc * lax.rsqrt(jnp.mean(xc * xc, axis=-1, keepdims=True) + EPS) * g


def _rotary(z, cos, sin):
    half = DK // 2
    z1, z2 = z[:, :half], z[:, half:]
    return jnp.concatenate([z1 * cos - z2 * sin, z2 * cos + z1 * sin], axis=-1)


def _group_rms(o, g):
    return o * lax.rsqrt(jnp.mean(o * o, axis=-1, keepdims=True) + EPS) * g


def _times_row_pattern(x, pattern):
    t, w = x.shape
    p = pattern.shape[0]
    return (x.reshape(t // p, p, w) * pattern[None]).reshape(t, w)


def _prompt_mix_kernel(x_ref, cos_ref, sin_ref, dmat_ref, ind_ref, std_ref, gmix_ref, gret_ref,
                       ggm_ref, win_ref, ws_ref, bias_ref, wa_ref, wb_ref, wo_ref,
                       wup_f_ref, wdown_f_ref, wpg_f_ref,
                       h_ref, s_ref, wup_b_ref, wdown_b_ref, wpg_b_ref, ya_ref, yb_ref, *, ch_decay):
    j = pl.program_id(1)

    wup_b_ref[...] = wup_f_ref[...].astype(BF16)
    wdown_b_ref[...] = wdown_f_ref[...].astype(BF16)
    wpg_b_ref[...] = wpg_f_ref[...].astype(BF16)

    @pl.when(j == 0)
    def _():
        s_ref[...] = jnp.zeros_like(s_ref)

    x = x_ref[0]
    n = _rms(x, gmix_ref[...]).astype(BF16)
    cos = cos_ref[...]
    sin = sin_ref[...]

    u = _gelu(_mm(n, win_ref[:, OFF_U:OFF_U + GM_WIDTH]))
    gv = _layernorm(_gelu(_mm(n, win_ref[:, OFF_GV:OFF_GV + GM_WIDTH])), ggm_ref[...])
    gvb = gv.astype(BF16)

    zq = _mm(n, win_ref[:, OFF_Q:OFF_Q + Q_W])
    zk = _mm(n, win_ref[:, OFF_K:OFF_K + Q_W])
    zg = _mm(n, win_ref[:, OFF_G:OFF_G + V_W])

    qb, kb, qd, kd = [], [], [], []
    for h in range(H_A):
        qr = _rotary(zq[:, h * DK:(h + 1) * DK], cos, sin)
        kr = _rotary(zk[:, h * DK:(h + 1) * DK], cos, sin)
        qb.append(qr.astype(BF16))
        kb.append(kr.astype(BF16))
        qd.append((qr * ind_ref[h]).astype(BF16))
        kd.append((kr * std_ref[h]).astype(BF16))

    zv = _mm(n, win_ref[:, OFF_V:OFF_V + V_W]).astype(BF16)
    out_gate = zg * _sigmoid(zg)

    for g in range(GM_GROUPS):
        cols = slice(g * GM_GC, (g + 1) * GM_GC)
        pair = jnp.concatenate([gvb[:GM_CHUNK, cols], gvb[GM_CHUNK:, cols]], axis=-1)
        sg = _mm(ws_ref[g], pair)
        for c in range(TILE // GM_CHUNK):
            rows = slice(c * GM_CHUNK, (c + 1) * GM_CHUNK)
            sg_c = sg[:, c * GM_GC:(c + 1) * GM_GC] + bias_ref[:, cols]
            yb_ref[rows, cols] = (u[rows, cols] * sg_c).astype(BF16)

    ga = gb = None
    for h in range(H_A):
        vb = zv[:, h * DV:(h + 1) * DV]
        sc = _mm_nt(qb[h], kb[h]) * dmat_ref[h]
        s_old = s_ref[0, h]
        o = _mm(sc.astype(BF16), vb) + _mm(qd[h], s_old.astype(BF16))
        s_ref[0, h] = s_old * ch_decay[h] + _mm_tn(kd[h], vb)
        on = _group_rms(o, gret_ref[:, h * DV:(h + 1) * DV])
        ya_ref[:, h * DV:(h + 1) * DV] = (out_gate[:, h * DV:(h + 1) * DV] * on).astype(BF16)
        if h == 1:
            ga = _sigmoid(_mm(n, win_ref[:, OFF_GA:OFF_GA + D_MODEL]))
        if h == 2:
            gb = _sigmoid(_mm(n, win_ref[:, OFF_GB:OFF_GB + D_MODEL]))
    merged = ga * _mm(ya_ref[...], wa_ref[...]) + gb * _mm(yb_ref[...], wb_ref[...])
    h_ref[0] = x + _mm(merged.astype(BF16), wo_ref[...])


def _resident(shape):
    zeros = (0,) * len(shape)
    return pl.BlockSpec(shape, lambda *_: zeros, pipeline_mode=pl.Buffered(1))


def _prompt_mix(x, cos, sin, dmat, ind, std, ch_decay, g_mix, g_ret, g_gm, w_in, ws_tril, bias, w_a, w_b, w_o,
                w_up, w_down, w_pg):
    b, L, _ = x.shape
    steps = L // TILE
    total = b * steps

    def slab(w):
        rows = w.shape[0] // total
        assert rows * total == w.shape[0] and rows % 16 == 0
        return pl.BlockSpec((rows, w.shape[1]), lambda i, j: (i * steps + j, 0))

    return pl.pallas_call(
        functools.partial(_prompt_mix_kernel, ch_decay=ch_decay),
        grid=(b, steps),
        in_specs=[
            pl.BlockSpec((1, TILE, D_MODEL), lambda i, j: (i, j, 0)),
            pl.BlockSpec((TILE, DK // 2), lambda i, j: (j, 0)),
            pl.BlockSpec((TILE, DK // 2), lambda i, j: (j, 0)),
            _resident(dmat.shape),
            _resident(ind.shape),
            _resident(std.shape),
            _resident(g_mix.shape),
            _resident(g_ret.shape),
            _resident(g_gm.shape),
            _resident(w_in.shape),
            _resident(ws_tril.shape),
            _resident(bias.shape),
            _resident(w_a.shape),
            _resident(w_b.shape),
            _resident(w_o.shape),
            slab(w_up),
            slab(w_down),
            slab(w_pg),
        ],
        out_specs=[
            pl.BlockSpec((1, TILE, D_MODEL), lambda i, j: (i, j, 0)),
            pl.BlockSpec((1, H_A, DK, DV), lambda i, j: (i, 0, 0, 0)),
            slab(w_up),
            slab(w_down),
            slab(w_pg),
        ],
        out_shape=[
            jax.ShapeDtypeStruct((b, L, D_MODEL), F32),
            jax.ShapeDtypeStruct((b, H_A, DK, DV), F32),
            jax.ShapeDtypeStruct(w_up.shape, BF16),
            jax.ShapeDtypeStruct(w_down.shape, BF16),
            jax.ShapeDtypeStruct(w_pg.shape, BF16),
        ],
        scratch_shapes=[
            pltpu.VMEM((TILE, V_W), BF16),
            pltpu.VMEM((TILE, GM_WIDTH), BF16),
        ],
        compiler_params=pltpu.CompilerParams(
            dimension_semantics=("arbitrary", "arbitrary"), vmem_limit_bytes=VMEM_LIMIT),
        name="prompt_mix",
    )(x, cos, sin, dmat, ind, std, g_mix, g_ret, g_gm, w_in, ws_tril, bias, w_a, w_b, w_o,
      w_up, w_down, w_pg)


def _tail_body(h, p_ref, gmlp_ref, gple_ref, gfin_ref, wup_ref, wdown_ref, wpg_ref, wpp_ref, y_ref):
    emb = _mm(p_ref[...].astype(BF16), wpp_ref[...])
    n2 = _rms(h, gmlp_ref[...]).astype(BF16)
    a = jnp.maximum(_mm(n2, wup_ref[...]), 0.0)
    h = h + _mm((a * a).astype(BF16), wdown_ref[...])
    n3 = _rms(h, gple_ref[...]).astype(BF16)
    gate = _sigmoid(_mm(n3, wpg_ref[...]))
    h = h + gate * emb
    y_ref[...] = _rms(h, gfin_ref[...])


def _tail_kernel(h_ref, *refs):
    _tail_body(h_ref[...], *refs)


def _tail_param_specs(tile, g_mlp, g_ple, g_fin, w_up, w_down, w_pg, w_pp):
    return [
        pl.BlockSpec((tile, PLE_DIM), lambda i: (i, 0)),
        _resident(g_mlp.shape),
        _resident(g_ple.shape),
        _resident(g_fin.shape),
        _resident(w_up.shape),
        _resident(w_down.shape),
        _resident(w_pg.shape),
        _resident(w_pp.shape),
    ]


def _tail(h, p, g_mlp, g_ple, g_fin, w_up, w_down, w_pg, w_pp):
    t = h.shape[0]
    tile = TAIL_TILE
    assert t % tile == 0
    return pl.pallas_call(
        _tail_kernel,
        grid=(t // tile,),
        in_specs=[pl.BlockSpec((tile, D_MODEL), lambda i: (i, 0))] + _tail_param_specs(
            tile, g_mlp, g_ple, g_fin, w_up, w_down, w_pg, w_pp),
        out_specs=pl.BlockSpec((tile, D_MODEL), lambda i: (i, 0)),
        out_shape=jax.ShapeDtypeStruct((t, D_MODEL), F32),
        compiler_params=pltpu.CompilerParams(
            dimension_semantics=("arbitrary",), vmem_limit_bytes=VMEM_LIMIT),
        name="tail",
    )(h, p, g_mlp, g_ple, g_fin, w_up, w_down, w_pg, w_pp)


def _sample_proj_kernel(x_ref, cos_ref, sin_ref, kdec_ref, gmix_ref, ggm_ref, win_f_ref,
                        z_ref, zb_ref, gv_ref, win_b_ref, n_ref):
    s = pl.program_id(0)

    @pl.when(s == 0)
    def _():
        n_ref[...] = _rms(x_ref[...], gmix_ref[...]).astype(BF16)

    scale = jnp.where(s == SEC_K, DK ** -0.5, 1.0).astype(F32)
    wb = (win_f_ref[...] * scale).astype(BF16)
    win_b_ref[...] = wb
    z = _mm(n_ref[...], wb)

    def rotated():
        cos = cos_ref[...]
        sin = sin_ref[...]
        return jnp.concatenate(
            [_rotary(z[:, h * DK:(h + 1) * DK], cos, sin) for h in range(H_A)], axis=-1)

    @pl.when(s < SEC_K)
    def _():
        q = rotated()
        z_ref[...] = q
        zb_ref[...] = q.astype(BF16)

    @pl.when(s == SEC_K)
    def _():
        kd = _times_row_pattern(rotated(), kdec_ref[...])
        z_ref[...] = kd
        zb_ref[...] = kd.astype(BF16)

    @pl.when((s >= SEC_V) & (s < SEC_G))
    def _():
        z_ref[...] = z
        zb_ref[...] = z.astype(BF16)

    @pl.when((s >= SEC_G) & (s < SEC_U))
    def _():
        z_ref[...] = z * _sigmoid(z)

    @pl.when(s == SEC_U)
    def _():
        z_ref[...] = _gelu(z)

    @pl.when(s == SEC_GV)
    def _():
        gv = _layernorm(_gelu(z), ggm_ref[...])
        z_ref[...] = gv
        gv_ref[...] = gv

    @pl.when(s >= SEC_GA)
    def _():
        z_ref[...] = _sigmoid(z)


def _sample_proj(x, cos, sin, kdec, g_mix, g_gm, w_in):
    t = x.shape[0]
    section = pl.BlockSpec((D_MODEL, SECTION), lambda s: (0, s))
    return pl.pallas_call(
        _sample_proj_kernel,
        grid=(IN_W // SECTION,),
        in_specs=[_resident(x.shape), _resident(cos.shape), _resident(sin.shape),
                  _resident(kdec.shape), _resident(g_mix.shape), _resident(g_gm.shape), section],
        out_specs=[pl.BlockSpec((t, SECTION), lambda s: (0, s)),
                   pl.BlockSpec((t, SECTION), lambda s: (0, jnp.minimum(s, SEC_G - 1))),
                   pl.BlockSpec((t, GM_WIDTH), lambda s: (0, 0)),
                   section],
        out_shape=[jax.ShapeDtypeStruct((t, IN_W), F32),
                   jax.ShapeDtypeStruct((t, OFF_G), BF16),
                   jax.ShapeDtypeStruct((t, GM_WIDTH), F32),
                   jax.ShapeDtypeStruct(w_in.shape, BF16)],
        scratch_shapes=[pltpu.VMEM((t, D_MODEL), BF16)],
        compiler_params=pltpu.CompilerParams(
            dimension_semantics=("arbitrary",), vmem_limit_bytes=VMEM_LIMIT),
        name="sample_proj",
    )(x, cos, sin, kdec, g_mix, g_gm, w_in)


def _sample_ret_sc(q_u, kd_u, v_u, state_u, ch_decay):
    n_units = state_u.shape[0]
    n_tok = SC_TOKENS // 2
    mesh = plsc.VectorSubcoreMesh(core_axis_name="core", subcore_axis_name="subcore",
                                  num_cores=SC_CORES, num_subcores=SC_SUBCORES)
    group = SC_LANES * SC_COLS

    def lanes(x, dtype):
        return jnp.full((SC_LANES,), x, dtype)

    def body(q, kd, v, s_in, s_out, cross):
        unit = pl.program_id(0)
        rc = pl.program_id(1)
        head = unit % H_A
        decay = jnp.float32(ch_decay[H_A - 1])
        for h in range(H_A - 2, -1, -1):
            decay = jnp.where(head == h, jnp.float32(ch_decay[h]), decay)
        dvec = lanes(decay, F32)
        tvecs = [lanes(t, jnp.int32) for t in range(n_tok)]

        @pl.when(rc == 0)
        def _():
            @pl.loop(0, DV, step=SC_LANES)
            def _(c):
                for t in range(SC_TOKENS):
                    cross.at[0, t, pl.ds(c, SC_LANES)][...] = jnp.zeros((SC_LANES,), F32)

        d0 = lanes(rc * SC_ROWS, jnp.int32)

        def bcast(ref, r):
            return [plsc.load_gather(ref.at[0], [tvecs[t], d0 + r]) for t in range(n_tok)]

        @plsc.parallel_loop(0, DV, step=group, unroll=SC_UNROLL)
        def _(c0):
            cols = [pl.ds(c0 + SC_LANES * j, SC_LANES) for j in range(SC_COLS)]
            acc = [cross.at[0, t, cols[j]][...] for t in range(n_tok) for j in range(SC_COLS)]
            for r in range(SC_ROWS):
                qs = bcast(q, r)
                for j in range(SC_COLS):
                    s = s_in.at[0, r, cols[j]][...]
                    for t in range(n_tok):
                        acc[t * SC_COLS + j] = acc[t * SC_COLS + j] + qs[t] * s
            for t in range(n_tok):
                for j in range(SC_COLS):
                    cross.at[0, t, cols[j]][...] = acc[t * SC_COLS + j]

        @plsc.parallel_loop(0, DV, step=group, unroll=SC_UNROLL)
        def _(c0):
            cols = [pl.ds(c0 + SC_LANES * j, SC_LANES) for j in range(SC_COLS)]
            vrow = [[v.at[0, t, cols[j]][...] for j in range(SC_COLS)] for t in range(n_tok)]
            for rb in range(0, SC_ROWS, SC_ROW_BATCH):
                batch = range(rb, rb + SC_ROW_BATCH)
                ks = {r: bcast(kd, r) for r in batch}
                s = {(r, j): s_in.at[0, r, cols[j]][...] for r in batch for j in range(SC_COLS)}
                for r in batch:
                    for j in range(SC_COLS):
                        terms = [s[r, j] * dvec] + [ks[r][t] * vrow[t][j] for t in range(n_tok)]
                        while len(terms) > 1:
                            terms = [terms[i] + terms[i + 1] if i + 1 < len(terms) else terms[i]
                                     for i in range(0, len(terms), 2)]
                        s_out.at[0, r, cols[j]][...] = terms[0]

    @functools.partial(
        pl.kernel, mesh=mesh,
        out_type=[jax.ShapeDtypeStruct(state_u.shape, F32),
                  jax.ShapeDtypeStruct((n_units, SC_TOKENS, DV), F32)],
        compiler_params=pltpu.CompilerParams(needs_layout_passes=False),
        name="sample_ret")
    def call(q_hbm, kd_hbm, v_hbm, s_hbm, snew_hbm, cross_hbm):
        tok = lambda w: pl.BlockSpec((1, SC_TOKENS, w), lambda u, r: (u, 0, 0))
        rows = pl.BlockSpec((1, SC_ROWS, DV), lambda u, r: (u, r, 0))
        pltpu.emit_pipeline(
            body,
            grid=(n_units, DK // SC_ROWS),
            in_specs=[tok(DK), tok(DK), tok(DV), rows],
            out_specs=[rows, tok(DV)],
            core_axis_name=("core", "subcore"),
            dimension_semantics=(pltpu.PARALLEL, pltpu.ARBITRARY),
        )(q_hbm, kd_hbm, v_hbm, s_hbm, snew_hbm, cross_hbm)

    return call(q_u, kd_u, v_u, state_u)


N_MERGE_IN = 18


def _sample_tail_kernel(*refs, dec_seq):
    (x_ref, cross_ref, q_ref, kd_ref, v_ref, sg_ref, u_ref, gv_ref, ga_ref, gb_ref,
     dm_ref, ind_ref, gret_ref, coef_ref, bias_ref, wa_ref, wb_ref, wo_ref) = refs[:N_MERGE_IN]
    t = x_ref.shape[0]
    ya = []
    for h in range(H_A):
        cols = slice(h * DV, (h + 1) * DV)
        qh = q_ref[:, h * DK:(h + 1) * DK]
        kh = kd_ref[:, h * DK:(h + 1) * DK]
        sc = _mm_nt(qh, kh) * dm_ref[h]
        o = _mm(sc.astype(BF16), v_ref[:, cols])
        o = o + _times_row_pattern(cross_ref[:, cols], ind_ref[:, cols])
        ya.append((sg_ref[:, cols] * _group_rms(o, gret_ref[:, cols])).astype(BF16))
    ya = jnp.concatenate(ya, axis=-1)

    gv = gv_ref[...]
    sub = coef_ref.shape[1]
    mix = None
    for s in range(dec_seq):
        shifted = gv if s == 0 else pltpu.roll(gv, s, 0)
        term = shifted.reshape(t // sub, sub, GM_WIDTH) * coef_ref[s][None]
        mix = term if mix is None else mix + term
    mix = (mix + bias_ref[...][None]).reshape(t, GM_WIDTH)
    yb = (u_ref[...] * mix).astype(BF16)

    merged = ga_ref[...] * _mm(ya, wa_ref[...]) + gb_ref[...] * _mm(yb, wb_ref[...])
    h_mix = x_ref[...] + _mm(merged.astype(BF16), wo_ref[...])
    _tail_body(h_mix, *refs[N_MERGE_IN:])


def _sample_tail(x, cross, z_s, zb_s, dm, ind, g_ret, coef, bias, w_a, w_b, w_o, p, g_mlp, g_ple, g_fin,
                 w_up, w_down, w_pg, w_pp, dec_seq):
    t = x.shape[0]
    tile = TILE
    row = lambda w: pl.BlockSpec((tile, w), lambda i: (i, 0))
    cols = lambda off, w: pl.BlockSpec((tile, w), lambda i: (i, off // w))
    once = lambda off, w: pl.BlockSpec((tile, w), lambda i: (i, off // w), pipeline_mode=pl.Buffered(1))
    return pl.pallas_call(
        functools.partial(_sample_tail_kernel, dec_seq=dec_seq),
        grid=(t // tile,),
        in_specs=[row(D_MODEL), row(V_W), cols(OFF_Q, Q_W), cols(OFF_K, Q_W), cols(OFF_V, V_W),
                  once(OFF_G, V_W), once(OFF_U, GM_WIDTH),
                  once(OFF_GV, GM_WIDTH), once(OFF_GA, D_MODEL), once(OFF_GB, D_MODEL),
                  _resident(dm.shape), _resident(ind.shape),
                  _resident(g_ret.shape), _resident(coef.shape), _resident(bias.shape),
                  _resident(w_a.shape), _resident(w_b.shape), _resident(w_o.shape)]
                 + _tail_param_specs(tile, g_mlp, g_ple, g_fin, w_up, w_down, w_pg, w_pp),
        out_specs=row(D_MODEL),
        out_shape=jax.ShapeDtypeStruct((t, D_MODEL), F32),
        compiler_params=pltpu.CompilerParams(
            dimension_semantics=("arbitrary",), vmem_limit_bytes=VMEM_LIMIT),
        name="sample_tail",
    )(x, cross, zb_s, zb_s, zb_s, z_s, z_s, z_s, z_s, z_s, dm, ind, g_ret, coef, bias, w_a, w_b, w_o,
      p, g_mlp, g_ple, g_fin, w_up, w_down, w_pg, w_pp)


def _rope_tables(pos):
    half = DK // 2
    inv = ROPE_BASE ** (-jnp.arange(half, dtype=F32) / half)
    ang = pos[:, None] * inv[None, :]
    return jnp.cos(ang), jnp.sin(ang)


def _log_decay():
    return np.log(1.0 - 2.0 ** (-5.0 - np.arange(H_A, dtype=np.float64)))


def _as_f32(a):
    return np.ascontiguousarray(a, dtype=np.float32)


def _decay_tables(c):
    lg = _log_decay()
    idx = np.arange(c, dtype=np.float64)
    diff = idx[:, None] - idx[None, :]
    dmat = np.where(diff[None] >= 0, np.exp(np.maximum(diff, 0.0)[None] * lg[:, None, None]), 0.0)
    in_decay = np.exp((idx[None, :] + 1.0) * lg[:, None])
    st_decay = np.exp((c - 1.0 - idx[None, :]) * lg[:, None])
    rows = lambda v: np.broadcast_to(v[:, :, None], (H_A, c, DK))
    return _as_f32(dmat), _as_f32(rows(in_decay)), _as_f32(rows(st_decay))


def _sample_decay_tables(c, rows, period):
    lg = _log_decay()
    t_of = np.arange(rows) % c
    same = (np.arange(rows)[:, None] // c) == (np.arange(rows)[None, :] // c)
    causal = t_of[:, None] >= t_of[None, :]
    mask = np.where((same & causal)[None],
                    np.exp((t_of[:, None] - (c - 1.0))[None] * lg[:, None, None]), 0.0)
    tp = np.arange(period) % c
    key = np.exp((c - 1.0 - tp)[:, None] * lg[None, :])
    cross = np.exp((tp + 1.0)[:, None] * lg[None, :])
    return (_as_f32(mask), _as_f32(np.repeat(key, DK, axis=1)), _as_f32(np.repeat(cross, DV, axis=1)))


def _chunk_decay(c):
    return tuple(float((1.0 - 2.0 ** (-5.0 - h)) ** c) for h in range(H_A))


def kernel(x_prompt, x_sample, p_prompt, p_sample, state_ret, g_mix, w_in, g_ret, w_s, b_s, g_gm,
           w_br_a, w_br_b, w_o, g_mlp, w_up, w_down, g_ple, w_pg, w_pp, g_final):
    depth = w_in.shape[0]
    assert depth == 1
    batch, seq, _ = x_prompt.shape
    dec_batch, dec_seq, _ = x_sample.shape
    assert 2 * dec_seq == SC_TOKENS
    i = 0

    w_a_b = w_br_a[i].astype(BF16)
    w_b_b = w_br_b[i].astype(BF16)
    w_o_b = w_o[i].astype(BF16)
    w_pp_b = w_pp[i].astype(BF16)
    gmix = g_mix[i][None]
    gret = g_ret[i][None]
    ggm = g_gm[i][None]
    gmlp = g_mlp[i][None]
    gple = g_ple[i][None]
    gfin = g_final[None]

    ts = dec_batch * dec_seq
    sub = 2 * dec_seq
    xs = x_sample.reshape(ts, D_MODEL)
    pos_s = PAST_LEN + jnp.arange(dec_seq, dtype=F32)
    cos_s, sin_s = _rope_tables(pos_s)
    cos_s = jnp.tile(cos_s, (dec_batch, 1))
    sin_s = jnp.tile(sin_s, (dec_batch, 1))
    dm_s, kdec_s, ind_s = _sample_decay_tables(dec_seq, TILE, sub)
    z_s, zb_s, gv_s, w_in_b = _sample_proj(xs, cos_s, sin_s, kdec_s, gmix, ggm, w_in[i])

    def per_unit(off, width):
        a = z_s[:, off:off + H_A * width].reshape(dec_batch, dec_seq, H_A, width)
        a = jnp.transpose(a, (0, 2, 1, 3)).reshape(dec_batch * H_A, dec_seq, width)
        return jnp.pad(a, ((0, 0), (0, SC_TOKENS - dec_seq), (0, 0)))

    s_s, cross_u = _sample_ret_sc(
        per_unit(OFF_Q, DK), per_unit(OFF_K, DK), per_unit(OFF_V, DV),
        state_ret[i].reshape(dec_batch * H_A, DK, DV), _chunk_decay(dec_seq))
    cross_s = jnp.transpose(
        cross_u[:, :dec_seq].reshape(dec_batch, H_A, dec_seq, DV), (0, 2, 1, 3)).reshape(ts, V_W)

    cos_p, sin_p = _rope_tables(jnp.arange(seq, dtype=F32))
    dmat, ind_p, std_p = _decay_tables(TILE)
    tril = np.tril(np.ones((GM_CHUNK, GM_CHUNK), dtype=bool))
    ws_tril = jnp.where(tril[None], w_s[i], 0.0).astype(BF16)
    bias_p = jnp.repeat(jnp.transpose(b_s[i]), GM_GC, axis=1)
    h_p, s_p, w_up_b, w_down_b, w_pg_b = _prompt_mix(
        x_prompt, cos_p, sin_p, dmat, ind_p, std_p, _chunk_decay(TILE),
        gmix, gret, ggm, w_in_b, ws_tril, bias_p, w_a_b, w_b_b, w_o_b, w_up[i], w_down[i], w_pg[i])
    y_p = _tail(h_p.reshape(batch * seq, D_MODEL), p_prompt[i].reshape(batch * seq, PLE_DIM),
                gmlp, gple, gfin, w_up_b, w_down_b, w_pg_b, w_pp_b)

    w_small = jnp.where(tril[None, :dec_seq, :dec_seq], w_s[i][:, :dec_seq, :dec_seq], 0.0)
    tpos = jnp.arange(sub) % dec_seq
    coef = []
    for s in range(dec_seq):
        src = tpos - s
        c = jnp.where((src >= 0)[None, :], w_small[:, tpos, jnp.maximum(src, 0)], 0.0)
        coef.append(jnp.repeat(jnp.transpose(c), GM_GC, axis=1))
    coef = jnp.stack(coef, axis=0)
    bias_s = jnp.repeat(jnp.transpose(b_s[i][:, tpos]), GM_GC, axis=1)
    y_s = _sample_tail(xs, cross_s, z_s, zb_s, dm_s, ind_s, gret, coef, bias_s, w_a_b, w_b_b, w_o_b,
                       p_sample[i].reshape(ts, PLE_DIM), gmlp, gple, gfin,
                       w_up_b, w_down_b, w_pg_b, w_pp_b, dec_seq)

    return (y_p.reshape(batch, seq, D_MODEL),
            y_s.reshape(dec_batch, dec_seq, D_MODEL),
            s_p[None],
            s_s.reshape(dec_batch, H_A, DK, DV)[None],
            gv_s.reshape(dec_batch, dec_seq, GM_WIDTH)[None])
```

```python
import functools

import numpy as np
import jax
import jax.numpy as jnp
from jax import lax
from jax.experimental import pallas as pl
from jax.experimental.pallas import tpu as pltpu
from jax.experimental.pallas import tpu_sc as plsc

D_MODEL = 1024
H_A = 4
DK = 256
DV = 512
Q_W = H_A * DK
V_W = H_A * DV
GM_WIDTH = 1024
GM_GROUPS = 8
GM_GC = GM_WIDTH // GM_GROUPS
GM_CHUNK = 128
D_FF = 4 * D_MODEL
PLE_DIM = 256
EPS = 1e-6
ROPE_BASE = 10000.0
PAST_LEN = 16384

OFF_Q = 0
OFF_K = OFF_Q + Q_W
OFF_V = OFF_K + Q_W
OFF_G = OFF_V + V_W
OFF_U = OFF_G + V_W
OFF_GV = OFF_U + GM_WIDTH
OFF_GA = OFF_GV + GM_WIDTH
OFF_GB = OFF_GA + D_MODEL
IN_W = OFF_GB + D_MODEL
SECTION = 1024
SEC_K, SEC_V, SEC_G, SEC_U, SEC_GV, SEC_GA = (
    OFF_K // SECTION, OFF_V // SECTION, OFF_G // SECTION, OFF_U // SECTION, OFF_GV // SECTION,
    OFF_GA // SECTION)

TILE = 256
TAIL_TILE = 512
VMEM_LIMIT = 56 * 1024 * 1024

SC_CORES = 2
SC_SUBCORES = 16
SC_LANES = 16
SC_ROWS = 16
SC_COLS = 4
SC_UNROLL = 2
SC_ROW_BATCH = 2
SC_TOKENS = 8

BF16 = jnp.bfloat16
F32 = jnp.float32


def _mm(a, b):
    return jnp.dot(a, b, preferred_element_type=F32)


def _mm_nt(a, b):
    return lax.dot_general(a, b, (((1,), (1,)), ((), ())), preferred_element_type=F32)


def _mm_tn(a, b):
    return lax.dot_general(a, b, (((0,), (0,)), ((), ())), preferred_element_type=F32)


def _rms(x, g):
    return x * lax.rsqrt(jnp.mean(x * x, axis=-1, keepdims=True) + EPS) * g


def _sigmoid(x):
    return 1.0 / (1.0 + jnp.exp(-x))


def _gelu(x):
    return 0.5 * x * (1.0 + lax.erf(x * (2.0 ** -0.5)))


def _layernorm(x, g):
    mu = jnp.mean(x, axis=-1, keepdims=True)
    xc = x - mu
    return xc * lax.rsqrt(jnp.mean(xc * xc, axis=-1, keepdims=True) + EPS) * g


def _rotary(z, cos, sin):
    half = DK // 2
    z1, z2 = z[:, :half], z[:, half:]
    return jnp.concatenate([z1 * cos - z2 * sin, z2 * cos + z1 * sin], axis=-1)


def _group_rms(o, g):
    return o * lax.rsqrt(jnp.mean(o * o, axis=-1, keepdims=True) + EPS) * g


def _times_row_pattern(x, pattern):
    t, w = x.shape
    p = pattern.shape[0]
    return (x.reshape(t // p, p, w) * pattern[None]).reshape(t, w)


def _prompt_mix_kernel(x_ref, cos_ref, sin_ref, dmat_ref, ind_ref, std_ref, gmix_ref, gret_ref,
                       ggm_ref, win_ref, ws_ref, bias_ref, wa_ref, wb_ref, wo_ref,
                       wup_f_ref, wdown_f_ref, wpg_f_ref,
                       h_ref, s_ref, wup_b_ref, wdown_b_ref, wpg_b_ref, ya_ref, yb_ref, *, ch_decay):
    j = pl.program_id(1)

    wup_b_ref[...] = wup_f_ref[...].astype(BF16)
    wdown_b_ref[...] = wdown_f_ref[...].astype(BF16)
    wpg_b_ref[...] = wpg_f_ref[...].astype(BF16)

    @pl.when(j == 0)
    def _():
        s_ref[...] = jnp.zeros_like(s_ref)

    x = x_ref[0]
    n = _rms(x, gmix_ref[...]).astype(BF16)
    cos = cos_ref[...]
    sin = sin_ref[...]

    u = _gelu(_mm(n, win_ref[:, OFF_U:OFF_U + GM_WIDTH]))
    gv = _layernorm(_gelu(_mm(n, win_ref[:, OFF_GV:OFF_GV + GM_WIDTH])), ggm_ref[...])
    gvb = gv.astype(BF16)

    zq = _mm(n, win_ref[:, OFF_Q:OFF_Q + Q_W])
    zk = _mm(n, win_ref[:, OFF_K:OFF_K + Q_W])
    zg = _mm(n, win_ref[:, OFF_G:OFF_G + V_W])

    qb, kb, qd, kd = [], [], [], []
    for h in range(H_A):
        qr = _rotary(zq[:, h * DK:(h + 1) * DK], cos, sin)
        kr = _rotary(zk[:, h * DK:(h + 1) * DK], cos, sin)
        qb.append(qr.astype(BF16))
        kb.append(kr.astype(BF16))
        qd.append((qr * ind_ref[h]).astype(BF16))
        kd.append((kr * std_ref[h]).astype(BF16))

    zv = _mm(n, win_ref[:, OFF_V:OFF_V + V_W]).astype(BF16)
    out_gate = zg * _sigmoid(zg)

    for g in range(GM_GROUPS):
        cols = slice(g * GM_GC, (g + 1) * GM_GC)
        pair = jnp.concatenate([gvb[:GM_CHUNK, cols], gvb[GM_CHUNK:, cols]], axis=-1)
        sg = _mm(ws_ref[g], pair)
        for c in range(TILE // GM_CHUNK):
            rows = slice(c * GM_CHUNK, (c + 1) * GM_CHUNK)
            sg_c = sg[:, c * GM_GC:(c + 1) * GM_GC] + bias_ref[:, cols]
            yb_ref[rows, cols] = (u[rows, cols] * sg_c).astype(BF16)

    ga = gb = None
    for h in range(H_A):
        vb = zv[:, h * DV:(h + 1) * DV]
        sc = _mm_nt(qb[h], kb[h]) * dmat_ref[h]
        s_old = s_ref[0, h]
        o = _mm(sc.astype(BF16), vb) + _mm(qd[h], s_old.astype(BF16))
        s_ref[0, h] = s_old * ch_decay[h] + _mm_tn(kd[h], vb)
        on = _group_rms(o, gret_ref[:, h * DV:(h + 1) * DV])
        ya_ref[:, h * DV:(h + 1) * DV] = (out_gate[:, h * DV:(h + 1) * DV] * on).astype(BF16)
        if h == 1:
            ga = _sigmoid(_mm(n, win_ref[:, OFF_GA:OFF_GA + D_MODEL]))
        if h == 2:
            gb = _sigmoid(_mm(n, win_ref[:, OFF_GB:OFF_GB + D_MODEL]))
    merged = ga * _mm(ya_ref[...], wa_ref[...]) + gb * _mm(yb_ref[...], wb_ref[...])
    h_ref[0] = x + _mm(merged.astype(BF16), wo_ref[...])


def _resident(shape):
    zeros = (0,) * len(shape)
    return pl.BlockSpec(shape, lambda *_: zeros, pipeline_mode=pl.Buffered(1))


def _prompt_mix(x, cos, sin, dmat, ind, std, ch_decay, g_mix, g_ret, g_gm, w_in, ws_tril, bias, w_a, w_b, w_o,
                w_up, w_down, w_pg):
    b, L, _ = x.shape
    steps = L // TILE
    total = b * steps

    def slab(w):
        rows = w.shape[0] // total
        assert rows * total == w.shape[0] and rows % 16 == 0
        return pl.BlockSpec((rows, w.shape[1]), lambda i, j: (i * steps + j, 0))

    return pl.pallas_call(
        functools.partial(_prompt_mix_kernel, ch_decay=ch_decay),
        grid=(b, steps),
        in_specs=[
            pl.BlockSpec((1, TILE, D_MODEL), lambda i, j: (i, j, 0)),
            pl.BlockSpec((TILE, DK // 2), lambda i, j: (j, 0)),
            pl.BlockSpec((TILE, DK // 2), lambda i, j: (j, 0)),
            _resident(dmat.shape),
            _resident(ind.shape),
            _resident(std.shape),
            _resident(g_mix.shape),
            _resident(g_ret.shape),
            _resident(g_gm.shape),
            _resident(w_in.shape),
            _resident(ws_tril.shape),
            _resident(bias.shape),
            _resident(w_a.shape),
            _resident(w_b.shape),
            _resident(w_o.shape),
            slab(w_up),
            slab(w_down),
            slab(w_pg),
        ],
        out_specs=[
            pl.BlockSpec((1, TILE, D_MODEL), lambda i, j: (i, j, 0)),
            pl.BlockSpec((1, H_A, DK, DV), lambda i, j: (i, 0, 0, 0)),
            slab(w_up),
            slab(w_down),
            slab(w_pg),
        ],
        out_shape=[
            jax.ShapeDtypeStruct((b, L, D_MODEL), F32),
            jax.ShapeDtypeStruct((b, H_A, DK, DV), F32),
            jax.ShapeDtypeStruct(w_up.shape, BF16),
            jax.ShapeDtypeStruct(w_down.shape, BF16),
            jax.ShapeDtypeStruct(w_pg.shape, BF16),
        ],
        scratch_shapes=[
            pltpu.VMEM((TILE, V_W), BF16),
            pltpu.VMEM((TILE, GM_WIDTH), BF16),
        ],
        compiler_params=pltpu.CompilerParams(
            dimension_semantics=("arbitrary", "arbitrary"), vmem_limit_bytes=VMEM_LIMIT),
        name="prompt_mix",
    )(x, cos, sin, dmat, ind, std, g_mix, g_ret, g_gm, w_in, ws_tril, bias, w_a, w_b, w_o,
      w_up, w_down, w_pg)


def _tail_body(h, p_ref, gmlp_ref, gple_ref, gfin_ref, wup_ref, wdown_ref, wpg_ref, wpp_ref, y_ref):
    emb = _mm(p_ref[...].astype(BF16), wpp_ref[...])
    n2 = _rms(h, gmlp_ref[...]).astype(BF16)
    a = jnp.maximum(_mm(n2, wup_ref[...]), 0.0)
    h = h + _mm((a * a).astype(BF16), wdown_ref[...])
    n3 = _rms(h, gple_ref[...]).astype(BF16)
    gate = _sigmoid(_mm(n3, wpg_ref[...]))
    h = h + gate * emb
    y_ref[...] = _rms(h, gfin_ref[...])


def _tail_kernel(h_ref, *refs):
    _tail_body(h_ref[...], *refs)


def _tail_param_specs(tile, g_mlp, g_ple, g_fin, w_up, w_down, w_pg, w_pp):
    return [
        pl.BlockSpec((tile, PLE_DIM), lambda i: (i, 0)),
        _resident(g_mlp.shape),
        _resident(g_ple.shape),
        _resident(g_fin.shape),
        _resident(w_up.shape),
        _resident(w_down.shape),
        _resident(w_pg.shape),
        _resident(w_pp.shape),
    ]


def _tail(h, p, g_mlp, g_ple, g_fin, w_up, w_down, w_pg, w_pp):
    t = h.shape[0]
    tile = TAIL_TILE
    assert t % tile == 0
    return pl.pallas_call(
        _tail_kernel,
        grid=(t // tile,),
        in_specs=[pl.BlockSpec((tile, D_MODEL), lambda i: (i, 0))] + _tail_param_specs(
            tile, g_mlp, g_ple, g_fin, w_up, w_down, w_pg, w_pp),
        out_specs=pl.BlockSpec((tile, D_MODEL), lambda i: (i, 0)),
        out_shape=jax.ShapeDtypeStruct((t, D_MODEL), F32),
        compiler_params=pltpu.CompilerParams(
            dimension_semantics=("arbitrary",), vmem_limit_bytes=VMEM_LIMIT),
        name="tail",
    )(h, p, g_mlp, g_ple, g_fin, w_up, w_down, w_pg, w_pp)


def _sample_proj_kernel(x_ref, cos_ref, sin_ref, kdec_ref, gmix_ref, ggm_ref, win_f_ref,
                        z_ref, zb_ref, gv_ref, win_b_ref, n_ref):
    s = pl.program_id(0)

    @pl.when(s == 0)
    def _():
        n_ref[...] = _rms(x_ref[...], gmix_ref[...]).astype(BF16)

    scale = jnp.where(s == SEC_K, DK ** -0.5, 1.0).astype(F32)
    wb = (win_f_ref[...] * scale).astype(BF16)
    win_b_ref[...] = wb
    z = _mm(n_ref[...], wb)

    def rotated():
        cos = cos_ref[...]
        sin = sin_ref[...]
        return jnp.concatenate(
            [_rotary(z[:, h * DK:(h + 1) * DK], cos, sin) for h in range(H_A)], axis=-1)

    @pl.when(s < SEC_K)
    def _():
        q = rotated()
        z_ref[...] = q
        zb_ref[...] = q.astype(BF16)

    @pl.when(s == SEC_K)
    def _():
        kd = _times_row_pattern(rotated(), kdec_ref[...])
        z_ref[...] = kd
        zb_ref[...] = kd.astype(BF16)

    @pl.when((s >= SEC_V) & (s < SEC_G))
    def _():
        z_ref[...] = z
        zb_ref[...] = z.astype(BF16)

    @pl.when((s >= SEC_G) & (s < SEC_U))
    def _():
        z_ref[...] = z * _sigmoid(z)

    @pl.when(s == SEC_U)
    def _():
        z_ref[...] = _gelu(z)

    @pl.when(s == SEC_GV)
    def _():
        gv = _layernorm(_gelu(z), ggm_ref[...])
        z_ref[...] = gv
        gv_ref[...] = gv

    @pl.when(s >= SEC_GA)
    def _():
        z_ref[...] = _sigmoid(z)


def _sample_proj(x, cos, sin, kdec, g_mix, g_gm, w_in):
    t = x.shape[0]
    section = pl.BlockSpec((D_MODEL, SECTION), lambda s: (0, s))
    return pl.pallas_call(
        _sample_proj_kernel,
        grid=(IN_W // SECTION,),
        in_specs=[_resident(x.shape), _resident(cos.shape), _resident(sin.shape),
                  _resident(kdec.shape), _resident(g_mix.shape), _resident(g_gm.shape), section],
        out_specs=[pl.BlockSpec((t, SECTION), lambda s: (0, s)),
                   pl.BlockSpec((t, SECTION), lambda s: (0, jnp.minimum(s, SEC_G - 1))),
                   pl.BlockSpec((t, GM_WIDTH), lambda s: (0, 0)),
                   section],
        out_shape=[jax.ShapeDtypeStruct((t, IN_W), F32),
                   jax.ShapeDtypeStruct((t, OFF_G), BF16),
                   jax.ShapeDtypeStruct((t, GM_WIDTH), F32),
                   jax.ShapeDtypeStruct(w_in.shape, BF16)],
        scratch_shapes=[pltpu.VMEM((t, D_MODEL), BF16)],
        compiler_params=pltpu.CompilerParams(
            dimension_semantics=("arbitrary",), vmem_limit_bytes=VMEM_LIMIT),
        name="sample_proj",
    )(x, cos, sin, kdec, g_mix, g_gm, w_in)


def _sample_ret_sc(z_s, state_u, ch_decay, dec_seq):
    ts = z_s.shape[0]
    n_tok = dec_seq
    pair = SC_TOKENS // n_tok
    n_units = (ts // SC_TOKENS) * H_A
    row_blocks = DK // SC_ROWS
    mesh = plsc.VectorSubcoreMesh(core_axis_name="core", subcore_axis_name="subcore",
                                  num_cores=SC_CORES, num_subcores=SC_SUBCORES)
    group = SC_LANES * SC_COLS

    def lanes(x, dtype):
        return jnp.full((SC_LANES,), x, dtype)

    def body(q, kd, v, s_in, s_out, cross):
        unit = pl.program_id(0)
        step = pl.program_id(1)
        head = unit % H_A
        tok0 = (step // row_blocks) * n_tok
        rc = step % row_blocks
        decay = jnp.float32(ch_decay[H_A - 1])
        for h in range(H_A - 2, -1, -1):
            decay = jnp.where(head == h, jnp.float32(ch_decay[h]), decay)
        dvec = lanes(decay, F32)
        tvecs = [lanes(tok0 + t, jnp.int32) for t in range(n_tok)]

        @pl.when(step == 0)
        def _():
            @pl.loop(0, DV, step=SC_LANES)
            def _(c):
                for t in range(SC_TOKENS):
                    cross.at[t, pl.ds(c, SC_LANES)][...] = jnp.zeros((SC_LANES,), F32)

        d0 = lanes(rc * SC_ROWS, jnp.int32)

        def bcast(ref, r):
            return [plsc.load_gather(ref, [tvecs[t], d0 + r]) for t in range(n_tok)]

        @plsc.parallel_loop(0, DV, step=group, unroll=SC_UNROLL)
        def _(c0):
            cols = [pl.ds(c0 + SC_LANES * j, SC_LANES) for j in range(SC_COLS)]
            acc = [cross.at[tok0 + t, cols[j]][...] for t in range(n_tok) for j in range(SC_COLS)]
            for r in range(SC_ROWS):
                qs = bcast(q, r)
                for j in range(SC_COLS):
                    s = s_in.at[0, r, cols[j]][...]
                    for t in range(n_tok):
                        acc[t * SC_COLS + j] = acc[t * SC_COLS + j] + qs[t] * s
            for t in range(n_tok):
                for j in range(SC_COLS):
                    cross.at[tok0 + t, cols[j]][...] = acc[t * SC_COLS + j]

        @plsc.parallel_loop(0, DV, step=group, unroll=SC_UNROLL)
        def _(c0):
            cols = [pl.ds(c0 + SC_LANES * j, SC_LANES) for j in range(SC_COLS)]
            vrow = [[v.at[tok0 + t, cols[j]][...] for j in range(SC_COLS)] for t in range(n_tok)]
            for rb in range(0, SC_ROWS, SC_ROW_BATCH):
                batch = range(rb, rb + SC_ROW_BATCH)
                ks = {r: bcast(kd, r) for r in batch}
                s = {(r, j): s_in.at[0, r, cols[j]][...] for r in batch for j in range(SC_COLS)}
                for r in batch:
                    for j in range(SC_COLS):
                        terms = [s[r, j] * dvec] + [ks[r][t] * vrow[t][j] for t in range(n_tok)]
                        while len(terms) > 1:
                            terms = [terms[i] + terms[i + 1] if i + 1 < len(terms) else terms[i]
                                     for i in range(0, len(terms), 2)]
                        s_out.at[0, r, cols[j]][...] = terms[0]

    @functools.partial(
        pl.kernel, mesh=mesh,
        out_type=[jax.ShapeDtypeStruct(state_u.shape, F32),
                  jax.ShapeDtypeStruct((ts, V_W), F32)],
        compiler_params=pltpu.CompilerParams(needs_layout_passes=False),
        name="sample_ret")
    def call(q_hbm, kd_hbm, v_hbm, s_hbm, snew_hbm, cross_hbm):
        tok = lambda off, w: pl.BlockSpec((SC_TOKENS, w), lambda u, s: (u // H_A, off // w + u % H_A))
        state = pl.BlockSpec(
            (1, SC_ROWS, DV),
            lambda u, s: (((u // H_A) * pair + s // row_blocks) * H_A + u % H_A, s % row_blocks, 0))
        pltpu.emit_pipeline(
            body,
            grid=(n_units, pair * row_blocks),
            in_specs=[tok(OFF_Q, DK), tok(OFF_K, DK), tok(OFF_V, DV), state],
            out_specs=[state, tok(0, DV)],
            core_axis_name=("core", "subcore"),
            dimension_semantics=(pltpu.PARALLEL, pltpu.ARBITRARY),
        )(q_hbm, kd_hbm, v_hbm, s_hbm, snew_hbm, cross_hbm)

    return call(z_s, z_s, z_s, state_u)


N_MERGE_IN = 18


def _sample_tail_kernel(*refs, dec_seq):
    (x_ref, cross_ref, q_ref, kd_ref, v_ref, sg_ref, u_ref, gv_ref, ga_ref, gb_ref,
     dm_ref, ind_ref, gret_ref, coef_ref, bias_ref, wa_ref, wb_ref, wo_ref) = refs[:N_MERGE_IN]
    t = x_ref.shape[0]
    ya = []
    for h in range(H_A):
        cols = slice(h * DV, (h + 1) * DV)
        qh = q_ref[:, h * DK:(h + 1) * DK]
        kh = kd_ref[:, h * DK:(h + 1) * DK]
        sc = _mm_nt(qh, kh) * dm_ref[h]
        o = _mm(sc.astype(BF16), v_ref[:, cols])
        o = o + _times_row_pattern(cross_ref[:, cols], ind_ref[:, cols])
        ya.append((sg_ref[:, cols] * _group_rms(o, gret_ref[:, cols])).astype(BF16))
    ya = jnp.concatenate(ya, axis=-1)

    gv = gv_ref[...]
    sub = coef_ref.shape[1]
    mix = None
    for s in range(dec_seq):
        shifted = gv if s == 0 else pltpu.roll(gv, s, 0)
        term = shifted.reshape(t // sub, sub, GM_WIDTH) * coef_ref[s][None]
        mix = term if mix is None else mix + term
    mix = (mix + bias_ref[...][None]).reshape(t, GM_WIDTH)
    yb = (u_ref[...] * mix).astype(BF16)

    merged = ga_ref[...] * _mm(ya, wa_ref[...]) + gb_ref[...] * _mm(yb, wb_ref[...])
    h_mix = x_ref[...] + _mm(merged.astype(BF16), wo_ref[...])
    _tail_body(h_mix, *refs[N_MERGE_IN:])


def _sample_tail(x, cross, z_s, zb_s, dm, ind, g_ret, coef, bias, w_a, w_b, w_o, p, g_mlp, g_ple, g_fin,
                 w_up, w_down, w_pg, w_pp, dec_seq):
    t = x.shape[0]
    tile = TILE
    row = lambda w: pl.BlockSpec((tile, w), lambda i: (i, 0))
    cols = lambda off, w: pl.BlockSpec((tile, w), lambda i: (i, off // w))
    once = lambda off, w: pl.BlockSpec((tile, w), lambda i: (i, off // w), pipeline_mode=pl.Buffered(1))
    return pl.pallas_call(
        functools.partial(_sample_tail_kernel, dec_seq=dec_seq),
        grid=(t // tile,),
        in_specs=[row(D_MODEL), row(V_W), cols(OFF_Q, Q_W), cols(OFF_K, Q_W), cols(OFF_V, V_W),
                  once(OFF_G, V_W), once(OFF_U, GM_WIDTH),
                  once(OFF_GV, GM_WIDTH), once(OFF_GA, D_MODEL), once(OFF_GB, D_MODEL),
                  _resident(dm.shape), _resident(ind.shape),
                  _resident(g_ret.shape), _resident(coef.shape), _resident(bias.shape),
                  _resident(w_a.shape), _resident(w_b.shape), _resident(w_o.shape)]
                 + _tail_param_specs(tile, g_mlp, g_ple, g_fin, w_up, w_down, w_pg, w_pp),
        out_specs=row(D_MODEL),
        out_shape=jax.ShapeDtypeStruct((t, D_MODEL), F32),
        compiler_params=pltpu.CompilerParams(
            dimension_semantics=("arbitrary",), vmem_limit_bytes=VMEM_LIMIT),
        name="sample_tail",
    )(x, cross, zb_s, zb_s, zb_s, z_s, z_s, z_s, z_s, z_s, dm, ind, g_ret, coef, bias, w_a, w_b, w_o,
      p, g_mlp, g_ple, g_fin, w_up, w_down, w_pg, w_pp)


def _rope_tables(pos):
    half = DK // 2
    inv = ROPE_BASE ** (-jnp.arange(half, dtype=F32) / half)
    ang = pos[:, None] * inv[None, :]
    return jnp.cos(ang), jnp.sin(ang)


def _log_decay():
    return np.log(1.0 - 2.0 ** (-5.0 - np.arange(H_A, dtype=np.float64)))


def _as_f32(a):
    return np.ascontiguousarray(a, dtype=np.float32)


def _decay_tables(c):
    lg = _log_decay()
    idx = np.arange(c, dtype=np.float64)
    diff = idx[:, None] - idx[None, :]
    dmat = np.where(diff[None] >= 0, np.exp(np.maximum(diff, 0.0)[None] * lg[:, None, None]), 0.0)
    in_decay = np.exp((idx[None, :] + 1.0) * lg[:, None])
    st_decay = np.exp((c - 1.0 - idx[None, :]) * lg[:, None])
    rows = lambda v: np.broadcast_to(v[:, :, None], (H_A, c, DK))
    return _as_f32(dmat), _as_f32(rows(in_decay)), _as_f32(rows(st_decay))


def _sample_decay_tables(c, rows, period):
    lg = _log_decay()
    t_of = np.arange(rows) % c
    same = (np.arange(rows)[:, None] // c) == (np.arange(rows)[None, :] // c)
    causal = t_of[:, None] >= t_of[None, :]
    mask = np.where((same & causal)[None],
                    np.exp((t_of[:, None] - (c - 1.0))[None] * lg[:, None, None]), 0.0)
    tp = np.arange(period) % c
    key = np.exp((c - 1.0 - tp)[:, None] * lg[None, :])
    cross = np.exp((tp + 1.0)[:, None] * lg[None, :])
    return (_as_f32(mask), _as_f32(np.repeat(key, DK, axis=1)), _as_f32(np.repeat(cross, DV, axis=1)))


def _chunk_decay(c):
    return tuple(float((1.0 - 2.0 ** (-5.0 - h)) ** c) for h in range(H_A))


def kernel(x_prompt, x_sample, p_prompt, p_sample, state_ret, g_mix, w_in, g_ret, w_s, b_s, g_gm,
           w_br_a, w_br_b, w_o, g_mlp, w_up, w_down, g_ple, w_pg, w_pp, g_final):
    depth = w_in.shape[0]
    assert depth == 1
    batch, seq, _ = x_prompt.shape
    dec_batch, dec_seq, _ = x_sample.shape
    assert 2 * dec_seq == SC_TOKENS
    i = 0

    w_a_b = w_br_a[i].astype(BF16)
    w_b_b = w_br_b[i].astype(BF16)
    w_o_b = w_o[i].astype(BF16)
    w_pp_b = w_pp[i].astype(BF16)
    gmix = g_mix[i][None]
    gret = g_ret[i][None]
    ggm = g_gm[i][None]
    gmlp = g_mlp[i][None]
    gple = g_ple[i][None]
    gfin = g_final[None]

    ts = dec_batch * dec_seq
    sub = 2 * dec_seq
    xs = x_sample.reshape(ts, D_MODEL)
    pos_s = PAST_LEN + jnp.arange(dec_seq, dtype=F32)
    cos_s, sin_s = _rope_tables(pos_s)
    cos_s = jnp.tile(cos_s, (dec_batch, 1))
    sin_s = jnp.tile(sin_s, (dec_batch, 1))
    dm_s, kdec_s, ind_s = _sample_decay_tables(dec_seq, TILE, sub)
    z_s, zb_s, gv_s, w_in_b = _sample_proj(xs, cos_s, sin_s, kdec_s, gmix, ggm, w_in[i])

    s_s, cross_s = _sample_ret_sc(
        z_s, state_ret[i].reshape(dec_batch * H_A, DK, DV), _chunk_decay(dec_seq), dec_seq)

    cos_p, sin_p = _rope_tables(jnp.arange(seq, dtype=F32))
    dmat, ind_p, std_p = _decay_tables(TILE)
    tril = np.tril(np.ones((GM_CHUNK, GM_CHUNK), dtype=bool))
    ws_tril = jnp.where(tril[None], w_s[i], 0.0).astype(BF16)
    bias_p = jnp.repeat(jnp.transpose(b_s[i]), GM_GC, axis=1)
    h_p, s_p, w_up_b, w_down_b, w_pg_b = _prompt_mix(
        x_prompt, cos_p, sin_p, dmat, ind_p, std_p, _chunk_decay(TILE),
        gmix, gret, ggm, w_in_b, ws_tril, bias_p, w_a_b, w_b_b, w_o_b, w_up[i], w_down[i], w_pg[i])
    y_p = _tail(h_p.reshape(batch * seq, D_MODEL), p_prompt[i].reshape(batch * seq, PLE_DIM),
                gmlp, gple, gfin, w_up_b, w_down_b, w_pg_b, w_pp_b)

    w_small = jnp.where(tril[None, :dec_seq, :dec_seq], w_s[i][:, :dec_seq, :dec_seq], 0.0)
    tpos = jnp.arange(sub) % dec_seq
    coef = []
    for s in range(dec_seq):
        src = tpos - s
        c = jnp.where((src >= 0)[None, :], w_small[:, tpos, jnp.maximum(src, 0)], 0.0)
        coef.append(jnp.repeat(jnp.transpose(c), GM_GC, axis=1))
    coef = jnp.stack(coef, axis=0)
    bias_s = jnp.repeat(jnp.transpose(b_s[i][:, tpos]), GM_GC, axis=1)
    y_s = _sample_tail(xs, cross_s, z_s, zb_s, dm_s, ind_s, gret, coef, bias_s, w_a_b, w_b_b, w_o_b,
                       p_sample[i].reshape(ts, PLE_DIM), gmlp, gple, gfin,
                       w_up_b, w_down_b, w_pg_b, w_pp_b, dec_seq)

    return (y_p.reshape(batch, seq, D_MODEL),
            y_s.reshape(dec_batch, dec_seq, D_MODEL),
            s_p[None],
            s_s.reshape(dec_batch, H_A, DK, DV)[None],
            gv_s.reshape(dec_batch, dec_seq, GM_WIDTH)[None])
```

```python
import functools

import numpy as np
import jax
import jax.numpy as jnp
from jax import lax
from jax.experimental import pallas as pl
from jax.experimental.pallas import tpu as pltpu
from jax.experimental.pallas import tpu_sc as plsc

D_MODEL = 1024
H_A = 4
DK = 256
DV = 512
Q_W = H_A * DK
V_W = H_A * DV
GM_WIDTH = 1024
GM_GROUPS = 8
GM_GC = GM_WIDTH // GM_GROUPS
GM_CHUNK = 128
D_FF = 4 * D_MODEL
PLE_DIM = 256
EPS = 1e-6
ROPE_BASE = 10000.0
PAST_LEN = 16384

OFF_Q = 0
OFF_K = OFF_Q + Q_W
OFF_V = OFF_K + Q_W
OFF_G = OFF_V + V_W
OFF_U = OFF_G + V_W
OFF_GV = OFF_U + GM_WIDTH
OFF_GA = OFF_GV + GM_WIDTH
OFF_GB = OFF_GA + D_MODEL
IN_W = OFF_GB + D_MODEL
SECTION = 1024
SEC_K, SEC_V, SEC_G, SEC_U, SEC_GV, SEC_GA = (
    OFF_K // SECTION, OFF_V // SECTION, OFF_G // SECTION, OFF_U // SECTION, OFF_GV // SECTION,
    OFF_GA // SECTION)

TILE = 256
TAIL_TILE = 1024
TAIL_FF_CHUNKS = 4
VMEM_LIMIT = 56 * 1024 * 1024

SC_CORES = 2
SC_SUBCORES = 16
SC_LANES = 16
SC_ROWS = 16
SC_COLS = 4
SC_UNROLL = 2
SC_ROW_BATCH = 2
SC_TOKENS = 8

BF16 = jnp.bfloat16
F32 = jnp.float32


def _mm(a, b):
    return jnp.dot(a, b, preferred_element_type=F32)


def _mm_nt(a, b):
    return lax.dot_general(a, b, (((1,), (1,)), ((), ())), preferred_element_type=F32)


def _mm_tn(a, b):
    return lax.dot_general(a, b, (((0,), (0,)), ((), ())), preferred_element_type=F32)


def _rms(x, g):
    return x * lax.rsqrt(jnp.mean(x * x, axis=-1, keepdims=True) + EPS) * g


def _sigmoid(x):
    return 1.0 / (1.0 + jnp.exp(-x))


def _gelu(x):
    return 0.5 * x * (1.0 + lax.erf(x * (2.0 ** -0.5)))


def _layernorm(x, g):
    mu = jnp.mean(x, axis=-1, keepdims=True)
    xc = x - mu
    return xc * lax.rsqrt(jnp.mean(xc * xc, axis=-1, keepdims=True) + EPS) * g


def _rotary(z, cos, sin):
    half = DK // 2
    z1, z2 = z[:, :half], z[:, half:]
    return jnp.concatenate([z1 * cos - z2 * sin, z2 * cos + z1 * sin], axis=-1)


def _group_rms(o, g):
    return o * lax.rsqrt(jnp.mean(o * o, axis=-1, keepdims=True) + EPS) * g


def _times_row_pattern(x, pattern):
    t, w = x.shape
    p = pattern.shape[0]
    return (x.reshape(t // p, p, w) * pattern[None]).reshape(t, w)


def _prompt_mix_kernel(x_ref, cos_ref, sin_ref, dmat_ref, ind_ref, std_ref, gmix_ref, gret_ref,
                       ggm_ref, win_ref, ws_ref, bias_ref, wa_ref, wb_ref, wo_ref,
                       wup_f_ref, wdown_f_ref, wpg_f_ref,
                       h_ref, s_ref, wup_b_ref, wdown_b_ref, wpg_b_ref, ya_ref, yb_ref, *, ch_decay):
    j = pl.program_id(1)

    wup_b_ref[...] = wup_f_ref[...].astype(BF16)
    wdown_b_ref[...] = wdown_f_ref[...].astype(BF16)
    wpg_b_ref[...] = wpg_f_ref[...].astype(BF16)

    @pl.when(j == 0)
    def _():
        s_ref[...] = jnp.zeros_like(s_ref)

    x = x_ref[0]
    n = _rms(x, gmix_ref[...]).astype(BF16)
    cos = cos_ref[...]
    sin = sin_ref[...]

    u = _gelu(_mm(n, win_ref[:, OFF_U:OFF_U + GM_WIDTH]))
    gv = _layernorm(_gelu(_mm(n, win_ref[:, OFF_GV:OFF_GV + GM_WIDTH])), ggm_ref[...])
    gvb = gv.astype(BF16)

    zq = _mm(n, win_ref[:, OFF_Q:OFF_Q + Q_W])
    zk = _mm(n, win_ref[:, OFF_K:OFF_K + Q_W])
    zg = _mm(n, win_ref[:, OFF_G:OFF_G + V_W])

    qb, kb, qd, kd = [], [], [], []
    for h in range(H_A):
        qr = _rotary(zq[:, h * DK:(h + 1) * DK], cos, sin)
        kr = _rotary(zk[:, h * DK:(h + 1) * DK], cos, sin)
        qb.append(qr.astype(BF16))
        kb.append(kr.astype(BF16))
        qd.append((qr * ind_ref[h]).astype(BF16))
        kd.append((kr * std_ref[h]).astype(BF16))

    zv = _mm(n, win_ref[:, OFF_V:OFF_V + V_W]).astype(BF16)
    out_gate = zg * _sigmoid(zg)

    for g in range(GM_GROUPS):
        cols = slice(g * GM_GC, (g + 1) * GM_GC)
        pair = jnp.concatenate([gvb[:GM_CHUNK, cols], gvb[GM_CHUNK:, cols]], axis=-1)
        sg = _mm(ws_ref[g], pair)
        for c in range(TILE // GM_CHUNK):
            rows = slice(c * GM_CHUNK, (c + 1) * GM_CHUNK)
            sg_c = sg[:, c * GM_GC:(c + 1) * GM_GC] + bias_ref[:, cols]
            yb_ref[rows, cols] = (u[rows, cols] * sg_c).astype(BF16)

    ga = gb = None
    for h in range(H_A):
        vb = zv[:, h * DV:(h + 1) * DV]
        sc = _mm_nt(qb[h], kb[h]) * dmat_ref[h]
        s_old = s_ref[0, h]
        o = _mm(sc.astype(BF16), vb) + _mm(qd[h], s_old.astype(BF16))
        s_ref[0, h] = s_old * ch_decay[h] + _mm_tn(kd[h], vb)
        on = _group_rms(o, gret_ref[:, h * DV:(h + 1) * DV])
        ya_ref[:, h * DV:(h + 1) * DV] = (out_gate[:, h * DV:(h + 1) * DV] * on).astype(BF16)
        if h == 1:
            ga = _sigmoid(_mm(n, win_ref[:, OFF_GA:OFF_GA + D_MODEL]))
        if h == 2:
            gb = _sigmoid(_mm(n, win_ref[:, OFF_GB:OFF_GB + D_MODEL]))
    merged = ga * _mm(ya_ref[...], wa_ref[...]) + gb * _mm(yb_ref[...], wb_ref[...])
    h_ref[0] = x + _mm(merged.astype(BF16), wo_ref[...])


def _resident(shape):
    zeros = (0,) * len(shape)
    return pl.BlockSpec(shape, lambda *_: zeros, pipeline_mode=pl.Buffered(1))


def _prompt_mix(x, cos, sin, dmat, ind, std, ch_decay, g_mix, g_ret, g_gm, w_in, ws_tril, bias, w_a, w_b, w_o,
                w_up, w_down, w_pg):
    b, L, _ = x.shape
    steps = L // TILE
    total = b * steps

    def slab(w):
        rows = w.shape[0] // total
        assert rows * total == w.shape[0] and rows % 16 == 0
        return pl.BlockSpec((rows, w.shape[1]), lambda i, j: (i * steps + j, 0))

    return pl.pallas_call(
        functools.partial(_prompt_mix_kernel, ch_decay=ch_decay),
        grid=(b, steps),
        in_specs=[
            pl.BlockSpec((1, TILE, D_MODEL), lambda i, j: (i, j, 0)),
            pl.BlockSpec((TILE, DK // 2), lambda i, j: (j, 0)),
            pl.BlockSpec((TILE, DK // 2), lambda i, j: (j, 0)),
            _resident(dmat.shape),
            _resident(ind.shape),
            _resident(std.shape),
            _resident(g_mix.shape),
            _resident(g_ret.shape),
            _resident(g_gm.shape),
            _resident(w_in.shape),
            _resident(ws_tril.shape),
            _resident(bias.shape),
            _resident(w_a.shape),
            _resident(w_b.shape),
            _resident(w_o.shape),
            slab(w_up),
            slab(w_down),
            slab(w_pg),
        ],
        out_specs=[
            pl.BlockSpec((1, TILE, D_MODEL), lambda i, j: (i, j, 0)),
            pl.BlockSpec((1, H_A, DK, DV), lambda i, j: (i, 0, 0, 0)),
            slab(w_up),
            slab(w_down),
            slab(w_pg),
        ],
        out_shape=[
            jax.ShapeDtypeStruct((b, L, D_MODEL), F32),
            jax.ShapeDtypeStruct((b, H_A, DK, DV), F32),
            jax.ShapeDtypeStruct(w_up.shape, BF16),
            jax.ShapeDtypeStruct(w_down.shape, BF16),
            jax.ShapeDtypeStruct(w_pg.shape, BF16),
        ],
        scratch_shapes=[
            pltpu.VMEM((TILE, V_W), BF16),
            pltpu.VMEM((TILE, GM_WIDTH), BF16),
        ],
        compiler_params=pltpu.CompilerParams(
            dimension_semantics=("arbitrary", "arbitrary"), vmem_limit_bytes=VMEM_LIMIT),
        name="prompt_mix",
    )(x, cos, sin, dmat, ind, std, g_mix, g_ret, g_gm, w_in, ws_tril, bias, w_a, w_b, w_o,
      w_up, w_down, w_pg)


def _tail_body(h, p_ref, gmlp_ref, gple_ref, gfin_ref, wup_ref, wdown_ref, wpg_ref, wpp_ref, y_ref,
               ff_chunks=1):
    emb = _mm(p_ref[...].astype(BF16), wpp_ref[...])
    n2 = _rms(h, gmlp_ref[...]).astype(BF16)
    width = D_FF // ff_chunks
    for c in range(ff_chunks):
        cols = slice(c * width, (c + 1) * width)
        a = jnp.maximum(_mm(n2, wup_ref[:, cols]), 0.0)
        h = h + _mm((a * a).astype(BF16), wdown_ref[cols, :])
    n3 =_rms(h, gple_ref[...]).astype(BF16)
    gate = _sigmoid(_mm(n3, wpg_ref[...]))
    h = h + gate * emb
    y_ref[...] = _rms(h, gfin_ref[...])


def _tail_kernel(h_ref, *refs):
    _tail_body(h_ref[...], *refs, ff_chunks=TAIL_FF_CHUNKS)


def _tail_param_specs(tile, g_mlp, g_ple, g_fin, w_up, w_down, w_pg, w_pp):
    return [
        pl.BlockSpec((tile, PLE_DIM), lambda i: (i, 0)),
        _resident(g_mlp.shape),
        _resident(g_ple.shape),
        _resident(g_fin.shape),
        _resident(w_up.shape),
        _resident(w_down.shape),
        _resident(w_pg.shape),
        _resident(w_pp.shape),
    ]


def _tail(h, p, g_mlp, g_ple, g_fin, w_up, w_down, w_pg, w_pp):
    t = h.shape[0]
    tile = TAIL_TILE
    assert t % tile == 0
    return pl.pallas_call(
        _tail_kernel,
        grid=(t // tile,),
        in_specs=[pl.BlockSpec((tile, D_MODEL), lambda i: (i, 0))] + _tail_param_specs(
            tile, g_mlp, g_ple, g_fin, w_up, w_down, w_pg, w_pp),
        out_specs=pl.BlockSpec((tile, D_MODEL), lambda i: (i, 0)),
        out_shape=jax.ShapeDtypeStruct((t, D_MODEL), F32),
        compiler_params=pltpu.CompilerParams(
            dimension_semantics=("arbitrary",), vmem_limit_bytes=VMEM_LIMIT),
        name="tail",
    )(h, p, g_mlp, g_ple, g_fin, w_up, w_down, w_pg, w_pp)


def _sample_proj_kernel(x_ref, cos_ref, sin_ref, kdec_ref, gmix_ref, ggm_ref, win_f_ref,
                        z_ref, zb_ref, gv_ref, win_b_ref, n_ref):
    s = pl.program_id(0)

    @pl.when(s == 0)
    def _():
        n_ref[...] = _rms(x_ref[...], gmix_ref[...]).astype(BF16)

    scale = jnp.where(s == SEC_K, DK ** -0.5, 1.0).astype(F32)
    wb = (win_f_ref[...] * scale).astype(BF16)
    win_b_ref[...] = wb
    z = _mm(n_ref[...], wb)

    def rotated():
        cos = cos_ref[...]
        sin = sin_ref[...]
        return jnp.concatenate(
            [_rotary(z[:, h * DK:(h + 1) * DK], cos, sin) for h in range(H_A)], axis=-1)

    @pl.when(s < SEC_K)
    def _():
        q = rotated()
        z_ref[...] = q
        zb_ref[...] = q.astype(BF16)

    @pl.when(s == SEC_K)
    def _():
        kd = _times_row_pattern(rotated(), kdec_ref[...])
        z_ref[...] = kd
        zb_ref[...] = kd.astype(BF16)

    @pl.when((s >= SEC_V) & (s < SEC_G))
    def _():
        z_ref[...] = z
        zb_ref[...] = z.astype(BF16)

    @pl.when((s >= SEC_G) & (s < SEC_U))
    def _():
        z_ref[...] = z * _sigmoid(z)

    @pl.when(s == SEC_U)
    def _():
        z_ref[...] = _gelu(z)

    @pl.when(s == SEC_GV)
    def _():
        gv = _layernorm(_gelu(z), ggm_ref[...])
        z_ref[...] = gv
        gv_ref[...] = gv

    @pl.when(s >= SEC_GA)
    def _():
        z_ref[...] = _sigmoid(z)


def _sample_proj(x, cos, sin, kdec, g_mix, g_gm, w_in):
    t = x.shape[0]
    section = pl.BlockSpec((D_MODEL, SECTION), lambda s: (0, s))
    return pl.pallas_call(
        _sample_proj_kernel,
        grid=(IN_W // SECTION,),
        in_specs=[_resident(x.shape), _resident(cos.shape), _resident(sin.shape),
                  _resident(kdec.shape), _resident(g_mix.shape), _resident(g_gm.shape), section],
        out_specs=[pl.BlockSpec((t, SECTION), lambda s: (0, s)),
                   pl.BlockSpec((t, SECTION), lambda s: (0, jnp.minimum(s, SEC_G - 1))),
                   pl.BlockSpec((t, GM_WIDTH), lambda s: (0, 0)),
                   section],
        out_shape=[jax.ShapeDtypeStruct((t, IN_W), F32),
                   jax.ShapeDtypeStruct((t, OFF_G), BF16),
                   jax.ShapeDtypeStruct((t, GM_WIDTH), F32),
                   jax.ShapeDtypeStruct(w_in.shape, BF16)],
        scratch_shapes=[pltpu.VMEM((t, D_MODEL), BF16)],
        compiler_params=pltpu.CompilerParams(
            dimension_semantics=("arbitrary",), vmem_limit_bytes=VMEM_LIMIT),
        name="sample_proj",
    )(x, cos, sin, kdec, g_mix, g_gm, w_in)


def _sample_ret_sc(z_s, state_u, ch_decay, dec_seq):
    ts = z_s.shape[0]
    n_tok = dec_seq
    pair = SC_TOKENS // n_tok
    n_units = (ts // SC_TOKENS) * H_A
    row_blocks = DK // SC_ROWS
    mesh = plsc.VectorSubcoreMesh(core_axis_name="core", subcore_axis_name="subcore",
                                  num_cores=SC_CORES, num_subcores=SC_SUBCORES)
    group = SC_LANES * SC_COLS

    def lanes(x, dtype):
        return jnp.full((SC_LANES,), x, dtype)

    def body(q, kd, v, s_in, s_out, cross):
        unit = pl.program_id(0)
        step = pl.program_id(1)
        head = unit % H_A
        tok0 = (step // row_blocks) * n_tok
        rc = step % row_blocks
        decay = jnp.float32(ch_decay[H_A - 1])
        for h in range(H_A - 2, -1, -1):
            decay = jnp.where(head == h, jnp.float32(ch_decay[h]), decay)
        dvec = lanes(decay, F32)
        tvecs = [lanes(tok0 + t, jnp.int32) for t in range(n_tok)]

        @pl.when(step == 0)
        def _():
            @pl.loop(0, DV, step=SC_LANES)
            def _(c):
                for t in range(SC_TOKENS):
                    cross.at[t, pl.ds(c, SC_LANES)][...] = jnp.zeros((SC_LANES,), F32)

        d0 = lanes(rc * SC_ROWS, jnp.int32)

        def bcast(ref, r):
            return [plsc.load_gather(ref, [tvecs[t], d0 + r]) for t in range(n_tok)]

        @plsc.parallel_loop(0, DV, step=group, unroll=SC_UNROLL)
        def _(c0):
            cols = [pl.ds(c0 + SC_LANES * j, SC_LANES) for j in range(SC_COLS)]
            acc = [cross.at[tok0 + t, cols[j]][...] for t in range(n_tok) for j in range(SC_COLS)]
            for r in range(SC_ROWS):
                qs = bcast(q, r)
                for j in range(SC_COLS):
                    s = s_in.at[0, r, cols[j]][...]
                    for t in range(n_tok):
                        acc[t * SC_COLS + j] = acc[t * SC_COLS + j] + qs[t] * s
            for t in range(n_tok):
                for j in range(SC_COLS):
                    cross.at[tok0 + t, cols[j]][...] = acc[t * SC_COLS + j]

        @plsc.parallel_loop(0, DV, step=group, unroll=SC_UNROLL)
        def _(c0):
            cols = [pl.ds(c0 + SC_LANES * j, SC_LANES) for j in range(SC_COLS)]
            vrow = [[v.at[tok0 + t, cols[j]][...] for j in range(SC_COLS)] for t in range(n_tok)]
            for rb in range(0, SC_ROWS, SC_ROW_BATCH):
                batch = range(rb, rb + SC_ROW_BATCH)
                ks = {r: bcast(kd, r) for r in batch}
                s = {(r, j): s_in.at[0, r, cols[j]][...] for r in batch for j in range(SC_COLS)}
                for r in batch:
                    for j in range(SC_COLS):
                        terms = [s[r, j] * dvec] + [ks[r][t] * vrow[t][j] for t in range(n_tok)]
                        while len(terms) > 1:
                            terms = [terms[i] + terms[i + 1] if i + 1 < len(terms) else terms[i]
                                     for i in range(0, len(terms), 2)]
                        s_out.at[0, r, cols[j]][...] = terms[0]

    @functools.partial(
        pl.kernel, mesh=mesh,
        out_type=[jax.ShapeDtypeStruct(state_u.shape, F32),
                  jax.ShapeDtypeStruct((ts, V_W), F32)],
        compiler_params=pltpu.CompilerParams(needs_layout_passes=False),
        name="sample_ret")
    def call(q_hbm, kd_hbm, v_hbm, s_hbm, snew_hbm, cross_hbm):
        tok = lambda off, w: pl.BlockSpec((SC_TOKENS, w), lambda u, s: (u // H_A, off // w + u % H_A))
        state = pl.BlockSpec(
            (1, SC_ROWS, DV),
            lambda u, s: (((u // H_A) * pair + s // row_blocks) * H_A + u % H_A, s % row_blocks, 0))
        pltpu.emit_pipeline(
            body,
            grid=(n_units, pair * row_blocks),
            in_specs=[tok(OFF_Q, DK), tok(OFF_K, DK), tok(OFF_V, DV), state],
            out_specs=[state, tok(0, DV)],
            core_axis_name=("core", "subcore"),
            dimension_semantics=(pltpu.PARALLEL, pltpu.ARBITRARY),
        )(q_hbm, kd_hbm, v_hbm, s_hbm, snew_hbm, cross_hbm)

    return call(z_s, z_s, z_s, state_u)


N_MERGE_IN = 18


def _sample_tail_kernel(*refs, dec_seq):
    (x_ref, cross_ref, q_ref, kd_ref, v_ref, sg_ref, u_ref, gv_ref, ga_ref, gb_ref,
     dm_ref, ind_ref, gret_ref, coef_ref, bias_ref, wa_ref, wb_ref, wo_ref) = refs[:N_MERGE_IN]
    t = x_ref.shape[0]
    ya = []
    for h in range(H_A):
        cols = slice(h * DV, (h + 1) * DV)
        qh = q_ref[:, h * DK:(h + 1) * DK]
        kh = kd_ref[:, h * DK:(h + 1) * DK]
        sc = _mm_nt(qh, kh) * dm_ref[h]
        o = _mm(sc.astype(BF16), v_ref[:, cols])
        o = o + _times_row_pattern(cross_ref[:, cols], ind_ref[:, cols])
        ya.append((sg_ref[:, cols] * _group_rms(o, gret_ref[:, cols])).astype(BF16))
    ya = jnp.concatenate(ya, axis=-1)

    gv = gv_ref[...]
    sub = coef_ref.shape[1]
    mix = None
    for s in range(dec_seq):
        shifted = gv if s == 0 else pltpu.roll(gv, s, 0)
        term = shifted.reshape(t // sub, sub, GM_WIDTH) * coef_ref[s][None]
        mix = term if mix is None else mix + term
    mix = (mix + bias_ref[...][None]).reshape(t, GM_WIDTH)
    yb = (u_ref[...] * mix).astype(BF16)

    merged = ga_ref[...] * _mm(ya, wa_ref[...]) + gb_ref[...] * _mm(yb, wb_ref[...])
    h_mix = x_ref[...] + _mm(merged.astype(BF16), wo_ref[...])
    _tail_body(h_mix, *refs[N_MERGE_IN:])


def _sample_tail(x, cross, z_s, zb_s, dm, ind, g_ret, coef, bias, w_a, w_b, w_o, p, g_mlp, g_ple, g_fin,
                 w_up, w_down, w_pg, w_pp, dec_seq):
    t = x.shape[0]
    tile = TILE
    row = lambda w: pl.BlockSpec((tile, w), lambda i: (i, 0))
    cols = lambda off, w: pl.BlockSpec((tile, w), lambda i: (i, off // w))
    once = lambda off, w: pl.BlockSpec((tile, w), lambda i: (i, off // w), pipeline_mode=pl.Buffered(1))
    return pl.pallas_call(
        functools.partial(_sample_tail_kernel, dec_seq=dec_seq),
        grid=(t // tile,),
        in_specs=[row(D_MODEL), row(V_W), cols(OFF_Q, Q_W), cols(OFF_K, Q_W), cols(OFF_V, V_W),
                  once(OFF_G, V_W), once(OFF_U, GM_WIDTH),
                  once(OFF_GV, GM_WIDTH), once(OFF_GA, D_MODEL), once(OFF_GB, D_MODEL),
                  _resident(dm.shape), _resident(ind.shape),
                  _resident(g_ret.shape), _resident(coef.shape), _resident(bias.shape),
                  _resident(w_a.shape), _resident(w_b.shape), _resident(w_o.shape)]
                 + _tail_param_specs(tile, g_mlp, g_ple, g_fin, w_up, w_down, w_pg, w_pp),
        out_specs=row(D_MODEL),
        out_shape=jax.ShapeDtypeStruct((t, D_MODEL), F32),
        compiler_params=pltpu.CompilerParams(
            dimension_semantics=("arbitrary",), vmem_limit_bytes=VMEM_LIMIT),
        name="sample_tail",
    )(x, cross, zb_s, zb_s, zb_s, z_s, z_s, z_s, z_s, z_s, dm, ind, g_ret, coef, bias, w_a, w_b, w_o,
      p, g_mlp, g_ple, g_fin, w_up, w_down, w_pg, w_pp)


def _rope_tables(pos):
    half = DK // 2
    inv = ROPE_BASE ** (-jnp.arange(half, dtype=F32) / half)
    ang = pos[:, None] * inv[None, :]
    return jnp.cos(ang), jnp.sin(ang)


def _log_decay():
    return np.log(1.0 - 2.0 ** (-5.0 - np.arange(H_A, dtype=np.float64)))


def _as_f32(a):
    return np.ascontiguousarray(a, dtype=np.float32)


def _decay_tables(c):
    lg = _log_decay()
    idx = np.arange(c, dtype=np.float64)
    diff = idx[:, None] - idx[None, :]
    dmat = np.where(diff[None] >= 0, np.exp(np.maximum(diff, 0.0)[None] * lg[:, None, None]), 0.0)
    in_decay = np.exp((idx[None, :] + 1.0) * lg[:, None])
    st_decay = np.exp((c - 1.0 - idx[None, :]) * lg[:, None])
    rows = lambda v: np.broadcast_to(v[:, :, None], (H_A, c, DK))
    return _as_f32(dmat), _as_f32(rows(in_decay)), _as_f32(rows(st_decay))


def _sample_decay_tables(c, rows, period):
    lg = _log_decay()
    t_of = np.arange(rows) % c
    same = (np.arange(rows)[:, None] // c) == (np.arange(rows)[None, :] // c)
    causal = t_of[:, None] >= t_of[None, :]
    mask = np.where((same & causal)[None],
                    np.exp((t_of[:, None] - (c - 1.0))[None] * lg[:, None, None]), 0.0)
    tp = np.arange(period) % c
    key = np.exp((c - 1.0 - tp)[:, None] * lg[None, :])
    cross = np.exp((tp + 1.0)[:, None] * lg[None, :])
    return (_as_f32(mask), _as_f32(np.repeat(key, DK, axis=1)), _as_f32(np.repeat(cross, DV, axis=1)))


def _chunk_decay(c):
    return tuple(float((1.0 - 2.0 ** (-5.0 - h)) ** c) for h in range(H_A))


def kernel(x_prompt, x_sample, p_prompt, p_sample, state_ret, g_mix, w_in, g_ret, w_s, b_s, g_gm,
           w_br_a, w_br_b, w_o, g_mlp, w_up, w_down, g_ple, w_pg, w_pp, g_final):
    depth = w_in.shape[0]
    assert depth == 1
    batch, seq, _ = x_prompt.shape
    dec_batch, dec_seq, _ = x_sample.shape
    assert 2 * dec_seq == SC_TOKENS
    i = 0

    w_a_b = w_br_a[i].astype(BF16)
    w_b_b = w_br_b[i].astype(BF16)
    w_o_b = w_o[i].astype(BF16)
    w_pp_b = w_pp[i].astype(BF16)
    gmix = g_mix[i][None]
    gret = g_ret[i][None]
    ggm = g_gm[i][None]
    gmlp = g_mlp[i][None]
    gple = g_ple[i][None]
    gfin = g_final[None]

    ts = dec_batch * dec_seq
    sub = 2 * dec_seq
    xs = x_sample.reshape(ts, D_MODEL)
    pos_s = PAST_LEN + jnp.arange(dec_seq, dtype=F32)
    cos_s, sin_s = _rope_tables(pos_s)
    cos_s = jnp.tile(cos_s, (dec_batch, 1))
    sin_s = jnp.tile(sin_s, (dec_batch, 1))
    dm_s, kdec_s, ind_s = _sample_decay_tables(dec_seq, TILE, sub)
    z_s, zb_s, gv_s, w_in_b = _sample_proj(xs, cos_s, sin_s, kdec_s, gmix, ggm, w_in[i])

    cos_p, sin_p = _rope_tables(jnp.arange(seq, dtype=F32))
    dmat, ind_p, std_p = _decay_tables(TILE)
    tril = np.tril(np.ones((GM_CHUNK, GM_CHUNK), dtype=bool))
    ws_tril = jnp.where(tril[None], w_s[i], 0.0).astype(BF16)
    bias_p = jnp.repeat(jnp.transpose(b_s[i]), GM_GC, axis=1)
    h_p, s_p, w_up_b, w_down_b, w_pg_b = _prompt_mix(
        x_prompt, cos_p, sin_p, dmat, ind_p, std_p, _chunk_decay(TILE),
        gmix, gret, ggm, w_in_b, ws_tril, bias_p, w_a_b, w_b_b, w_o_b, w_up[i], w_down[i], w_pg[i])
    y_p = _tail(h_p.reshape(batch * seq, D_MODEL), p_prompt[i].reshape(batch * seq, PLE_DIM),
                gmlp, gple, gfin, w_up_b, w_down_b, w_pg_b, w_pp_b)

    s_s, cross_s = _sample_ret_sc(
        z_s, state_ret[i].reshape(dec_batch * H_A, DK, DV), _chunk_decay(dec_seq), dec_seq)

    w_small = jnp.where(tril[None, :dec_seq, :dec_seq], w_s[i][:, :dec_seq, :dec_seq], 0.0)
    tpos = jnp.arange(sub) % dec_seq
    coef = []
    for s in range(dec_seq):
        src = tpos - s
        c = jnp.where((src >= 0)[None, :], w_small[:, tpos, jnp.maximum(src, 0)], 0.0)
        coef.append(jnp.repeat(jnp.transpose(c), GM_GC, axis=1))
    coef = jnp.stack(coef, axis=0)
    bias_s = jnp.repeat(jnp.transpose(b_s[i][:, tpos]), GM_GC, axis=1)
    y_s = _sample_tail(xs, cross_s, z_s, zb_s, dm_s, ind_s, gret, coef, bias_s, w_a_b, w_b_b, w_o_b,
                       p_sample[i].reshape(ts, PLE_DIM), gmlp, gple, gfin,
                       w_up_b, w_down_b, w_pg_b, w_pp_b, dec_seq)

    return (y_p.reshape(batch, seq, D_MODEL),
            y_s.reshape(dec_batch, dec_seq, D_MODEL),
            s_p[None],
            s_s.reshape(dec_batch, H_A, DK, DV)[None],
            gv_s.reshape(dec_batch, dec_seq, GM_WIDTH)[None])
```

```python
import functools

import numpy as np
import jax
import jax.numpy as jnp
from jax import lax
from jax.experimental import pallas as pl
from jax.experimental.pallas import tpu as pltpu
from jax.experimental.pallas import tpu_sc as plsc

D_MODEL = 1024
H_A = 4
DK = 256
DV = 512
Q_W = H_A * DK
V_W = H_A * DV
GM_WIDTH = 1024
GM_GROUPS = 8
GM_GC = GM_WIDTH // GM_GROUPS
GM_CHUNK = 128
D_FF = 4 * D_MODEL
PLE_DIM = 256
EPS = 1e-6
ROPE_BASE = 10000.0
PAST_LEN = 16384

OFF_Q = 0
OFF_K = OFF_Q + Q_W
OFF_V = OFF_K + Q_W
OFF_G = OFF_V + V_W
OFF_U = OFF_G + V_W
OFF_GV = OFF_U + GM_WIDTH
OFF_GA = OFF_GV + GM_WIDTH
OFF_GB = OFF_GA + D_MODEL
IN_W = OFF_GB + D_MODEL
SECTION = 1024
SEC_K, SEC_V, SEC_G, SEC_U, SEC_GV, SEC_GA = (
    OFF_K // SECTION, OFF_V // SECTION, OFF_G // SECTION, OFF_U // SECTION, OFF_GV // SECTION,
    OFF_GA // SECTION)

TILE = 256
TAIL_TILE = 1024
TAIL_FF_CHUNKS = 4
VMEM_LIMIT = 56 * 1024 * 1024

SC_CORES = 2
SC_SUBCORES = 16
SC_LANES = 16
SC_ROWS = 16
SC_COLS = 4
SC_UNROLL = 2
SC_ROW_BATCH = 2
SC_TOKENS = 8

BF16 = jnp.bfloat16
F32 = jnp.float32


def _mm(a, b):
    return jnp.dot(a, b, preferred_element_type=F32)


def _mm_nt(a, b):
    return lax.dot_general(a, b, (((1,), (1,)), ((), ())), preferred_element_type=F32)


def _mm_tn(a, b):
    return lax.dot_general(a, b, (((0,), (0,)), ((), ())), preferred_element_type=F32)


def _rms(x, g):
    return x * lax.rsqrt(jnp.mean(x * x, axis=-1, keepdims=True) + EPS) * g


def _sigmoid(x):
    return 1.0 / (1.0 + jnp.exp(-x))


def _gelu(x):
    return 0.5 * x * (1.0 + lax.erf(x * (2.0 ** -0.5)))


def _layernorm(x, g):
    mu = jnp.mean(x, axis=-1, keepdims=True)
    xc = x - mu
    return xc * lax.rsqrt(jnp.mean(xc * xc, axis=-1, keepdims=True) + EPS) * g


def _rotary(z, cos, sin):
    half = DK // 2
    z1, z2 = z[:, :half], z[:, half:]
    return jnp.concatenate([z1 * cos - z2 * sin, z2 * cos + z1 * sin], axis=-1)


def _group_rms(o, g):
    return o * lax.rsqrt(jnp.mean(o * o, axis=-1, keepdims=True) + EPS) * g


def _times_row_pattern(x, pattern):
    t, w = x.shape
    p = pattern.shape[0]
    return (x.reshape(t // p, p, w) * pattern[None]).reshape(t, w)


def _prompt_mix_kernel(x_ref, cos_ref, sin_ref, dmat_ref, ind_ref, std_ref, gmix_ref, gret_ref,
                       ggm_ref, win_ref, ws_ref, bias_ref, wa_ref, wb_ref, wo_ref,
                       wup_f_ref, wdown_f_ref, wpg_f_ref,
                       h_ref, s_ref, wup_b_ref, wdown_b_ref, wpg_b_ref, ya_ref, yb_ref, *, ch_decay):
    j = pl.program_id(1)

    wup_b_ref[...] = wup_f_ref[...].astype(BF16)
    wdown_b_ref[...] = wdown_f_ref[...].astype(BF16)
    wpg_b_ref[...] = wpg_f_ref[...].astype(BF16)

    @pl.when(j == 0)
    def _():
        s_ref[...] = jnp.zeros_like(s_ref)

    x = x_ref[0]
    n = _rms(x, gmix_ref[...]).astype(BF16)
    cos = cos_ref[...]
    sin = sin_ref[...]

    u = _gelu(_mm(n, win_ref[:, OFF_U:OFF_U + GM_WIDTH]))
    gv = _layernorm(_gelu(_mm(n, win_ref[:, OFF_GV:OFF_GV + GM_WIDTH])), ggm_ref[...])
    gvb = gv.astype(BF16)

    zq = _mm(n, win_ref[:, OFF_Q:OFF_Q + Q_W])
    zk = _mm(n, win_ref[:, OFF_K:OFF_K + Q_W])
    zg = _mm(n, win_ref[:, OFF_G:OFF_G + V_W])

    qb, kb, qd, kd = [], [], [], []
    for h in range(H_A):
        qr = _rotary(zq[:, h * DK:(h + 1) * DK], cos, sin)
        kr = _rotary(zk[:, h * DK:(h + 1) * DK], cos, sin)
        qb.append(qr.astype(BF16))
        kb.append(kr.astype(BF16))
        qd.append((qr * ind_ref[h]).astype(BF16))
        kd.append((kr * std_ref[h]).astype(BF16))

    zv = _mm(n, win_ref[:, OFF_V:OFF_V + V_W]).astype(BF16)
    out_gate = zg * _sigmoid(zg)

    for g in range(GM_GROUPS):
        cols = slice(g * GM_GC, (g + 1) * GM_GC)
        pair = jnp.concatenate([gvb[:GM_CHUNK, cols], gvb[GM_CHUNK:, cols]], axis=-1)
        sg = _mm(ws_ref[g], pair)
        for c in range(TILE // GM_CHUNK):
            rows = slice(c * GM_CHUNK, (c + 1) * GM_CHUNK)
            sg_c = sg[:, c * GM_GC:(c + 1) * GM_GC] + bias_ref[:, cols]
            yb_ref[rows, cols] = (u[rows, cols] * sg_c).astype(BF16)

    ga = gb = None
    for h in range(H_A):
        vb = zv[:, h * DV:(h + 1) * DV]
        sc = _mm_nt(qb[h], kb[h]) * dmat_ref[h]
        s_old = s_ref[0, h]
        o = _mm(sc.astype(BF16), vb) + _mm(qd[h], s_old.astype(BF16))
        s_ref[0, h] = s_old * ch_decay[h] + _mm_tn(kd[h], vb)
        on = _group_rms(o, gret_ref[:, h * DV:(h + 1) * DV])
        ya_ref[:, h * DV:(h + 1) * DV] = (out_gate[:, h * DV:(h + 1) * DV] * on).astype(BF16)
        if h == 1:
            ga = _sigmoid(_mm(n, win_ref[:, OFF_GA:OFF_GA + D_MODEL]))
        if h == 2:
            gb = _sigmoid(_mm(n, win_ref[:, OFF_GB:OFF_GB + D_MODEL]))
    merged = ga * _mm(ya_ref[...], wa_ref[...]) + gb * _mm(yb_ref[...], wb_ref[...])
    h_ref[0] = x + _mm(merged.astype(BF16), wo_ref[...])


def _resident(shape):
    zeros = (0,) * len(shape)
    return pl.BlockSpec(shape, lambda *_: zeros, pipeline_mode=pl.Buffered(1))


def _prompt_mix(x, cos, sin, dmat, ind, std, ch_decay, g_mix, g_ret, g_gm, w_in, ws_tril, bias, w_a, w_b, w_o,
                w_up, w_down, w_pg):
    b, L, _ = x.shape
    steps = L // TILE
    total = b * steps

    def slab(w):
        rows = w.shape[0] // total
        assert rows * total == w.shape[0] and rows % 16 == 0
        return pl.BlockSpec((rows, w.shape[1]), lambda i, j: (i * steps + j, 0))

    return pl.pallas_call(
        functools.partial(_prompt_mix_kernel, ch_decay=ch_decay),
        grid=(b, steps),
        in_specs=[
            pl.BlockSpec((1, TILE, D_MODEL), lambda i, j: (i, j, 0)),
            pl.BlockSpec((TILE, DK // 2), lambda i, j: (j, 0)),
            pl.BlockSpec((TILE, DK // 2), lambda i, j: (j, 0)),
            _resident(dmat.shape),
            _resident(ind.shape),
            _resident(std.shape),
            _resident(g_mix.shape),
            _resident(g_ret.shape),
            _resident(g_gm.shape),
            _resident(w_in.shape),
            _resident(ws_tril.shape),
            _resident(bias.shape),
            _resident(w_a.shape),
            _resident(w_b.shape),
            _resident(w_o.shape),
            slab(w_up),
            slab(w_down),
            slab(w_pg),
        ],
        out_specs=[
            pl.BlockSpec((1, TILE, D_MODEL), lambda i, j: (i, j, 0)),
            pl.BlockSpec((1, H_A, DK, DV), lambda i, j: (i, 0, 0, 0)),
            slab(w_up),
            slab(w_down),
            slab(w_pg),
        ],
        out_shape=[
            jax.ShapeDtypeStruct((b, L, D_MODEL), F32),
            jax.ShapeDtypeStruct((b, H_A, DK, DV), F32),
            jax.ShapeDtypeStruct(w_up.shape, BF16),
            jax.ShapeDtypeStruct(w_down.shape, BF16),
            jax.ShapeDtypeStruct(w_pg.shape, BF16),
        ],
        scratch_shapes=[
            pltpu.VMEM((TILE, V_W), BF16),
            pltpu.VMEM((TILE, GM_WIDTH), BF16),
        ],
        compiler_params=pltpu.CompilerParams(
            dimension_semantics=("arbitrary", "arbitrary"), vmem_limit_bytes=VMEM_LIMIT),
        name="prompt_mix",
    )(x, cos, sin, dmat, ind, std, g_mix, g_ret, g_gm, w_in, ws_tril, bias, w_a, w_b, w_o,
      w_up, w_down, w_pg)


def _tail_body(h, p_ref, gmlp_ref, gple_ref, gfin_ref, wup_ref, wdown_ref, wpg_ref, wpp_ref, y_ref,
               ff_chunks=1):
    emb = _mm(p_ref[...].reshape(h.shape[0], PLE_DIM).astype(BF16), wpp_ref[...])
    n2 = _rms(h, gmlp_ref[...]).astype(BF16)
    width = D_FF // ff_chunks
    for c in range(ff_chunks):
        cols = slice(c * width, (c + 1) * width)
        a = jnp.maximum(_mm(n2, wup_ref[:, cols]), 0.0)
        h = h + _mm((a * a).astype(BF16), wdown_ref[cols, :])
    n3 =_rms(h, gple_ref[...]).astype(BF16)
    gate = _sigmoid(_mm(n3, wpg_ref[...]))
    h = h + gate * emb
    y_ref[...] = _rms(h, gfin_ref[...]).reshape(y_ref.shape)


def _tail_kernel(h_ref, *refs):
    _tail_body(h_ref[...], *refs, ff_chunks=TAIL_FF_CHUNKS)


def _tail_param_specs(p_spec, g_mlp, g_ple, g_fin, w_up, w_down, w_pg, w_pp):
    return [
        p_spec,
        _resident(g_mlp.shape),
        _resident(g_ple.shape),
        _resident(g_fin.shape),
        _resident(w_up.shape),
        _resident(w_down.shape),
        _resident(w_pg.shape),
        _resident(w_pp.shape),
    ]


def _tail(h, p, g_mlp, g_ple, g_fin, w_up, w_down, w_pg, w_pp):
    t = h.shape[0]
    tile = TAIL_TILE
    assert t % tile == 0
    return pl.pallas_call(
        _tail_kernel,
        grid=(t // tile,),
        in_specs=[pl.BlockSpec((tile, D_MODEL), lambda i: (i, 0))] + _tail_param_specs(
            pl.BlockSpec((tile, PLE_DIM), lambda i: (i, 0)),
            g_mlp, g_ple, g_fin, w_up, w_down, w_pg, w_pp),
        out_specs=pl.BlockSpec((tile, D_MODEL), lambda i: (i, 0)),
        out_shape=jax.ShapeDtypeStruct((t, D_MODEL), F32),
        compiler_params=pltpu.CompilerParams(
            dimension_semantics=("arbitrary",), vmem_limit_bytes=VMEM_LIMIT),
        name="tail",
    )(h, p, g_mlp, g_ple, g_fin, w_up, w_down, w_pg, w_pp)


def _sample_proj_kernel(x_ref, cos_ref, sin_ref, kdec_ref, gmix_ref, ggm_ref, win_f_ref,
                        z_ref, zb_ref, gv_ref, win_b_ref, n_ref):
    s = pl.program_id(0)

    @pl.when(s == 0)
    def _():
        x = x_ref[...].reshape(n_ref.shape)
        n_ref[...] = _rms(x, gmix_ref[...]).astype(BF16)

    scale = jnp.where(s == SEC_K, DK ** -0.5, 1.0).astype(F32)
    wb = (win_f_ref[...] * scale).astype(BF16)
    win_b_ref[...] = wb
    z = _mm(n_ref[...], wb)

    def rotated():
        cos = cos_ref[...]
        sin = sin_ref[...]
        return jnp.concatenate(
            [_rotary(z[:, h * DK:(h + 1) * DK], cos, sin) for h in range(H_A)], axis=-1)

    @pl.when(s < SEC_K)
    def _():
        q = rotated()
        z_ref[...] = q
        zb_ref[...] = q.astype(BF16)

    @pl.when(s == SEC_K)
    def _():
        kd = _times_row_pattern(rotated(), kdec_ref[...])
        z_ref[...] = kd
        zb_ref[...] = kd.astype(BF16)

    @pl.when((s >= SEC_V) & (s < SEC_G))
    def _():
        z_ref[...] = z
        zb_ref[...] = z.astype(BF16)

    @pl.when((s >= SEC_G) & (s < SEC_U))
    def _():
        z_ref[...] = z * _sigmoid(z)

    @pl.when(s == SEC_U)
    def _():
        z_ref[...] = _gelu(z)

    @pl.when(s == SEC_GV)
    def _():
        gv = _layernorm(_gelu(z), ggm_ref[...])
        z_ref[...] = gv
        gv_ref[...] = gv.reshape(gv_ref.shape)

    @pl.when(s >= SEC_GA)
    def _():
        z_ref[...] = _sigmoid(z)


def _sample_proj(x, cos, sin, kdec, g_mix, g_gm, w_in, dec_seq):
    t = x.shape[0] * x.shape[1]
    section = pl.BlockSpec((D_MODEL, SECTION), lambda s: (0, s))
    return pl.pallas_call(
        _sample_proj_kernel,
        grid=(IN_W // SECTION,),
        in_specs=[_resident(x.shape), _resident(cos.shape), _resident(sin.shape),
                  _resident(kdec.shape), _resident(g_mix.shape), _resident(g_gm.shape), section],
        out_specs=[pl.BlockSpec((t, SECTION), lambda s: (0, s)),
                   pl.BlockSpec((t, SECTION), lambda s: (0, jnp.minimum(s, SEC_G - 1))),
                   pl.BlockSpec((t // dec_seq, dec_seq, GM_WIDTH), lambda s: (0, 0, 0)),
                   section],
        out_shape=[jax.ShapeDtypeStruct((t, IN_W), F32),
                   jax.ShapeDtypeStruct((t, OFF_G), BF16),
                   jax.ShapeDtypeStruct((t // dec_seq, dec_seq, GM_WIDTH), F32),
                   jax.ShapeDtypeStruct(w_in.shape, BF16)],
        scratch_shapes=[pltpu.VMEM((t, D_MODEL), BF16)],
        compiler_params=pltpu.CompilerParams(
            dimension_semantics=("arbitrary",), vmem_limit_bytes=VMEM_LIMIT),
        name="sample_proj",
    )(x, cos, sin, kdec, g_mix, g_gm, w_in)


def _sample_ret_sc(z_s, state_u, ch_decay, dec_seq):
    ts = z_s.shape[0]
    n_tok = dec_seq
    pair = SC_TOKENS // n_tok
    n_units = (ts // SC_TOKENS) * H_A
    row_blocks = DK // SC_ROWS
    mesh = plsc.VectorSubcoreMesh(core_axis_name="core", subcore_axis_name="subcore",
                                  num_cores=SC_CORES, num_subcores=SC_SUBCORES)
    group = SC_LANES * SC_COLS

    def lanes(x, dtype):
        return jnp.full((SC_LANES,), x, dtype)

    def body(q, kd, v, s_in, s_out, cross):
        unit = pl.program_id(0)
        step = pl.program_id(1)
        head = unit % H_A
        tok0 = (step // row_blocks) * n_tok
        rc = step % row_blocks
        decay = jnp.float32(ch_decay[H_A - 1])
        for h in range(H_A - 2, -1, -1):
            decay = jnp.where(head == h, jnp.float32(ch_decay[h]), decay)
        dvec = lanes(decay, F32)
        tvecs = [lanes(tok0 + t, jnp.int32) for t in range(n_tok)]

        @pl.when(step == 0)
        def _():
            @pl.loop(0, DV, step=SC_LANES)
            def _(c):
                for t in range(SC_TOKENS):
                    cross.at[t, pl.ds(c, SC_LANES)][...] = jnp.zeros((SC_LANES,), F32)

        d0 = lanes(rc * SC_ROWS, jnp.int32)

        def bcast(ref, r):
            return [plsc.load_gather(ref, [tvecs[t], d0 + r]) for t in range(n_tok)]

        @plsc.parallel_loop(0, DV, step=group, unroll=SC_UNROLL)
        def _(c0):
            cols = [pl.ds(c0 + SC_LANES * j, SC_LANES) for j in range(SC_COLS)]
            acc = [cross.at[tok0 + t, cols[j]][...] for t in range(n_tok) for j in range(SC_COLS)]
            for r in range(SC_ROWS):
                qs = bcast(q, r)
                for j in range(SC_COLS):
                    s = s_in.at[0, r, cols[j]][...]
                    for t in range(n_tok):
                        acc[t * SC_COLS + j] = acc[t * SC_COLS + j] + qs[t] * s
            for t in range(n_tok):
                for j in range(SC_COLS):
                    cross.at[tok0 + t, cols[j]][...] = acc[t * SC_COLS + j]

        @plsc.parallel_loop(0, DV, step=group, unroll=SC_UNROLL)
        def _(c0):
            cols = [pl.ds(c0 + SC_LANES * j, SC_LANES) for j in range(SC_COLS)]
            vrow = [[v.at[tok0 + t, cols[j]][...] for j in range(SC_COLS)] for t in range(n_tok)]
            for rb in range(0, SC_ROWS, SC_ROW_BATCH):
                batch = range(rb, rb + SC_ROW_BATCH)
                ks = {r: bcast(kd, r) for r in batch}
                s = {(r, j): s_in.at[0, r, cols[j]][...] for r in batch for j in range(SC_COLS)}
                for r in batch:
                    for j in range(SC_COLS):
                        terms = [s[r, j] * dvec] + [ks[r][t] * vrow[t][j] for t in range(n_tok)]
                        while len(terms) > 1:
                            terms = [terms[i] + terms[i + 1] if i + 1 < len(terms) else terms[i]
                                     for i in range(0, len(terms), 2)]
                        s_out.at[0, r, cols[j]][...] = terms[0]

    @functools.partial(
        pl.kernel, mesh=mesh,
        out_type=[jax.ShapeDtypeStruct(state_u.shape, F32),
                  jax.ShapeDtypeStruct((ts, V_W), F32)],
        compiler_params=pltpu.CompilerParams(needs_layout_passes=False),
        name="sample_ret")
    def call(q_hbm, kd_hbm, v_hbm, s_hbm, snew_hbm, cross_hbm):
        tok = lambda off, w: pl.BlockSpec((SC_TOKENS, w), lambda u, s: (u // H_A, off // w + u % H_A))
        state = pl.BlockSpec(
            (1, SC_ROWS, DV),
            lambda u, s: (((u // H_A) * pair + s // row_blocks) * H_A + u % H_A, s % row_blocks, 0))
        pltpu.emit_pipeline(
            body,
            grid=(n_units, pair * row_blocks),
            in_specs=[tok(OFF_Q, DK), tok(OFF_K, DK), tok(OFF_V, DV), state],
            out_specs=[state, tok(0, DV)],
            core_axis_name=("core", "subcore"),
            dimension_semantics=(pltpu.PARALLEL, pltpu.ARBITRARY),
        )(q_hbm, kd_hbm, v_hbm, s_hbm, snew_hbm, cross_hbm)

    return call(z_s, z_s, z_s, state_u)


N_MERGE_IN = 18


def _sample_tail_kernel(*refs, dec_seq):
    (x_ref, cross_ref, q_ref, kd_ref, v_ref, sg_ref, u_ref, gv_ref, ga_ref, gb_ref,
     dm_ref, ind_ref, gret_ref, coef_ref, bias_ref, wa_ref, wb_ref, wo_ref) = refs[:N_MERGE_IN]
    t = cross_ref.shape[0]
    ya = []
    for h in range(H_A):
        cols = slice(h * DV, (h + 1) * DV)
        qh = q_ref[:, h * DK:(h + 1) * DK]
        kh = kd_ref[:, h * DK:(h + 1) * DK]
        sc = _mm_nt(qh, kh) * dm_ref[h]
        o = _mm(sc.astype(BF16), v_ref[:, cols])
        o = o + _times_row_pattern(cross_ref[:, cols], ind_ref[:, cols])
        ya.append((sg_ref[:, cols] * _group_rms(o, gret_ref[:, cols])).astype(BF16))
    ya = jnp.concatenate(ya, axis=-1)

    gv = gv_ref[...]
    sub = coef_ref.shape[1]
    mix = None
    for s in range(dec_seq):
        shifted = gv if s == 0 else pltpu.roll(gv, s, 0)
        term = shifted.reshape(t // sub, sub, GM_WIDTH) * coef_ref[s][None]
        mix = term if mix is None else mix + term
    mix = (mix + bias_ref[...][None]).reshape(t, GM_WIDTH)
    yb = (u_ref[...] * mix).astype(BF16)

    merged = ga_ref[...] * _mm(ya, wa_ref[...]) + gb_ref[...] * _mm(yb, wb_ref[...])
    h_mix = x_ref[...].reshape(t, D_MODEL) + _mm(merged.astype(BF16), wo_ref[...])
    _tail_body(h_mix, *refs[N_MERGE_IN:])


def _sample_tail(x, cross, z_s, zb_s, dm, ind, g_ret, coef, bias, w_a, w_b, w_o, p, g_mlp, g_ple, g_fin,
                 w_up, w_down, w_pg, w_pp, dec_seq):
    t = x.shape[0] * dec_seq
    tile = TILE
    row = lambda w: pl.BlockSpec((tile, w), lambda i: (i, 0))
    by_batch = lambda w: pl.BlockSpec((tile // dec_seq, dec_seq, w), lambda i: (i, 0, 0))
    cols = lambda off, w: pl.BlockSpec((tile, w), lambda i: (i, off // w))
    once = lambda off, w: pl.BlockSpec((tile, w), lambda i: (i, off // w), pipeline_mode=pl.Buffered(1))
    return pl.pallas_call(
        functools.partial(_sample_tail_kernel, dec_seq=dec_seq),
        grid=(t // tile,),
        in_specs=[by_batch(D_MODEL), row(V_W), cols(OFF_Q, Q_W), cols(OFF_K, Q_W), cols(OFF_V, V_W),
                  once(OFF_G, V_W), once(OFF_U, GM_WIDTH),
                  once(OFF_GV, GM_WIDTH), once(OFF_GA, D_MODEL), once(OFF_GB, D_MODEL),
                  _resident(dm.shape), _resident(ind.shape),
                  _resident(g_ret.shape), _resident(coef.shape), _resident(bias.shape),
                  _resident(w_a.shape), _resident(w_b.shape), _resident(w_o.shape)]
                 + _tail_param_specs(by_batch(PLE_DIM), g_mlp, g_ple, g_fin, w_up, w_down, w_pg, w_pp),
        out_specs=by_batch(D_MODEL),
        out_shape=jax.ShapeDtypeStruct(x.shape, F32),
        compiler_params=pltpu.CompilerParams(
            dimension_semantics=("arbitrary",), vmem_limit_bytes=VMEM_LIMIT),
        name="sample_tail",
    )(x, cross, zb_s, zb_s, zb_s, z_s, z_s, z_s, z_s, z_s, dm, ind, g_ret, coef, bias, w_a, w_b, w_o,
      p, g_mlp, g_ple, g_fin, w_up, w_down, w_pg, w_pp)


def _rope_tables(pos):
    half = DK // 2
    inv = ROPE_BASE ** (-jnp.arange(half, dtype=F32) / half)
    ang = pos[:, None] * inv[None, :]
    return jnp.cos(ang), jnp.sin(ang)


def _log_decay():
    return np.log(1.0 - 2.0 ** (-5.0 - np.arange(H_A, dtype=np.float64)))


def _as_f32(a):
    return np.ascontiguousarray(a, dtype=np.float32)


def _decay_tables(c):
    lg = _log_decay()
    idx = np.arange(c, dtype=np.float64)
    diff = idx[:, None] - idx[None, :]
    dmat = np.where(diff[None] >= 0, np.exp(np.maximum(diff, 0.0)[None] * lg[:, None, None]), 0.0)
    in_decay = np.exp((idx[None, :] + 1.0) * lg[:, None])
    st_decay = np.exp((c - 1.0 - idx[None, :]) * lg[:, None])
    rows = lambda v: np.broadcast_to(v[:, :, None], (H_A, c, DK))
    return _as_f32(dmat), _as_f32(rows(in_decay)), _as_f32(rows(st_decay))


def _sample_decay_tables(c, rows, period):
    lg = _log_decay()
    t_of = np.arange(rows) % c
    same = (np.arange(rows)[:, None] // c) == (np.arange(rows)[None, :] // c)
    causal = t_of[:, None] >= t_of[None, :]
    mask = np.where((same & causal)[None],
                    np.exp((t_of[:, None] - (c - 1.0))[None] * lg[:, None, None]), 0.0)
    tp = np.arange(period) % c
    key = np.exp((c - 1.0 - tp)[:, None] * lg[None, :])
    cross = np.exp((tp + 1.0)[:, None] * lg[None, :])
    return (_as_f32(mask), _as_f32(np.repeat(key, DK, axis=1)), _as_f32(np.repeat(cross, DV, axis=1)))


def _chunk_decay(c):
    return tuple(float((1.0 - 2.0 ** (-5.0 - h)) ** c) for h in range(H_A))


def kernel(x_prompt, x_sample, p_prompt, p_sample, state_ret, g_mix, w_in, g_ret, w_s, b_s, g_gm,
           w_br_a, w_br_b, w_o, g_mlp, w_up, w_down, g_ple, w_pg, w_pp, g_final):
    depth = w_in.shape[0]
    assert depth == 1
    batch, seq, _ = x_prompt.shape
    dec_batch, dec_seq, _ = x_sample.shape
    assert 2 * dec_seq == SC_TOKENS
    i = 0

    w_a_b = w_br_a[i].astype(BF16)
    w_b_b = w_br_b[i].astype(BF16)
    w_o_b = w_o[i].astype(BF16)
    w_pp_b = w_pp[i].astype(BF16)
    gmix = g_mix[i][None]
    gret = g_ret[i][None]
    ggm = g_gm[i][None]
    gmlp = g_mlp[i][None]
    gple = g_ple[i][None]
    gfin = g_final[None]

    ts = dec_batch * dec_seq
    sub = 2 * dec_seq
    pos_s = PAST_LEN + jnp.arange(dec_seq, dtype=F32)
    cos_s, sin_s = _rope_tables(pos_s)
    cos_s = jnp.tile(cos_s, (dec_batch, 1))
    sin_s = jnp.tile(sin_s, (dec_batch, 1))
    dm_s, kdec_s, ind_s = _sample_decay_tables(dec_seq, TILE, sub)
    z_s, zb_s, gv_s, w_in_b = _sample_proj(x_sample, cos_s, sin_s, kdec_s, gmix, ggm, w_in[i], dec_seq)

    cos_p, sin_p = _rope_tables(jnp.arange(seq, dtype=F32))
    dmat, ind_p, std_p = _decay_tables(TILE)
    tril = np.tril(np.ones((GM_CHUNK, GM_CHUNK), dtype=bool))
    ws_tril = jnp.where(tril[None], w_s[i], 0.0).astype(BF16)
    bias_p = jnp.repeat(jnp.transpose(b_s[i]), GM_GC, axis=1)
    h_p, s_p, w_up_b, w_down_b, w_pg_b = _prompt_mix(
        x_prompt, cos_p, sin_p, dmat, ind_p, std_p, _chunk_decay(TILE),
        gmix, gret, ggm, w_in_b, ws_tril, bias_p, w_a_b, w_b_b, w_o_b, w_up[i], w_down[i], w_pg[i])
    y_p = _tail(h_p.reshape(batch * seq, D_MODEL), p_prompt[i].reshape(batch * seq, PLE_DIM),
                gmlp, gple, gfin, w_up_b, w_down_b, w_pg_b, w_pp_b)

    s_s, cross_s = _sample_ret_sc(
        z_s, state_ret[i].reshape(dec_batch * H_A, DK, DV), _chunk_decay(dec_seq), dec_seq)

    w_small = jnp.where(tril[None, :dec_seq, :dec_seq], w_s[i][:, :dec_seq, :dec_seq], 0.0)
    tpos = jnp.arange(sub) % dec_seq
    coef = []
    for s in range(dec_seq):
        src = tpos - s
        c = jnp.where((src >= 0)[None, :], w_small[:, tpos, jnp.maximum(src, 0)], 0.0)
        coef.append(jnp.repeat(jnp.transpose(c), GM_GC, axis=1))
    coef = jnp.stack(coef, axis=0)
    bias_s = jnp.repeat(jnp.transpose(b_s[i][:, tpos]), GM_GC, axis=1)
    y_s = _sample_tail(x_sample, cross_s, z_s, zb_s, dm_s, ind_s, gret, coef, bias_s, w_a_b, w_b_b, w_o_b,
                       p_sample[i], gmlp, gple, gfin,
                       w_up_b, w_down_b, w_pg_b, w_pp_b, dec_seq)

    return (y_p.reshape(batch, seq, D_MODEL),
            y_s,
            s_p[None],
            s_s.reshape(dec_batch, H_A, DK, DV)[None],
            gv_s[None])
```

```python
import functools

import numpy as np
import jax
import jax.numpy as jnp
from jax import lax
from jax.experimental import pallas as pl
from jax.experimental.pallas import tpu as pltpu
from jax.experimental.pallas import tpu_sc as plsc

D_MODEL = 1024
H_A = 4
DK = 256
DV = 512
Q_W = H_A * DK
V_W = H_A * DV
GM_WIDTH = 1024
GM_GROUPS = 8
GM_GC = GM_WIDTH // GM_GROUPS
GM_CHUNK = 128
D_FF = 4 * D_MODEL
PLE_DIM = 256
EPS = 1e-6
ROPE_BASE = 10000.0
PAST_LEN = 16384

OFF_Q = 0
OFF_K = OFF_Q + Q_W
OFF_V = OFF_K + Q_W
OFF_G = OFF_V + V_W
OFF_U = OFF_G + V_W
OFF_GV = OFF_U + GM_WIDTH
OFF_GA = OFF_GV + GM_WIDTH
OFF_GB = OFF_GA + D_MODEL
IN_W = OFF_GB + D_MODEL
SECTION = 1024
SEC_K, SEC_V, SEC_G, SEC_U, SEC_GV, SEC_GA = (
    OFF_K // SECTION, OFF_V // SECTION, OFF_G // SECTION, OFF_U // SECTION, OFF_GV // SECTION,
    OFF_GA // SECTION)

TILE = 256
TAIL_TILE = 1024
TAIL_FF_CHUNKS = 4
VMEM_LIMIT = 56 * 1024 * 1024

SC_CORES = 2
SC_SUBCORES = 16
SC_LANES = 16
SC_ROWS = 16
SC_COLS = 4
SC_UNROLL = 2
SC_ROW_BATCH = 2
SC_TOKENS = 8

BF16 = jnp.bfloat16
F32 = jnp.float32


def _mm(a, b):
    return jnp.dot(a, b, preferred_element_type=F32)


def _mm_nt(a, b):
    return lax.dot_general(a, b, (((1,), (1,)), ((), ())), preferred_element_type=F32)


def _mm_tn(a, b):
    return lax.dot_general(a, b, (((0,), (0,)), ((), ())), preferred_element_type=F32)


def _rms(x, g):
    return x * lax.rsqrt(jnp.mean(x * x, axis=-1, keepdims=True) + EPS) * g


def _sigmoid(x):
    return 1.0 / (1.0 + jnp.exp(-x))


def _gelu(x):
    return 0.5 * x * (1.0 + lax.erf(x * (2.0 ** -0.5)))


def _layernorm(x, g):
    mu = jnp.mean(x, axis=-1, keepdims=True)
    xc = x - mu
    return xc * lax.rsqrt(jnp.mean(xc * xc, axis=-1, keepdims=True) + EPS) * g


def _rotary(z, cos, sin):
    half = DK // 2
    z1, z2 = z[:, :half], z[:, half:]
    return jnp.concatenate([z1 * cos - z2 * sin, z2 * cos + z1 * sin], axis=-1)


def _group_rms(o, g):
    return o * lax.rsqrt(jnp.mean(o * o, axis=-1, keepdims=True) + EPS) * g


def _times_row_pattern(x, pattern):
    t, w = x.shape
    p = pattern.shape[0]
    return (x.reshape(t // p, p, w) * pattern[None]).reshape(t, w)


def _prompt_mix_kernel(x_ref, cos_ref, sin_ref, dmat_ref, ind_ref, std_ref, gmix_ref, gret_ref,
                       ggm_ref, win_ref, ws_ref, bias_ref, wa_ref, wb_ref, wo_ref,
                       wup_f_ref, wdown_f_ref, wpg_f_ref,
                       h_ref, s_ref, wup_b_ref, wdown_b_ref, wpg_b_ref, ya_ref, yb_ref, *, ch_decay):
    j = pl.program_id(1)

    wup_b_ref[...] = wup_f_ref[...].astype(BF16)
    wdown_b_ref[...] = wdown_f_ref[...].astype(BF16)
    wpg_b_ref[...] = wpg_f_ref[...].astype(BF16)

    @pl.when(j == 0)
    def _():
        s_ref[...] = jnp.zeros_like(s_ref)

    x = x_ref[0]
    n = _rms(x, gmix_ref[...]).astype(BF16)
    cos = cos_ref[...]
    sin = sin_ref[...]

    u = _gelu(_mm(n, win_ref[:, OFF_U:OFF_U + GM_WIDTH]))
    gv = _layernorm(_gelu(_mm(n, win_ref[:, OFF_GV:OFF_GV + GM_WIDTH])), ggm_ref[...])
    gvb = gv.astype(BF16)

    zq = _mm(n, win_ref[:, OFF_Q:OFF_Q + Q_W])
    zk = _mm(n, win_ref[:, OFF_K:OFF_K + Q_W])
    zg = _mm(n, win_ref[:, OFF_G:OFF_G + V_W])

    qb, kb, qd, kd = [], [], [], []
    for h in range(H_A):
        qr = _rotary(zq[:, h * DK:(h + 1) * DK], cos, sin)
        kr = _rotary(zk[:, h * DK:(h + 1) * DK], cos, sin)
        qb.append(qr.astype(BF16))
        kb.append(kr.astype(BF16))
        qd.append((qr * ind_ref[h]).astype(BF16))
        kd.append((kr * std_ref[h]).astype(BF16))

    zv = _mm(n, win_ref[:, OFF_V:OFF_V + V_W]).astype(BF16)
    out_gate = zg * _sigmoid(zg)

    for g in range(GM_GROUPS):
        cols = slice(g * GM_GC, (g + 1) * GM_GC)
        pair = jnp.concatenate([gvb[:GM_CHUNK, cols], gvb[GM_CHUNK:, cols]], axis=-1)
        sg = _mm(ws_ref[g], pair)
        for c in range(TILE // GM_CHUNK):
            rows = slice(c * GM_CHUNK, (c + 1) * GM_CHUNK)
            sg_c = sg[:, c * GM_GC:(c + 1) * GM_GC] + bias_ref[:, cols]
            yb_ref[rows, cols] = (u[rows, cols] * sg_c).astype(BF16)

    ga = gb = None
    for h in range(H_A):
        vb = zv[:, h * DV:(h + 1) * DV]
        sc = _mm_nt(qb[h], kb[h]) * dmat_ref[h]
        s_old = s_ref[0, h]
        o = _mm(sc.astype(BF16), vb) + _mm(qd[h], s_old.astype(BF16))
        s_ref[0, h] = s_old * ch_decay[h] + _mm_tn(kd[h], vb)
        on = _group_rms(o, gret_ref[:, h * DV:(h + 1) * DV])
        ya_ref[:, h * DV:(h + 1) * DV] = (out_gate[:, h * DV:(h + 1) * DV] * on).astype(BF16)
        if h == 1:
            ga = _sigmoid(_mm(n, win_ref[:, OFF_GA:OFF_GA + D_MODEL]))
        if h == 2:
            gb = _sigmoid(_mm(n, win_ref[:, OFF_GB:OFF_GB + D_MODEL]))
    merged = ga * _mm(ya_ref[...], wa_ref[...]) + gb * _mm(yb_ref[...], wb_ref[...])
    h_ref[0] = x + _mm(merged.astype(BF16), wo_ref[...])


def _resident(shape):
    zeros = (0,) * len(shape)
    return pl.BlockSpec(shape, lambda *_: zeros, pipeline_mode=pl.Buffered(1))


def _prompt_mix(x, cos, sin, dmat, ind, std, ch_decay, g_mix, g_ret, g_gm, w_in, ws_tril, bias, w_a, w_b, w_o,
                w_up, w_down, w_pg):
    b, L, _ = x.shape
    steps = L // TILE
    total = b * steps

    def slab(w):
        rows = w.shape[0] // total
        assert rows * total == w.shape[0] and rows % 16 == 0
        return pl.BlockSpec((rows, w.shape[1]), lambda i, j: (i * steps + j, 0))

    return pl.pallas_call(
        functools.partial(_prompt_mix_kernel, ch_decay=ch_decay),
        grid=(b, steps),
        in_specs=[
            pl.BlockSpec((1, TILE, D_MODEL), lambda i, j: (i, j, 0)),
            pl.BlockSpec((TILE, DK // 2), lambda i, j: (j, 0)),
            pl.BlockSpec((TILE, DK // 2), lambda i, j: (j, 0)),
            _resident(dmat.shape),
            _resident(ind.shape),
            _resident(std.shape),
            _resident(g_mix.shape),
            _resident(g_ret.shape),
            _resident(g_gm.shape),
            _resident(w_in.shape),
            _resident(ws_tril.shape),
            _resident(bias.shape),
            _resident(w_a.shape),
            _resident(w_b.shape),
            _resident(w_o.shape),
            slab(w_up),
            slab(w_down),
            slab(w_pg),
        ],
        out_specs=[
            pl.BlockSpec((1, TILE, D_MODEL), lambda i, j: (i, j, 0)),
            pl.BlockSpec((1, H_A, DK, DV), lambda i, j: (i, 0, 0, 0)),
            slab(w_up),
            slab(w_down),
            slab(w_pg),
        ],
        out_shape=[
            jax.ShapeDtypeStruct((b, L, D_MODEL), F32),
            jax.ShapeDtypeStruct((b, H_A, DK, DV), F32),
            jax.ShapeDtypeStruct(w_up.shape, BF16),
            jax.ShapeDtypeStruct(w_down.shape, BF16),
            jax.ShapeDtypeStruct(w_pg.shape, BF16),
        ],
        scratch_shapes=[
            pltpu.VMEM((TILE, V_W), BF16),
            pltpu.VMEM((TILE, GM_WIDTH), BF16),
        ],
        compiler_params=pltpu.CompilerParams(
            dimension_semantics=("arbitrary", "arbitrary"), vmem_limit_bytes=VMEM_LIMIT),
        name="prompt_mix",
    )(x, cos, sin, dmat, ind, std, g_mix, g_ret, g_gm, w_in, ws_tril, bias, w_a, w_b, w_o,
      w_up, w_down, w_pg)


def _tail_body(h, p_ref, gmlp_ref, gple_ref, gfin_ref, wup_ref, wdown_ref, wpg_ref, wpp_ref, y_ref,
               ff_chunks=1):
    emb = _mm(p_ref[...].reshape(h.shape[0], PLE_DIM).astype(BF16), wpp_ref[...])
    n2 = _rms(h, gmlp_ref[...]).astype(BF16)
    width = D_FF // ff_chunks
    for c in range(ff_chunks):
        cols = slice(c * width, (c + 1) * width)
        a = jnp.maximum(_mm(n2, wup_ref[:, cols]), 0.0)
        h = h + _mm((a * a).astype(BF16), wdown_ref[cols, :])
    n3 =_rms(h, gple_ref[...]).astype(BF16)
    gate = _sigmoid(_mm(n3, wpg_ref[...]))
    h = h + gate * emb
    y_ref[...] = _rms(h, gfin_ref[...]).reshape(y_ref.shape)


def _tail_kernel(h_ref, *refs):
    _tail_body(h_ref[...], *refs, ff_chunks=TAIL_FF_CHUNKS)


def _tail_param_specs(p_spec, g_mlp, g_ple, g_fin, w_up, w_down, w_pg, w_pp):
    return [
        p_spec,
        _resident(g_mlp.shape),
        _resident(g_ple.shape),
        _resident(g_fin.shape),
        _resident(w_up.shape),
        _resident(w_down.shape),
        _resident(w_pg.shape),
        _resident(w_pp.shape),
    ]


def _tail(h, p, g_mlp, g_ple, g_fin, w_up, w_down, w_pg, w_pp):
    t = h.shape[0]
    tile = TAIL_TILE
    assert t % tile == 0
    return pl.pallas_call(
        _tail_kernel,
        grid=(t // tile,),
        in_specs=[pl.BlockSpec((tile, D_MODEL), lambda i: (i, 0))] + _tail_param_specs(
            pl.BlockSpec((tile, PLE_DIM), lambda i: (i, 0)),
            g_mlp, g_ple, g_fin, w_up, w_down, w_pg, w_pp),
        out_specs=pl.BlockSpec((tile, D_MODEL), lambda i: (i, 0)),
        out_shape=jax.ShapeDtypeStruct((t, D_MODEL), F32),
        compiler_params=pltpu.CompilerParams(
            dimension_semantics=("arbitrary",), vmem_limit_bytes=VMEM_LIMIT),
        name="tail",
    )(h, p, g_mlp, g_ple, g_fin, w_up, w_down, w_pg, w_pp)


def _sample_proj_kernel(x_ref, cos_ref, sin_ref, kdec_ref, gmix_ref, ggm_ref, win_f_ref,
                        z_ref, zb_ref, gv_ref, win_b_ref, n_ref):
    s = pl.program_id(0)

    @pl.when(s == 0)
    def _():
        x = x_ref[...].reshape(n_ref.shape)
        n_ref[...] = _rms(x, gmix_ref[...]).astype(BF16)

    scale = jnp.where(s == SEC_K, DK ** -0.5, 1.0).astype(F32)
    wb = (win_f_ref[...] * scale).astype(BF16)
    win_b_ref[...] = wb
    z = _mm(n_ref[...], wb)

    def rotated():
        cos = cos_ref[...]
        sin = sin_ref[...]
        return jnp.concatenate(
            [_rotary(z[:, h * DK:(h + 1) * DK], cos, sin) for h in range(H_A)], axis=-1)

    @pl.when(s < SEC_K)
    def _():
        q = rotated()
        z_ref[...] = q
        zb_ref[...] = q.astype(BF16)

    @pl.when(s == SEC_K)
    def _():
        kd = _times_row_pattern(rotated(), kdec_ref[...])
        z_ref[...] = kd
        zb_ref[...] = kd.astype(BF16)

    @pl.when((s >= SEC_V) & (s < SEC_G))
    def _():
        z_ref[...] = z
        zb_ref[...] = z.astype(BF16)

    @pl.when((s >= SEC_G) & (s < SEC_U))
    def _():
        z_ref[...] = z * _sigmoid(z)

    @pl.when(s == SEC_U)
    def _():
        z_ref[...] = _gelu(z)

    @pl.when(s == SEC_GV)
    def _():
        gv = _layernorm(_gelu(z), ggm_ref[...])
        z_ref[...] = gv
        gv_ref[...] = gv.reshape(gv_ref.shape)

    @pl.when(s >= SEC_GA)
    def _():
        z_ref[...] = _sigmoid(z)


def _sample_proj(x, cos, sin, kdec, g_mix, g_gm, w_in, dec_seq):
    t = x.shape[0] * x.shape[1]
    section = pl.BlockSpec((D_MODEL, SECTION), lambda s: (0, s))
    return pl.pallas_call(
        _sample_proj_kernel,
        grid=(IN_W // SECTION,),
        in_specs=[_resident(x.shape), _resident(cos.shape), _resident(sin.shape),
                  _resident(kdec.shape), _resident(g_mix.shape), _resident(g_gm.shape), section],
        out_specs=[pl.BlockSpec((t, SECTION), lambda s: (0, s)),
                   pl.BlockSpec((t, SECTION), lambda s: (0, jnp.minimum(s, SEC_G - 1))),
                   pl.BlockSpec((t // dec_seq, dec_seq, GM_WIDTH), lambda s: (0, 0, 0)),
                   section],
        out_shape=[jax.ShapeDtypeStruct((t, IN_W), F32),
                   jax.ShapeDtypeStruct((t, OFF_G), BF16),
                   jax.ShapeDtypeStruct((t // dec_seq, dec_seq, GM_WIDTH), F32),
                   jax.ShapeDtypeStruct(w_in.shape, BF16)],
        scratch_shapes=[pltpu.VMEM((t, D_MODEL), BF16)],
        compiler_params=pltpu.CompilerParams(
            dimension_semantics=("arbitrary",), vmem_limit_bytes=VMEM_LIMIT),
        name="sample_proj",
    )(x, cos, sin, kdec, g_mix, g_gm, w_in)


def _sample_ret_sc(z_s, state_u, ch_decay, dec_seq):
    ts = z_s.shape[0]
    n_tok = dec_seq
    pair = SC_TOKENS // n_tok
    n_units = (ts // SC_TOKENS) * H_A
    row_blocks = DK // SC_ROWS
    mesh = plsc.VectorSubcoreMesh(core_axis_name="core", subcore_axis_name="subcore",
                                  num_cores=SC_CORES, num_subcores=SC_SUBCORES)
    group = SC_LANES * SC_COLS

    def lanes(x, dtype):
        return jnp.full((SC_LANES,), x, dtype)

    def body(q, kd, v, s_in, s_out, cross):
        unit = pl.program_id(0)
        step = pl.program_id(1)
        head = unit % H_A
        tok0 = (step // row_blocks) * n_tok
        rc = step % row_blocks
        decay = jnp.float32(ch_decay[H_A - 1])
        for h in range(H_A - 2, -1, -1):
            decay = jnp.where(head == h, jnp.float32(ch_decay[h]), decay)
        dvec = lanes(decay, F32)
        tvecs = [lanes(tok0 + t, jnp.int32) for t in range(n_tok)]

        @pl.when(step == 0)
        def _():
            @pl.loop(0, DV, step=SC_LANES)
            def _(c):
                for t in range(SC_TOKENS):
                    cross.at[t, pl.ds(c, SC_LANES)][...] = jnp.zeros((SC_LANES,), F32)

        d0 = lanes(rc * SC_ROWS, jnp.int32)

        def bcast(ref, r):
            return [plsc.load_gather(ref, [tvecs[t], d0 + r]) for t in range(n_tok)]

        @plsc.parallel_loop(0, DV, step=group, unroll=SC_UNROLL)
        def _(c0):
            cols = [pl.ds(c0 + SC_LANES * j, SC_LANES) for j in range(SC_COLS)]
            acc = [cross.at[tok0 + t, cols[j]][...] for t in range(n_tok) for j in range(SC_COLS)]
            for r in range(SC_ROWS):
                qs = bcast(q, r)
                for j in range(SC_COLS):
                    s = s_in.at[0, r, cols[j]][...]
                    for t in range(n_tok):
                        acc[t * SC_COLS + j] = acc[t * SC_COLS + j] + qs[t] * s
            for t in range(n_tok):
                for j in range(SC_COLS):
                    cross.at[tok0 + t, cols[j]][...] = acc[t * SC_COLS + j]

        @plsc.parallel_loop(0, DV, step=group, unroll=SC_UNROLL)
        def _(c0):
            cols = [pl.ds(c0 + SC_LANES * j, SC_LANES) for j in range(SC_COLS)]
            vrow = [[v.at[tok0 + t, cols[j]][...] for j in range(SC_COLS)] for t in range(n_tok)]
            for rb in range(0, SC_ROWS, SC_ROW_BATCH):
                batch = range(rb, rb + SC_ROW_BATCH)
                ks = {r: bcast(kd, r) for r in batch}
                s = {(r, j): s_in.at[0, r, cols[j]][...] for r in batch for j in range(SC_COLS)}
                for r in batch:
                    for j in range(SC_COLS):
                        terms = [s[r, j] * dvec] + [ks[r][t] * vrow[t][j] for t in range(n_tok)]
                        while len(terms) > 1:
                            terms = [terms[i] + terms[i + 1] if i + 1 < len(terms) else terms[i]
                                     for i in range(0, len(terms), 2)]
                        s_out.at[0, r, cols[j]][...] = terms[0]

    @functools.partial(
        pl.kernel, mesh=mesh,
        out_type=[jax.ShapeDtypeStruct(state_u.shape, F32),
                  jax.ShapeDtypeStruct((ts, V_W), F32)],
        compiler_params=pltpu.CompilerParams(needs_layout_passes=False),
        name="sample_ret")
    def call(q_hbm, kd_hbm, v_hbm, s_hbm, snew_hbm, cross_hbm):
        tok = lambda off, w: pl.BlockSpec((SC_TOKENS, w), lambda u, s: (u // H_A, off // w + u % H_A))
        state = pl.BlockSpec(
            (1, SC_ROWS, DV),
            lambda u, s: (((u // H_A) * pair + s // row_blocks) * H_A + u % H_A, s % row_blocks, 0))
        pltpu.emit_pipeline(
            body,
            grid=(n_units, pair * row_blocks),
            in_specs=[tok(OFF_Q, DK), tok(OFF_K, DK), tok(OFF_V, DV), state],
            out_specs=[state, tok(0, DV)],
            core_axis_name=("core", "subcore"),
            dimension_semantics=(pltpu.PARALLEL, pltpu.ARBITRARY),
        )(q_hbm, kd_hbm, v_hbm, s_hbm, snew_hbm, cross_hbm)

    return call(z_s, z_s, z_s, state_u)


N_MERGE_IN = 18


def _sample_tail_kernel(*refs, dec_seq):
    (x_ref, cross_ref, q_ref, kd_ref, v_ref, sg_ref, u_ref, gv_ref, ga_ref, gb_ref,
     dm_ref, ind_ref, gret_ref, coef_ref, bias_ref, wa_ref, wb_ref, wo_ref) = refs[:N_MERGE_IN]
    t = cross_ref.shape[0]
    ya = []
    for h in range(H_A):
        cols = slice(h * DV, (h + 1) * DV)
        qh = q_ref[:, h * DK:(h + 1) * DK]
        kh = kd_ref[:, h * DK:(h + 1) * DK]
        sc = _mm_nt(qh, kh) * dm_ref[h]
        o = _mm(sc.astype(BF16), v_ref[:, cols])
        o = o + _times_row_pattern(cross_ref[:, cols], ind_ref[:, cols])
        ya.append((sg_ref[:, cols] * _group_rms(o, gret_ref[:, cols])).astype(BF16))
    ya = jnp.concatenate(ya, axis=-1)

    gv = gv_ref[...]
    sub = coef_ref.shape[1]
    mix = None
    for s in range(dec_seq):
        shifted = gv if s == 0 else pltpu.roll(gv, s, 0)
        term = shifted.reshape(t // sub, sub, GM_WIDTH) * coef_ref[s][None]
        mix = term if mix is None else mix + term
    mix = (mix + bias_ref[...][None]).reshape(t, GM_WIDTH)
    yb = (u_ref[...] * mix).astype(BF16)

    merged = ga_ref[...] * _mm(ya, wa_ref[...]) + gb_ref[...] * _mm(yb, wb_ref[...])
    h_mix = x_ref[...].reshape(t, D_MODEL) + _mm(merged.astype(BF16), wo_ref[...])
    _tail_body(h_mix, *refs[N_MERGE_IN:])


def _sample_tail(x, cross, z_s, zb_s, dm, ind, g_ret, coef, bias, w_a, w_b, w_o, p, g_mlp, g_ple, g_fin,
                 w_up, w_down, w_pg, w_pp, dec_seq):
    t = x.shape[0] * dec_seq
    tile = TILE
    row = lambda w: pl.BlockSpec((tile, w), lambda i: (i, 0))
    by_batch = lambda w: pl.BlockSpec((tile // dec_seq, dec_seq, w), lambda i: (i, 0, 0))
    cols = lambda off, w: pl.BlockSpec((tile, w), lambda i: (i, off // w))
    once = lambda off, w: pl.BlockSpec((tile, w), lambda i: (i, off // w), pipeline_mode=pl.Buffered(1))
    return pl.pallas_call(
        functools.partial(_sample_tail_kernel, dec_seq=dec_seq),
        grid=(t // tile,),
        in_specs=[by_batch(D_MODEL), row(V_W), cols(OFF_Q, Q_W), cols(OFF_K, Q_W), cols(OFF_V, V_W),
                  once(OFF_G, V_W), once(OFF_U, GM_WIDTH),
                  once(OFF_GV, GM_WIDTH), once(OFF_GA, D_MODEL), once(OFF_GB, D_MODEL),
                  _resident(dm.shape), _resident(ind.shape),
                  _resident(g_ret.shape), _resident(coef.shape), _resident(bias.shape),
                  _resident(w_a.shape), _resident(w_b.shape), _resident(w_o.shape)]
                 + _tail_param_specs(by_batch(PLE_DIM), g_mlp, g_ple, g_fin, w_up, w_down, w_pg, w_pp),
        out_specs=by_batch(D_MODEL),
        out_shape=jax.ShapeDtypeStruct(x.shape, F32),
        compiler_params=pltpu.CompilerParams(
            dimension_semantics=("arbitrary",), vmem_limit_bytes=VMEM_LIMIT),
        name="sample_tail",
    )(x, cross, zb_s, zb_s, zb_s, z_s, z_s, z_s, z_s, z_s, dm, ind, g_ret, coef, bias, w_a, w_b, w_o,
      p, g_mlp, g_ple, g_fin, w_up, w_down, w_pg, w_pp)


def _rope_tables(pos):
    half = DK // 2
    inv = ROPE_BASE ** (-jnp.arange(half, dtype=F32) / half)
    ang = pos[:, None] * inv[None, :]
    return jnp.cos(ang), jnp.sin(ang)


def _log_decay():
    return np.log(1.0 - 2.0 ** (-5.0 - np.arange(H_A, dtype=np.float64)))


def _as_f32(a):
    return np.ascontiguousarray(a, dtype=np.float32)


def _decay_tables(c):
    lg = _log_decay()
    idx = np.arange(c, dtype=np.float64)
    diff = idx[:, None] - idx[None, :]
    dmat = np.where(diff[None] >= 0, np.exp(np.maximum(diff, 0.0)[None] * lg[:, None, None]), 0.0)
    in_decay = np.exp((idx[None, :] + 1.0) * lg[:, None])
    st_decay = np.exp((c - 1.0 - idx[None, :]) * lg[:, None])
    rows = lambda v: np.broadcast_to(v[:, :, None], (H_A, c, DK))
    return _as_f32(dmat), _as_f32(rows(in_decay)), _as_f32(rows(st_decay))


def _sample_decay_tables(c, rows, period):
    lg = _log_decay()
    t_of = np.arange(rows) % c
    same = (np.arange(rows)[:, None] // c) == (np.arange(rows)[None, :] // c)
    causal = t_of[:, None] >= t_of[None, :]
    mask = np.where((same & causal)[None],
                    np.exp((t_of[:, None] - (c - 1.0))[None] * lg[:, None, None]), 0.0)
    tp = np.arange(period) % c
    key = np.exp((c - 1.0 - tp)[:, None] * lg[None, :])
    cross = np.exp((tp + 1.0)[:, None] * lg[None, :])
    return (_as_f32(mask), _as_f32(np.repeat(key, DK, axis=1)), _as_f32(np.repeat(cross, DV, axis=1)))


def _chunk_decay(c):
    return tuple(float((1.0 - 2.0 ** (-5.0 - h)) ** c) for h in range(H_A))


def kernel(x_prompt, x_sample, p_prompt, p_sample, state_ret, g_mix, w_in, g_ret, w_s, b_s, g_gm,
           w_br_a, w_br_b, w_o, g_mlp, w_up, w_down, g_ple, w_pg, w_pp, g_final):
    depth = w_in.shape[0]
    assert depth == 1
    batch, seq, _ = x_prompt.shape
    dec_batch, dec_seq, _ = x_sample.shape
    assert SC_TOKENS % dec_seq == 0 and (dec_batch * dec_seq) % TILE == 0
    i = 0

    w_a_b = w_br_a[i].astype(BF16)
    w_b_b = w_br_b[i].astype(BF16)
    w_o_b = w_o[i].astype(BF16)
    w_pp_b = w_pp[i].astype(BF16)
    gmix = g_mix[i][None]
    gret = g_ret[i][None]
    ggm = g_gm[i][None]
    gmlp = g_mlp[i][None]
    gple = g_ple[i][None]
    gfin = g_final[None]

    ts = dec_batch * dec_seq
    sub = 2 * dec_seq
    pos_s = PAST_LEN + jnp.arange(dec_seq, dtype=F32)
    cos_s, sin_s = _rope_tables(pos_s)
    cos_s = jnp.tile(cos_s, (dec_batch, 1))
    sin_s = jnp.tile(sin_s, (dec_batch, 1))
    dm_s, kdec_s, ind_s = _sample_decay_tables(dec_seq, TILE, sub)
    z_s, zb_s, gv_s, w_in_b = _sample_proj(x_sample, cos_s, sin_s, kdec_s, gmix, ggm, w_in[i], dec_seq)

    cos_p, sin_p = _rope_tables(jnp.arange(seq, dtype=F32))
    dmat, ind_p, std_p = _decay_tables(TILE)
    tril = np.tril(np.ones((GM_CHUNK, GM_CHUNK), dtype=bool))
    ws_tril = jnp.where(tril[None], w_s[i], 0.0).astype(BF16)
    bias_p = jnp.repeat(jnp.transpose(b_s[i]), GM_GC, axis=1)
    h_p, s_p, w_up_b, w_down_b, w_pg_b = _prompt_mix(
        x_prompt, cos_p, sin_p, dmat, ind_p, std_p, _chunk_decay(TILE),
        gmix, gret, ggm, w_in_b, ws_tril, bias_p, w_a_b, w_b_b, w_o_b, w_up[i], w_down[i], w_pg[i])
    y_p = _tail(h_p.reshape(batch * seq, D_MODEL), p_prompt[i].reshape(batch * seq, PLE_DIM),
                gmlp, gple, gfin, w_up_b, w_down_b, w_pg_b, w_pp_b)

    s_s, cross_s = _sample_ret_sc(
        z_s, state_ret[i].reshape(dec_batch * H_A, DK, DV), _chunk_decay(dec_seq), dec_seq)

    w_small = jnp.where(tril[None, :dec_seq, :dec_seq], w_s[i][:, :dec_seq, :dec_seq], 0.0)
    tpos = jnp.arange(sub) % dec_seq
    coef = []
    for s in range(dec_seq):
        src = tpos - s
        c = jnp.where((src >= 0)[None, :], w_small[:, tpos, jnp.maximum(src, 0)], 0.0)
        coef.append(jnp.repeat(jnp.transpose(c), GM_GC, axis=1))
    coef = jnp.stack(coef, axis=0)
    bias_s = jnp.repeat(jnp.transpose(b_s[i][:, tpos]), GM_GC, axis=1)
    y_s = _sample_tail(x_sample, cross_s, z_s, zb_s, dm_s, ind_s, gret, coef, bias_s, w_a_b, w_b_b, w_o_b,
                       p_sample[i], gmlp, gple, gfin,
                       w_up_b, w_down_b, w_pg_b, w_pp_b, dec_seq)

    return (y_p.reshape(batch, seq, D_MODEL),
            y_s,
            s_p[None],
            s_s.reshape(dec_batch, H_A, DK, DV)[None],
            gv_s[None])
```

```python
import functools

import numpy as np
import jax
import jax.numpy as jnp
from jax import lax
from jax.experimental import pallas as pl
from jax.experimental.pallas import tpu as pltpu
from jax.experimental.pallas import tpu_sc as plsc

D_MODEL = 1024
H_A = 4
DK = 256
DV = 512
Q_W = H_A * DK
V_W = H_A * DV
GM_WIDTH = 1024
GM_GROUPS = 8
GM_GC = GM_WIDTH // GM_GROUPS
GM_CHUNK = 128
D_FF = 4 * D_MODEL
PLE_DIM = 256
EPS = 1e-6
ROPE_BASE = 10000.0
PAST_LEN = 16384

OFF_Q = 0
OFF_K = OFF_Q + Q_W
OFF_V = OFF_K + Q_W
OFF_G = OFF_V + V_W
OFF_U = OFF_G + V_W
OFF_GV = OFF_U + GM_WIDTH
OFF_GA = OFF_GV + GM_WIDTH
OFF_GB = OFF_GA + D_MODEL
IN_W = OFF_GB + D_MODEL
SECTION = 1024
SEC_K, SEC_V, SEC_G, SEC_U, SEC_GV, SEC_GA = (
    OFF_K // SECTION, OFF_V // SECTION, OFF_G // SECTION, OFF_U // SECTION, OFF_GV // SECTION,
    OFF_GA // SECTION)

TILE = 256
TAIL_TILE = 1024
TAIL_FF_CHUNKS = 4
VMEM_LIMIT = 56 * 1024 * 1024

SC_CORES = 2
SC_SUBCORES = 16
SC_LANES = 16
SC_ROWS = 16
SC_COLS = 4
SC_UNROLL = 2
SC_ROW_BATCH = 2
SC_TOKENS = 8

BF16 = jnp.bfloat16
F32 = jnp.float32


def _mm(a, b):
    return jnp.dot(a, b, preferred_element_type=F32)


def _mm_nt(a, b):
    return lax.dot_general(a, b, (((1,), (1,)), ((), ())), preferred_element_type=F32)


def _mm_tn(a, b):
    return lax.dot_general(a, b, (((0,), (0,)), ((), ())), preferred_element_type=F32)


def _rms(x, g):
    return x * lax.rsqrt(jnp.mean(x * x, axis=-1, keepdims=True) + EPS) * g


def _sigmoid(x):
    return 1.0 / (1.0 + jnp.exp(-x))


def _gelu(x):
    return 0.5 * x * (1.0 + lax.erf(x * (2.0 ** -0.5)))


def _layernorm(x, g):
    mu = jnp.mean(x, axis=-1, keepdims=True)
    xc = x - mu
    return xc * lax.rsqrt(jnp.mean(xc * xc, axis=-1, keepdims=True) + EPS) * g


def _rotary(z, cos, sin):
    half = DK // 2
    z1, z2 = z[:, :half], z[:, half:]
    return jnp.concatenate([z1 * cos - z2 * sin, z2 * cos + z1 * sin], axis=-1)


def _group_rms(o, g):
    return o * lax.rsqrt(jnp.mean(o * o, axis=-1, keepdims=True) + EPS) * g


def _times_row_pattern(x, pattern):
    t, w = x.shape
    p = pattern.shape[0]
    return (x.reshape(t // p, p, w) * pattern[None]).reshape(t, w)


def _prompt_mix_kernel(x_ref, cos_ref, sin_ref, dmat_ref, ind_ref, std_ref, gmix_ref, gret_ref,
                       ggm_ref, win_ref, ws_ref, bias_ref, wa_ref, wb_ref, wo_ref,
                       wup_f_ref, wdown_f_ref, wpg_f_ref,
                       h_ref, s_ref, wup_b_ref, wdown_b_ref, wpg_b_ref, ya_ref, yb_ref, *, ch_decay):
    j = pl.program_id(1)

    wup_b_ref[...] = wup_f_ref[...].astype(BF16)
    wdown_b_ref[...] = wdown_f_ref[...].astype(BF16)
    wpg_b_ref[...] = wpg_f_ref[...].astype(BF16)

    @pl.when(j == 0)
    def _():
        s_ref[...] = jnp.zeros_like(s_ref)

    x = x_ref[0]
    n = _rms(x, gmix_ref[...]).astype(BF16)
    cos = cos_ref[...]
    sin = sin_ref[...]

    u = _gelu(_mm(n, win_ref[:, OFF_U:OFF_U + GM_WIDTH]))
    gv = _layernorm(_gelu(_mm(n, win_ref[:, OFF_GV:OFF_GV + GM_WIDTH])), ggm_ref[...])
    gvb = gv.astype(BF16)

    zq = _mm(n, win_ref[:, OFF_Q:OFF_Q + Q_W])
    zk = _mm(n, win_ref[:, OFF_K:OFF_K + Q_W])
    zg = _mm(n, win_ref[:, OFF_G:OFF_G + V_W])

    qb, kb, qd, kd = [], [], [], []
    for h in range(H_A):
        qr = _rotary(zq[:, h * DK:(h + 1) * DK], cos, sin)
        kr = _rotary(zk[:, h * DK:(h + 1) * DK], cos, sin)
        qb.append(qr.astype(BF16))
        kb.append(kr.astype(BF16))
        qd.append((qr * ind_ref[h]).astype(BF16))
        kd.append((kr * std_ref[h]).astype(BF16))

    zv = _mm(n, win_ref[:, OFF_V:OFF_V + V_W]).astype(BF16)
    out_gate = zg * _sigmoid(zg)

    for g in range(GM_GROUPS):
        cols = slice(g * GM_GC, (g + 1) * GM_GC)
        pair = jnp.concatenate([gvb[:GM_CHUNK, cols], gvb[GM_CHUNK:, cols]], axis=-1)
        sg = _mm(ws_ref[g], pair)
        for c in range(TILE // GM_CHUNK):
            rows = slice(c * GM_CHUNK, (c + 1) * GM_CHUNK)
            sg_c = sg[:, c * GM_GC:(c + 1) * GM_GC] + bias_ref[:, cols]
            yb_ref[rows, cols] = (u[rows, cols] * sg_c).astype(BF16)

    ga = gb = None
    for h in range(H_A):
        vb = zv[:, h * DV:(h + 1) * DV]
        sc = _mm_nt(qb[h], kb[h]) * dmat_ref[h]
        s_old = s_ref[0, h]
        o = _mm(sc.astype(BF16), vb) + _mm(qd[h], s_old.astype(BF16))
        s_ref[0, h] = s_old * ch_decay[h] + _mm_tn(kd[h], vb)
        on = _group_rms(o, gret_ref[:, h * DV:(h + 1) * DV])
        ya_ref[:, h * DV:(h + 1) * DV] = (out_gate[:, h * DV:(h + 1) * DV] * on).astype(BF16)
        if h == 1:
            ga = _sigmoid(_mm(n, win_ref[:, OFF_GA:OFF_GA + D_MODEL]))
        if h == 2:
            gb = _sigmoid(_mm(n, win_ref[:, OFF_GB:OFF_GB + D_MODEL]))
    merged = ga * _mm(ya_ref[...], wa_ref[...]) + gb * _mm(yb_ref[...], wb_ref[...])
    h_ref[0] = x + _mm(merged.astype(BF16), wo_ref[...])


def _resident(shape):
    zeros = (0,) * len(shape)
    return pl.BlockSpec(shape, lambda *_: zeros, pipeline_mode=pl.Buffered(1))


def _prompt_mix(x, cos, sin, dmat, ind, std, ch_decay, g_mix, g_ret, g_gm, w_in, ws_tril, bias, w_a, w_b, w_o,
                w_up, w_down, w_pg):
    b, L, _ = x.shape
    steps = L // TILE
    total = b * steps

    def slab(w):
        rows = w.shape[0] // total
        assert rows * total == w.shape[0] and rows % 16 == 0
        return pl.BlockSpec((rows, w.shape[1]), lambda i, j: (i * steps + j, 0))

    return pl.pallas_call(
        functools.partial(_prompt_mix_kernel, ch_decay=ch_decay),
        grid=(b, steps),
        in_specs=[
            pl.BlockSpec((1, TILE, D_MODEL), lambda i, j: (i, j, 0)),
            pl.BlockSpec((TILE, DK // 2), lambda i, j: (j, 0)),
            pl.BlockSpec((TILE, DK // 2), lambda i, j: (j, 0)),
            _resident(dmat.shape),
            _resident(ind.shape),
            _resident(std.shape),
            _resident(g_mix.shape),
            _resident(g_ret.shape),
            _resident(g_gm.shape),
            _resident(w_in.shape),
            _resident(ws_tril.shape),
            _resident(bias.shape),
            _resident(w_a.shape),
            _resident(w_b.shape),
            _resident(w_o.shape),
            slab(w_up),
            slab(w_down),
            slab(w_pg),
        ],
        out_specs=[
            pl.BlockSpec((1, TILE, D_MODEL), lambda i, j: (i, j, 0)),
            pl.BlockSpec((1, H_A, DK, DV), lambda i, j: (i, 0, 0, 0)),
            slab(w_up),
            slab(w_down),
            slab(w_pg),
        ],
        out_shape=[
            jax.ShapeDtypeStruct((b, L, D_MODEL), F32),
            jax.ShapeDtypeStruct((b, H_A, DK, DV), F32),
            jax.ShapeDtypeStruct(w_up.shape, BF16),
            jax.ShapeDtypeStruct(w_down.shape, BF16),
            jax.ShapeDtypeStruct(w_pg.shape, BF16),
        ],
        scratch_shapes=[
            pltpu.VMEM((TILE, V_W), BF16),
            pltpu.VMEM((TILE, GM_WIDTH), BF16),
        ],
        compiler_params=pltpu.CompilerParams(
            dimension_semantics=("arbitrary", "arbitrary"), vmem_limit_bytes=VMEM_LIMIT),
        name="prompt_mix",
    )(x, cos, sin, dmat, ind, std, g_mix, g_ret, g_gm, w_in, ws_tril, bias, w_a, w_b, w_o,
      w_up, w_down, w_pg)


def _tail_body(h, p_ref, gmlp_ref, gple_ref, gfin_ref, wup_ref, wdown_ref, wpg_ref, wpp_ref, y_ref,
               ff_chunks=1):
    emb = _mm(p_ref[...].reshape(h.shape[0], PLE_DIM).astype(BF16), wpp_ref[...])
    n2 = _rms(h, gmlp_ref[...]).astype(BF16)
    width = D_FF // ff_chunks
    for c in range(ff_chunks):
        cols = slice(c * width, (c + 1) * width)
        a = jnp.maximum(_mm(n2, wup_ref[:, cols]), 0.0)
        h = h + _mm((a * a).astype(BF16), wdown_ref[cols, :])
    n3 =_rms(h, gple_ref[...]).astype(BF16)
    gate = _sigmoid(_mm(n3, wpg_ref[...]))
    h = h + gate * emb
    y_ref[...] = _rms(h, gfin_ref[...]).reshape(y_ref.shape)


def _tail_kernel(h_ref, *refs):
    _tail_body(h_ref[...], *refs, ff_chunks=TAIL_FF_CHUNKS)


def _tail_param_specs(p_spec, g_mlp, g_ple, g_fin, w_up, w_down, w_pg, w_pp):
    return [
        p_spec,
        _resident(g_mlp.shape),
        _resident(g_ple.shape),
        _resident(g_fin.shape),
        _resident(w_up.shape),
        _resident(w_down.shape),
        _resident(w_pg.shape),
        _resident(w_pp.shape),
    ]


def _tail(h, p, g_mlp, g_ple, g_fin, w_up, w_down, w_pg, w_pp):
    t = h.shape[0]
    tile = TAIL_TILE
    assert t % tile == 0
    return pl.pallas_call(
        _tail_kernel,
        grid=(t // tile,),
        in_specs=[pl.BlockSpec((tile, D_MODEL), lambda i: (i, 0))] + _tail_param_specs(
            pl.BlockSpec((tile, PLE_DIM), lambda i: (i, 0)),
            g_mlp, g_ple, g_fin, w_up, w_down, w_pg, w_pp),
        out_specs=pl.BlockSpec((tile, D_MODEL), lambda i: (i, 0)),
        out_shape=jax.ShapeDtypeStruct((t, D_MODEL), F32),
        compiler_params=pltpu.CompilerParams(
            dimension_semantics=("arbitrary",), vmem_limit_bytes=VMEM_LIMIT),
        name="tail",
    )(h, p, g_mlp, g_ple, g_fin, w_up, w_down, w_pg, w_pp)


def _sample_proj_kernel(x_ref, cos_ref, sin_ref, kdec_ref, gmix_ref, ggm_ref, win_f_ref,
                        z_ref, zb_ref, gv_ref, win_b_ref, n_ref):
    s = pl.program_id(0)

    @pl.when(s == 0)
    def _():
        x = x_ref[...].reshape(n_ref.shape)
        n_ref[...] = _rms(x, gmix_ref[...]).astype(BF16)

    scale = jnp.where(s == SEC_K, DK ** -0.5, 1.0).astype(F32)
    wb = (win_f_ref[...] * scale).astype(BF16)
    win_b_ref[...] = wb
    z = _mm(n_ref[...], wb)

    def rotated():
        cos = cos_ref[...]
        sin = sin_ref[...]
        return jnp.concatenate(
            [_rotary(z[:, h * DK:(h + 1) * DK], cos, sin) for h in range(H_A)], axis=-1)

    @pl.when(s < SEC_K)
    def _():
        q = rotated()
        z_ref[...] = q
        zb_ref[...] = q.astype(BF16)

    @pl.when(s == SEC_K)
    def _():
        kd = _times_row_pattern(rotated(), kdec_ref[...])
        z_ref[...] = kd
        zb_ref[...] = kd.astype(BF16)

    @pl.when((s >= SEC_V) & (s < SEC_G))
    def _():
        z_ref[...] = z
        zb_ref[...] = z.astype(BF16)

    @pl.when((s >= SEC_G) & (s < SEC_U))
    def _():
        z_ref[...] = z * _sigmoid(z)

    @pl.when(s == SEC_U)
    def _():
        z_ref[...] = _gelu(z)

    @pl.when(s == SEC_GV)
    def _():
        gv = _layernorm(_gelu(z), ggm_ref[...])
        z_ref[...] = gv
        gv_ref[...] = gv.reshape(gv_ref.shape)

    @pl.when(s >= SEC_GA)
    def _():
        z_ref[...] = _sigmoid(z)


def _sample_proj(x, cos, sin, kdec, g_mix, g_gm, w_in, dec_seq):
    t = x.shape[0] * x.shape[1]
    section = pl.BlockSpec((D_MODEL, SECTION), lambda s: (0, s))
    return pl.pallas_call(
        _sample_proj_kernel,
        grid=(IN_W // SECTION,),
        in_specs=[_resident(x.shape), _resident(cos.shape), _resident(sin.shape),
                  _resident(kdec.shape), _resident(g_mix.shape), _resident(g_gm.shape), section],
        out_specs=[pl.BlockSpec((t, SECTION), lambda s: (0, s)),
                   pl.BlockSpec((t, SECTION), lambda s: (0, jnp.minimum(s, SEC_G - 1))),
                   pl.BlockSpec((t // dec_seq, dec_seq, GM_WIDTH), lambda s: (0, 0, 0)),
                   section],
        out_shape=[jax.ShapeDtypeStruct((t, IN_W), F32),
                   jax.ShapeDtypeStruct((t, OFF_G), BF16),
                   jax.ShapeDtypeStruct((t // dec_seq, dec_seq, GM_WIDTH), F32),
                   jax.ShapeDtypeStruct(w_in.shape, BF16)],
        scratch_shapes=[pltpu.VMEM((t, D_MODEL), BF16)],
        compiler_params=pltpu.CompilerParams(
            dimension_semantics=("arbitrary",), vmem_limit_bytes=VMEM_LIMIT),
        name="sample_proj",
    )(x, cos, sin, kdec, g_mix, g_gm, w_in)


def _sample_ret_sc(z_s, state_u, ch_decay, dec_seq):
    ts = z_s.shape[0]
    n_tok = dec_seq
    pair = SC_TOKENS // n_tok
    n_units = (ts // SC_TOKENS) * H_A
    row_blocks = DK // SC_ROWS
    mesh = plsc.VectorSubcoreMesh(core_axis_name="core", subcore_axis_name="subcore",
                                  num_cores=SC_CORES, num_subcores=SC_SUBCORES)
    group = SC_LANES * SC_COLS

    def lanes(x, dtype):
        return jnp.full((SC_LANES,), x, dtype)

    def body(q, kd, v, s_in, s_out, cross):
        unit = pl.program_id(0)
        step = pl.program_id(1)
        head = unit % H_A
        tok0 = (step // row_blocks) * n_tok
        rc = step % row_blocks
        decay = jnp.float32(ch_decay[H_A - 1])
        for h in range(H_A - 2, -1, -1):
            decay = jnp.where(head == h, jnp.float32(ch_decay[h]), decay)
        dvec = lanes(decay, F32)
        tvecs = [lanes(tok0 + t, jnp.int32) for t in range(n_tok)]

        @pl.when(step == 0)
        def _():
            @pl.loop(0, DV, step=SC_LANES)
            def _(c):
                for t in range(SC_TOKENS):
                    cross.at[t, pl.ds(c, SC_LANES)][...] = jnp.zeros((SC_LANES,), F32)

        d0 = lanes(rc * SC_ROWS, jnp.int32)

        def bcast(ref, r):
            return [plsc.load_gather(ref, [tvecs[t], d0 + r]) for t in range(n_tok)]

        @plsc.parallel_loop(0, DV, step=group, unroll=SC_UNROLL)
        def _(c0):
            cols = [pl.ds(c0 + SC_LANES * j, SC_LANES) for j in range(SC_COLS)]
            acc = [cross.at[tok0 + t, cols[j]][...] for t in range(n_tok) for j in range(SC_COLS)]
            for r in range(SC_ROWS):
                qs = bcast(q, r)
                for j in range(SC_COLS):
                    s = s_in.at[0, r, cols[j]][...]
                    for t in range(n_tok):
                        acc[t * SC_COLS + j] = acc[t * SC_COLS + j] + qs[t] * s
            for t in range(n_tok):
                for j in range(SC_COLS):
                    cross.at[tok0 + t, cols[j]][...] = acc[t * SC_COLS + j]

        @plsc.parallel_loop(0, DV, step=group, unroll=SC_UNROLL)
        def _(c0):
            cols = [pl.ds(c0 + SC_LANES * j, SC_LANES) for j in range(SC_COLS)]
            vrow = [[v.at[tok0 + t, cols[j]][...] for j in range(SC_COLS)] for t in range(n_tok)]
            for rb in range(0, SC_ROWS, SC_ROW_BATCH):
                batch = range(rb, rb + SC_ROW_BATCH)
                ks = {r: bcast(kd, r) for r in batch}
                s = {(r, j): s_in.at[0, r, cols[j]][...] for r in batch for j in range(SC_COLS)}
                for r in batch:
                    for j in range(SC_COLS):
                        terms = [s[r, j] * dvec] + [ks[r][t] * vrow[t][j] for t in range(n_tok)]
                        while len(terms) > 1:
                            terms = [terms[i] + terms[i + 1] if i + 1 < len(terms) else terms[i]
                                     for i in range(0, len(terms), 2)]
                        s_out.at[0, r, cols[j]][...] = terms[0]

    @functools.partial(
        pl.kernel, mesh=mesh,
        out_type=[jax.ShapeDtypeStruct(state_u.shape, F32),
                  jax.ShapeDtypeStruct((ts, V_W), F32)],
        compiler_params=pltpu.CompilerParams(needs_layout_passes=False),
        name="sample_ret")
    def call(q_hbm, kd_hbm, v_hbm, s_hbm, snew_hbm, cross_hbm):
        tok = lambda off, w: pl.BlockSpec((SC_TOKENS, w), lambda u, s: (u // H_A, off // w + u % H_A))
        state = pl.BlockSpec(
            (1, SC_ROWS, DV),
            lambda u, s: (((u // H_A) * pair + s // row_blocks) * H_A + u % H_A, s % row_blocks, 0))
        pltpu.emit_pipeline(
            body,
            grid=(n_units, pair * row_blocks),
            in_specs=[tok(OFF_Q, DK), tok(OFF_K, DK), tok(OFF_V, DV), state],
            out_specs=[state, tok(0, DV)],
            core_axis_name=("core", "subcore"),
            dimension_semantics=(pltpu.PARALLEL, pltpu.ARBITRARY),
        )(q_hbm, kd_hbm, v_hbm, s_hbm, snew_hbm, cross_hbm)

    return call(z_s, z_s, z_s, state_u)


N_MERGE_IN = 18


def _sample_tail_kernel(*refs, dec_seq):
    (x_ref, cross_ref, q_ref, kd_ref, v_ref, sg_ref, u_ref, gv_ref, ga_ref, gb_ref,
     dm_ref, ind_ref, gret_ref, coef_ref, bias_ref, wa_ref, wb_ref, wo_ref) = refs[:N_MERGE_IN]
    t = cross_ref.shape[0]
    ya = []
    for h in range(H_A):
        cols = slice(h * DV, (h + 1) * DV)
        qh = q_ref[:, h * DK:(h + 1) * DK]
        kh = kd_ref[:, h * DK:(h + 1) * DK]
        sc = _mm_nt(qh, kh) * dm_ref[h]
        o = _mm(sc.astype(BF16), v_ref[:, cols])
        o = o + _times_row_pattern(cross_ref[:, cols], ind_ref[:, cols])
        ya.append((sg_ref[:, cols] * _group_rms(o, gret_ref[:, cols])).astype(BF16))
    ya = jnp.concatenate(ya, axis=-1)

    gv = gv_ref[...]
    sub = coef_ref.shape[1]
    mix = None
    for s in range(dec_seq):
        shifted = gv if s == 0 else pltpu.roll(gv, s, 0)
        term = shifted.reshape(t // sub, sub, GM_WIDTH) * coef_ref[s][None]
        mix = term if mix is None else mix + term
    mix = (mix + bias_ref[...][None]).reshape(t, GM_WIDTH)
    yb = (u_ref[...] * mix).astype(BF16)

    merged = ga_ref[...] * _mm(ya, wa_ref[...]) + gb_ref[...] * _mm(yb, wb_ref[...])
    h_mix = x_ref[...].reshape(t, D_MODEL) + _mm(merged.astype(BF16), wo_ref[...])
    _tail_body(h_mix, *refs[N_MERGE_IN:])


def _sample_tail(x, cross, z_s, zb_s, dm, ind, g_ret, coef, bias, w_a, w_b, w_o, p, g_mlp, g_ple, g_fin,
                 w_up, w_down, w_pg, w_pp, dec_seq):
    t = x.shape[0] * dec_seq
    tile = TILE
    row = lambda w: pl.BlockSpec((tile, w), lambda i: (i, 0))
    by_batch = lambda w: pl.BlockSpec((tile // dec_seq, dec_seq, w), lambda i: (i, 0, 0))
    cols = lambda off, w: pl.BlockSpec((tile, w), lambda i: (i, off // w))
    once = lambda off, w: pl.BlockSpec((tile, w), lambda i: (i, off // w), pipeline_mode=pl.Buffered(1))
    return pl.pallas_call(
        functools.partial(_sample_tail_kernel, dec_seq=dec_seq),
        grid=(t // tile,),
        in_specs=[by_batch(D_MODEL), row(V_W), cols(OFF_Q, Q_W), cols(OFF_K, Q_W), cols(OFF_V, V_W),
                  once(OFF_G, V_W), once(OFF_U, GM_WIDTH),
                  cols(OFF_GV, GM_WIDTH), cols(OFF_GA, D_MODEL), cols(OFF_GB, D_MODEL),
                  _resident(dm.shape), _resident(ind.shape),
                  _resident(g_ret.shape), _resident(coef.shape), _resident(bias.shape),
                  _resident(w_a.shape), _resident(w_b.shape), _resident(w_o.shape)]
                 + _tail_param_specs(by_batch(PLE_DIM), g_mlp, g_ple, g_fin, w_up, w_down, w_pg, w_pp),
        out_specs=by_batch(D_MODEL),
        out_shape=jax.ShapeDtypeStruct(x.shape, F32),
        compiler_params=pltpu.CompilerParams(
            dimension_semantics=("arbitrary",), vmem_limit_bytes=VMEM_LIMIT),
        name="sample_tail",
    )(x, cross, zb_s, zb_s, zb_s, z_s, z_s, z_s, z_s, z_s, dm, ind, g_ret, coef, bias, w_a, w_b, w_o,
      p, g_mlp, g_ple, g_fin, w_up, w_down, w_pg, w_pp)


def _rope_tables(pos):
    half = DK // 2
    inv = ROPE_BASE ** (-jnp.arange(half, dtype=F32) / half)
    ang = pos[:, None] * inv[None, :]
    return jnp.cos(ang), jnp.sin(ang)


def _log_decay():
    return np.log(1.0 - 2.0 ** (-5.0 - np.arange(H_A, dtype=np.float64)))


def _as_f32(a):
    return np.ascontiguousarray(a, dtype=np.float32)


def _decay_tables(c):
    lg = _log_decay()
    idx = np.arange(c, dtype=np.float64)
    diff = idx[:, None] - idx[None, :]
    dmat = np.where(diff[None] >= 0, np.exp(np.maximum(diff, 0.0)[None] * lg[:, None, None]), 0.0)
    in_decay = np.exp((idx[None, :] + 1.0) * lg[:, None])
    st_decay = np.exp((c - 1.0 - idx[None, :]) * lg[:, None])
    rows = lambda v: np.broadcast_to(v[:, :, None], (H_A, c, DK))
    return _as_f32(dmat), _as_f32(rows(in_decay)), _as_f32(rows(st_decay))


def _sample_decay_tables(c, rows, period):
    lg = _log_decay()
    t_of = np.arange(rows) % c
    same = (np.arange(rows)[:, None] // c) == (np.arange(rows)[None, :] // c)
    causal = t_of[:, None] >= t_of[None, :]
    mask = np.where((same & causal)[None],
                    np.exp((t_of[:, None] - (c - 1.0))[None] * lg[:, None, None]), 0.0)
    tp = np.arange(period) % c
    key = np.exp((c - 1.0 - tp)[:, None] * lg[None, :])
    cross = np.exp((tp + 1.0)[:, None] * lg[None, :])
    return (_as_f32(mask), _as_f32(np.repeat(key, DK, axis=1)), _as_f32(np.repeat(cross, DV, axis=1)))


def _chunk_decay(c):
    return tuple(float((1.0 - 2.0 ** (-5.0 - h)) ** c) for h in range(H_A))


def kernel(x_prompt, x_sample, p_prompt, p_sample, state_ret, g_mix, w_in, g_ret, w_s, b_s, g_gm,
           w_br_a, w_br_b, w_o, g_mlp, w_up, w_down, g_ple, w_pg, w_pp, g_final):
    depth = w_in.shape[0]
    assert depth == 1
    batch, seq, _ = x_prompt.shape
    dec_batch, dec_seq, _ = x_sample.shape
    assert SC_TOKENS % dec_seq == 0 and (dec_batch * dec_seq) % TILE == 0
    i = 0

    w_a_b = w_br_a[i].astype(BF16)
    w_b_b = w_br_b[i].astype(BF16)
    w_o_b = w_o[i].astype(BF16)
    w_pp_b = w_pp[i].astype(BF16)
    gmix = g_mix[i][None]
    gret = g_ret[i][None]
    ggm = g_gm[i][None]
    gmlp = g_mlp[i][None]
    gple = g_ple[i][None]
    gfin = g_final[None]

    ts = dec_batch * dec_seq
    sub = 2 * dec_seq
    pos_s = PAST_LEN + jnp.arange(dec_seq, dtype=F32)
    cos_s, sin_s = _rope_tables(pos_s)
    cos_s = jnp.tile(cos_s, (dec_batch, 1))
    sin_s = jnp.tile(sin_s, (dec_batch, 1))
    dm_s, kdec_s, ind_s = _sample_decay_tables(dec_seq, TILE, sub)
    z_s, zb_s, gv_s, w_in_b = _sample_proj(x_sample, cos_s, sin_s, kdec_s, gmix, ggm, w_in[i], dec_seq)

    cos_p, sin_p = _rope_tables(jnp.arange(seq, dtype=F32))
    dmat, ind_p, std_p = _decay_tables(TILE)
    tril = np.tril(np.ones((GM_CHUNK, GM_CHUNK), dtype=bool))
    ws_tril = jnp.where(tril[None], w_s[i], 0.0).astype(BF16)
    bias_p = jnp.repeat(jnp.transpose(b_s[i]), GM_GC, axis=1)
    h_p, s_p, w_up_b, w_down_b, w_pg_b = _prompt_mix(
        x_prompt, cos_p, sin_p, dmat, ind_p, std_p, _chunk_decay(TILE),
        gmix, gret, ggm, w_in_b, ws_tril, bias_p, w_a_b, w_b_b, w_o_b, w_up[i], w_down[i], w_pg[i])
    y_p = _tail(h_p.reshape(batch * seq, D_MODEL), p_prompt[i].reshape(batch * seq, PLE_DIM),
                gmlp, gple, gfin, w_up_b, w_down_b, w_pg_b, w_pp_b)

    s_s, cross_s = _sample_ret_sc(
        z_s, state_ret[i].reshape(dec_batch * H_A, DK, DV), _chunk_decay(dec_seq), dec_seq)

    w_small = jnp.where(tril[None, :dec_seq, :dec_seq], w_s[i][:, :dec_seq, :dec_seq], 0.0)
    tpos = jnp.arange(sub) % dec_seq
    coef = []
    for s in range(dec_seq):
        src = tpos - s
        c = jnp.where((src >= 0)[None, :], w_small[:, tpos, jnp.maximum(src, 0)], 0.0)
        coef.append(jnp.repeat(jnp.transpose(c), GM_GC, axis=1))
    coef = jnp.stack(coef, axis=0)
    bias_s = jnp.repeat(jnp.transpose(b_s[i][:, tpos]), GM_GC, axis=1)
    y_s = _sample_tail(x_sample, cross_s, z_s, zb_s, dm_s, ind_s, gret, coef, bias_s, w_a_b, w_b_b, w_o_b,
                       p_sample[i], gmlp, gple, gfin,
                       w_up_b, w_down_b, w_pg_b, w_pp_b, dec_seq)

    return (y_p.reshape(batch, seq, D_MODEL),
            y_s,
            s_p[None],
            s_s.reshape(dec_batch, H_A, DK, DV)[None],
            gv_s[None])
```

```python
import functools

import numpy as np
import jax
import jax.numpy as jnp
from jax import lax
from jax.experimental import pallas as pl
from jax.experimental.pallas import tpu as pltpu
from jax.experimental.pallas import tpu_sc as plsc

D_MODEL = 1024
H_A = 4
DK = 256
DV = 512
Q_W = H_A * DK
V_W = H_A * DV
GM_WIDTH = 1024
GM_GROUPS = 8
GM_GC = GM_WIDTH // GM_GROUPS
GM_CHUNK = 128
D_FF = 4 * D_MODEL
PLE_DIM = 256
EPS = 1e-6
ROPE_BASE = 10000.0
PAST_LEN = 16384

OFF_Q = 0
OFF_K = OFF_Q + Q_W
OFF_V = OFF_K + Q_W
OFF_G = OFF_V + V_W
OFF_U = OFF_G + V_W
OFF_GV = OFF_U + GM_WIDTH
OFF_GA = OFF_GV + GM_WIDTH
OFF_GB = OFF_GA + D_MODEL
IN_W = OFF_GB + D_MODEL
SECTION = 1024
SEC_K, SEC_V, SEC_G, SEC_U, SEC_GV, SEC_GA = (
    OFF_K // SECTION, OFF_V // SECTION, OFF_G // SECTION, OFF_U // SECTION, OFF_GV // SECTION,
    OFF_GA // SECTION)

TILE = 256
TAIL_TILE = 1024
TAIL_FF_CHUNKS = 4
VMEM_LIMIT = 56 * 1024 * 1024

SC_CORES = 2
SC_SUBCORES = 16
SC_LANES = 16
SC_ROWS = 16
SC_COLS = 4
SC_UNROLL = 2
SC_ROW_BATCH = 2
SC_TOKENS = 8

BF16 = jnp.bfloat16
F32 = jnp.float32


def _mm(a, b):
    return jnp.dot(a, b, preferred_element_type=F32)


def _mm_nt(a, b):
    return lax.dot_general(a, b, (((1,), (1,)), ((), ())), preferred_element_type=F32)


def _mm_tn(a, b):
    return lax.dot_general(a, b, (((0,), (0,)), ((), ())), preferred_element_type=F32)


def _rms(x, g):
    return x * lax.rsqrt(jnp.mean(x * x, axis=-1, keepdims=True) + EPS) * g


def _sigmoid(x):
    return 1.0 / (1.0 + jnp.exp(-x))


def _gelu(x):
    return 0.5 * x * (1.0 + lax.erf(x * (2.0 ** -0.5)))


def _layernorm(x, g):
    mu = jnp.mean(x, axis=-1, keepdims=True)
    xc = x - mu
    return xc * lax.rsqrt(jnp.mean(xc * xc, axis=-1, keepdims=True) + EPS) * g


def _rotary(z, cos, sin):
    half = DK // 2
    z1, z2 = z[:, :half], z[:, half:]
    return jnp.concatenate([z1 * cos - z2 * sin, z2 * cos + z1 * sin], axis=-1)


def _group_rms(o, g):
    return o * lax.rsqrt(jnp.mean(o * o, axis=-1, keepdims=True) + EPS) * g


def _times_row_pattern(x, pattern):
    t, w = x.shape
    p = pattern.shape[0]
    return (x.reshape(t // p, p, w) * pattern[None]).reshape(t, w)


def _prompt_mix_kernel(x_ref, cos_ref, sin_ref, dmat_ref, ind_ref, std_ref, gmix_ref, gret_ref,
                       ggm_ref, win_ref, ws_ref, bias_ref, wa_ref, wb_ref, wo_ref,
                       wup_f_ref, wdown_f_ref, wpg_f_ref,
                       h_ref, s_ref, wup_b_ref, wdown_b_ref, wpg_b_ref, ya_ref, yb_ref, *, ch_decay):
    j = pl.program_id(1)

    @pl.when(j == 0)
    def _():
        s_ref[...] = jnp.zeros_like(s_ref)

    x = x_ref[0]
    n = _rms(x, gmix_ref[...]).astype(BF16)
    cos = cos_ref[...]
    sin = sin_ref[...]

    u = _gelu(_mm(n, win_ref[:, OFF_U:OFF_U + GM_WIDTH]))
    gv = _layernorm(_gelu(_mm(n, win_ref[:, OFF_GV:OFF_GV + GM_WIDTH])), ggm_ref[...])
    gvb = gv.astype(BF16)

    zq = _mm(n, win_ref[:, OFF_Q:OFF_Q + Q_W])
    zk = _mm(n, win_ref[:, OFF_K:OFF_K + Q_W])
    zg = _mm(n, win_ref[:, OFF_G:OFF_G + V_W])

    qb, kb, qd, kd = [], [], [], []
    for h in range(H_A):
        qr = _rotary(zq[:, h * DK:(h + 1) * DK], cos, sin)
        kr = _rotary(zk[:, h * DK:(h + 1) * DK], cos, sin)
        qb.append(qr.astype(BF16))
        kb.append(kr.astype(BF16))
        qd.append((qr * ind_ref[h]).astype(BF16))
        kd.append((kr * std_ref[h]).astype(BF16))

    zv = _mm(n, win_ref[:, OFF_V:OFF_V + V_W]).astype(BF16)
    out_gate = zg * _sigmoid(zg)

    for g in range(GM_GROUPS):
        cols = slice(g * GM_GC, (g + 1) * GM_GC)
        pair = jnp.concatenate([gvb[:GM_CHUNK, cols], gvb[GM_CHUNK:, cols]], axis=-1)
        sg = _mm(ws_ref[g], pair)
        for c in range(TILE // GM_CHUNK):
            rows = slice(c * GM_CHUNK, (c + 1) * GM_CHUNK)
            sg_c = sg[:, c * GM_GC:(c + 1) * GM_GC] + bias_ref[:, cols]
            yb_ref[rows, cols] = (u[rows, cols] * sg_c).astype(BF16)

    ga = gb = None
    for h in range(H_A):
        vb = zv[:, h * DV:(h + 1) * DV]
        sc = _mm_nt(qb[h], kb[h]) * dmat_ref[h]
        s_old = s_ref[0, h]
        o = _mm(sc.astype(BF16), vb) + _mm(qd[h], s_old.astype(BF16))
        s_ref[0, h] = s_old * ch_decay[h] + _mm_tn(kd[h], vb)
        on = _group_rms(o, gret_ref[:, h * DV:(h + 1) * DV])
        ya_ref[:, h * DV:(h + 1) * DV] = (out_gate[:, h * DV:(h + 1) * DV] * on).astype(BF16)
        if h == 0:
            ga = _sigmoid(_mm(n, win_ref[:, OFF_GA:OFF_GA + D_MODEL]))
        if h == 2:
            gb = _sigmoid(_mm(n, win_ref[:, OFF_GB:OFF_GB + D_MODEL]))
    wup_b_ref[...] = wup_f_ref[...].astype(BF16)
    wdown_b_ref[...] = wdown_f_ref[...].astype(BF16)
    wpg_b_ref[...] = wpg_f_ref[...].astype(BF16)

    merged = ga * _mm(ya_ref[...], wa_ref[...]) + gb * _mm(yb_ref[...], wb_ref[...])
    h_ref[0] = x + _mm(merged.astype(BF16), wo_ref[...])


def _resident(shape):
    zeros = (0,) * len(shape)
    return pl.BlockSpec(shape, lambda *_: zeros, pipeline_mode=pl.Buffered(1))


def _prompt_mix(x, cos, sin, dmat, ind, std, ch_decay, g_mix, g_ret, g_gm, w_in, ws_tril, bias, w_a, w_b, w_o,
                w_up, w_down, w_pg):
    b, L, _ = x.shape
    steps = L // TILE
    total = b * steps

    def slab(w):
        rows = w.shape[0] // total
        assert rows * total == w.shape[0] and rows % 16 == 0
        return pl.BlockSpec((rows, w.shape[1]), lambda i, j: (i * steps + j, 0))

    return pl.pallas_call(
        functools.partial(_prompt_mix_kernel, ch_decay=ch_decay),
        grid=(b, steps),
        in_specs=[
            pl.BlockSpec((1, TILE, D_MODEL), lambda i, j: (i, j, 0)),
            pl.BlockSpec((TILE, DK // 2), lambda i, j: (j, 0)),
            pl.BlockSpec((TILE, DK // 2), lambda i, j: (j, 0)),
            _resident(dmat.shape),
            _resident(ind.shape),
            _resident(std.shape),
            _resident(g_mix.shape),
            _resident(g_ret.shape),
            _resident(g_gm.shape),
            _resident(w_in.shape),
            _resident(ws_tril.shape),
            _resident(bias.shape),
            _resident(w_a.shape),
            _resident(w_b.shape),
            _resident(w_o.shape),
            slab(w_up),
            slab(w_down),
            slab(w_pg),
        ],
        out_specs=[
            pl.BlockSpec((1, TILE, D_MODEL), lambda i, j: (i, j, 0)),
            pl.BlockSpec((1, H_A, DK, DV), lambda i, j: (i, 0, 0, 0)),
            slab(w_up),
            slab(w_down),
            slab(w_pg),
        ],
        out_shape=[
            jax.ShapeDtypeStruct((b, L, D_MODEL), F32),
            jax.ShapeDtypeStruct((b, H_A, DK, DV), F32),
            jax.ShapeDtypeStruct(w_up.shape, BF16),
            jax.ShapeDtypeStruct(w_down.shape, BF16),
            jax.ShapeDtypeStruct(w_pg.shape, BF16),
        ],
        scratch_shapes=[
            pltpu.VMEM((TILE, V_W), BF16),
            pltpu.VMEM((TILE, GM_WIDTH), BF16),
        ],
        compiler_params=pltpu.CompilerParams(
            dimension_semantics=("arbitrary", "arbitrary"), vmem_limit_bytes=VMEM_LIMIT),
        name="prompt_mix",
    )(x, cos, sin, dmat, ind, std, g_mix, g_ret, g_gm, w_in, ws_tril, bias, w_a, w_b, w_o,
      w_up, w_down, w_pg)


def _tail_body(h, p_ref, gmlp_ref, gple_ref, gfin_ref, wup_ref, wdown_ref, wpg_ref, wpp_ref, y_ref,
               ff_chunks=1):
    emb = _mm(p_ref[...].reshape(h.shape[0], PLE_DIM).astype(BF16), wpp_ref[...])
    n2 = _rms(h, gmlp_ref[...]).astype(BF16)
    width = D_FF // ff_chunks
    for c in range(ff_chunks):
        cols = slice(c * width, (c + 1) * width)
        a = jnp.maximum(_mm(n2, wup_ref[:, cols]), 0.0)
        h = h + _mm((a * a).astype(BF16), wdown_ref[cols, :])
    n3 =_rms(h, gple_ref[...]).astype(BF16)
    gate = _sigmoid(_mm(n3, wpg_ref[...]))
    h = h + gate * emb
    y_ref[...] = _rms(h, gfin_ref[...]).reshape(y_ref.shape)


def _tail_kernel(h_ref, *refs):
    _tail_body(h_ref[...], *refs, ff_chunks=TAIL_FF_CHUNKS)


def _tail_param_specs(p_spec, g_mlp, g_ple, g_fin, w_up, w_down, w_pg, w_pp):
    return [
        p_spec,
        _resident(g_mlp.shape),
        _resident(g_ple.shape),
        _resident(g_fin.shape),
        _resident(w_up.shape),
        _resident(w_down.shape),
        _resident(w_pg.shape),
        _resident(w_pp.shape),
    ]


def _tail(h, p, g_mlp, g_ple, g_fin, w_up, w_down, w_pg, w_pp):
    t = h.shape[0]
    tile = TAIL_TILE
    assert t % tile == 0
    return pl.pallas_call(
        _tail_kernel,
        grid=(t // tile,),
        in_specs=[pl.BlockSpec((tile, D_MODEL), lambda i: (i, 0))] + _tail_param_specs(
            pl.BlockSpec((tile, PLE_DIM), lambda i: (i, 0)),
            g_mlp, g_ple, g_fin, w_up, w_down, w_pg, w_pp),
        out_specs=pl.BlockSpec((tile, D_MODEL), lambda i: (i, 0)),
        out_shape=jax.ShapeDtypeStruct((t, D_MODEL), F32),
        compiler_params=pltpu.CompilerParams(
            dimension_semantics=("arbitrary",), vmem_limit_bytes=VMEM_LIMIT),
        name="tail",
    )(h, p, g_mlp, g_ple, g_fin, w_up, w_down, w_pg, w_pp)


def _sample_proj_kernel(x_ref, cos_ref, sin_ref, kdec_ref, gmix_ref, ggm_ref, win_f_ref,
                        z_ref, zb_ref, gv_ref, win_b_ref, n_ref):
    s = pl.program_id(0)

    @pl.when(s == 0)
    def _():
        x = x_ref[...].reshape(n_ref.shape)
        n_ref[...] = _rms(x, gmix_ref[...]).astype(BF16)

    scale = jnp.where(s == SEC_K, DK ** -0.5, 1.0).astype(F32)
    wb = (win_f_ref[...] * scale).astype(BF16)
    win_b_ref[...] = wb
    z = _mm(n_ref[...], wb)

    def rotated():
        cos = cos_ref[...]
        sin = sin_ref[...]
        return jnp.concatenate(
            [_rotary(z[:, h * DK:(h + 1) * DK], cos, sin) for h in range(H_A)], axis=-1)

    @pl.when(s < SEC_K)
    def _():
        q = rotated()
        z_ref[...] = q
        zb_ref[...] = q.astype(BF16)

    @pl.when(s == SEC_K)
    def _():
        kd = _times_row_pattern(rotated(), kdec_ref[...])
        z_ref[...] = kd
        zb_ref[...] = kd.astype(BF16)

    @pl.when((s >= SEC_V) & (s < SEC_G))
    def _():
        z_ref[...] = z
        zb_ref[...] = z.astype(BF16)

    @pl.when((s >= SEC_G) & (s < SEC_U))
    def _():
        z_ref[...] = z * _sigmoid(z)

    @pl.when(s == SEC_U)
    def _():
        z_ref[...] = _gelu(z)

    @pl.when(s == SEC_GV)
    def _():
        gv = _layernorm(_gelu(z), ggm_ref[...])
        z_ref[...] = gv
        gv_ref[...] = gv.reshape(gv_ref.shape)

    @pl.when(s >= SEC_GA)
    def _():
        z_ref[...] = _sigmoid(z)


def _sample_proj(x, cos, sin, kdec, g_mix, g_gm, w_in, dec_seq):
    t = x.shape[0] * x.shape[1]
    section = pl.BlockSpec((D_MODEL, SECTION), lambda s: (0, s))
    return pl.pallas_call(
        _sample_proj_kernel,
        grid=(IN_W // SECTION,),
        in_specs=[_resident(x.shape), _resident(cos.shape), _resident(sin.shape),
                  _resident(kdec.shape), _resident(g_mix.shape), _resident(g_gm.shape), section],
        out_specs=[pl.BlockSpec((t, SECTION), lambda s: (0, s)),
                   pl.BlockSpec((t, SECTION), lambda s: (0, jnp.minimum(s, SEC_G - 1))),
                   pl.BlockSpec((t // dec_seq, dec_seq, GM_WIDTH), lambda s: (0, 0, 0)),
                   section],
        out_shape=[jax.ShapeDtypeStruct((t, IN_W), F32),
                   jax.ShapeDtypeStruct((t, OFF_G), BF16),
                   jax.ShapeDtypeStruct((t // dec_seq, dec_seq, GM_WIDTH), F32),
                   jax.ShapeDtypeStruct(w_in.shape, BF16)],
        scratch_shapes=[pltpu.VMEM((t, D_MODEL), BF16)],
        compiler_params=pltpu.CompilerParams(
            dimension_semantics=("arbitrary",), vmem_limit_bytes=VMEM_LIMIT),
        name="sample_proj",
    )(x, cos, sin, kdec, g_mix, g_gm, w_in)


def _sample_ret_sc(z_s, state_u, ch_decay, dec_seq):
    ts = z_s.shape[0]
    n_tok = dec_seq
    pair = SC_TOKENS // n_tok
    n_units = (ts // SC_TOKENS) * H_A
    row_blocks = DK // SC_ROWS
    mesh = plsc.VectorSubcoreMesh(core_axis_name="core", subcore_axis_name="subcore",
                                  num_cores=SC_CORES, num_subcores=SC_SUBCORES)
    group = SC_LANES * SC_COLS

    def lanes(x, dtype):
        return jnp.full((SC_LANES,), x, dtype)

    def body(q, kd, v, s_in, s_out, cross):
        unit = pl.program_id(0)
        step = pl.program_id(1)
        head = unit % H_A
        tok0 = (step // row_blocks) * n_tok
        rc = step % row_blocks
        decay = jnp.float32(ch_decay[H_A - 1])
        for h in range(H_A - 2, -1, -1):
            decay = jnp.where(head == h, jnp.float32(ch_decay[h]), decay)
        dvec = lanes(decay, F32)
        tvecs = [lanes(tok0 + t, jnp.int32) for t in range(n_tok)]

        @pl.when(step == 0)
        def _():
            @pl.loop(0, DV, step=SC_LANES)
            def _(c):
                for t in range(SC_TOKENS):
                    cross.at[t, pl.ds(c, SC_LANES)][...] = jnp.zeros((SC_LANES,), F32)

        d0 = lanes(rc * SC_ROWS, jnp.int32)

        def bcast(ref, r):
            return [plsc.load_gather(ref, [tvecs[t], d0 + r]) for t in range(n_tok)]

        @plsc.parallel_loop(0, DV, step=group, unroll=SC_UNROLL)
        def _(c0):
            cols = [pl.ds(c0 + SC_LANES * j, SC_LANES) for j in range(SC_COLS)]
            acc = [cross.at[tok0 + t, cols[j]][...] for t in range(n_tok) for j in range(SC_COLS)]
            for r in range(SC_ROWS):
                qs = bcast(q, r)
                for j in range(SC_COLS):
                    s = s_in.at[0, r, cols[j]][...]
                    for t in range(n_tok):
                        acc[t * SC_COLS + j] = acc[t * SC_COLS + j] + qs[t] * s
            for t in range(n_tok):
                for j in range(SC_COLS):
                    cross.at[tok0 + t, cols[j]][...] = acc[t * SC_COLS + j]

        @plsc.parallel_loop(0, DV, step=group, unroll=SC_UNROLL)
        def _(c0):
            cols = [pl.ds(c0 + SC_LANES * j, SC_LANES) for j in range(SC_COLS)]
            vrow = [[v.at[tok0 + t, cols[j]][...] for j in range(SC_COLS)] for t in range(n_tok)]
            for rb in range(0, SC_ROWS, SC_ROW_BATCH):
                batch = range(rb, rb + SC_ROW_BATCH)
                ks = {r: bcast(kd, r) for r in batch}
                s = {(r, j): s_in.at[0, r, cols[j]][...] for r in batch for j in range(SC_COLS)}
                for r in batch:
                    for j in range(SC_COLS):
                        terms = [s[r, j] * dvec] + [ks[r][t] * vrow[t][j] for t in range(n_tok)]
                        while len(terms) > 1:
                            terms = [terms[i] + terms[i + 1] if i + 1 < len(terms) else terms[i]
                                     for i in range(0, len(terms), 2)]
                        s_out.at[0, r, cols[j]][...] = terms[0]

    @functools.partial(
        pl.kernel, mesh=mesh,
        out_type=[jax.ShapeDtypeStruct(state_u.shape, F32),
                  jax.ShapeDtypeStruct((ts, V_W), F32)],
        compiler_params=pltpu.CompilerParams(needs_layout_passes=False),
        name="sample_ret")
    def call(q_hbm, kd_hbm, v_hbm, s_hbm, snew_hbm, cross_hbm):
        tok = lambda off, w: pl.BlockSpec((SC_TOKENS, w), lambda u, s: (u // H_A, off // w + u % H_A))
        state = pl.BlockSpec(
            (1, SC_ROWS, DV),
            lambda u, s: (((u // H_A) * pair + s // row_blocks) * H_A + u % H_A, s % row_blocks, 0))
        pltpu.emit_pipeline(
            body,
            grid=(n_units, pair * row_blocks),
            in_specs=[tok(OFF_Q, DK), tok(OFF_K, DK), tok(OFF_V, DV), state],
            out_specs=[state, tok(0, DV)],
            core_axis_name=("core", "subcore"),
            dimension_semantics=(pltpu.PARALLEL, pltpu.ARBITRARY),
        )(q_hbm, kd_hbm, v_hbm, s_hbm, snew_hbm, cross_hbm)

    return call(z_s, z_s, z_s, state_u)


N_MERGE_IN = 18


def _sample_tail_kernel(*refs, dec_seq):
    (x_ref, cross_ref, q_ref, kd_ref, v_ref, sg_ref, u_ref, gv_ref, ga_ref, gb_ref,
     dm_ref, ind_ref, gret_ref, coef_ref, bias_ref, wa_ref, wb_ref, wo_ref) = refs[:N_MERGE_IN]
    t = cross_ref.shape[0]
    ya = []
    for h in range(H_A):
        cols = slice(h * DV, (h + 1) * DV)
        qh = q_ref[:, h * DK:(h + 1) * DK]
        kh = kd_ref[:, h * DK:(h + 1) * DK]
        sc = _mm_nt(qh, kh) * dm_ref[h]
        o = _mm(sc.astype(BF16), v_ref[:, cols])
        o = o + _times_row_pattern(cross_ref[:, cols], ind_ref[:, cols])
        ya.append((sg_ref[:, cols] * _group_rms(o, gret_ref[:, cols])).astype(BF16))
    ya = jnp.concatenate(ya, axis=-1)

    gv = gv_ref[...]
    sub = coef_ref.shape[1]
    mix = None
    for s in range(dec_seq):
        shifted = gv if s == 0 else pltpu.roll(gv, s, 0)
        term = shifted.reshape(t // sub, sub, GM_WIDTH) * coef_ref[s][None]
        mix = term if mix is None else mix + term
    mix = (mix + bias_ref[...][None]).reshape(t, GM_WIDTH)
    yb = (u_ref[...] * mix).astype(BF16)

    merged = ga_ref[...] * _mm(ya, wa_ref[...]) + gb_ref[...] * _mm(yb, wb_ref[...])
    h_mix = x_ref[...].reshape(t, D_MODEL) + _mm(merged.astype(BF16), wo_ref[...])
    _tail_body(h_mix, *refs[N_MERGE_IN:])


def _sample_tail(x, cross, z_s, zb_s, dm, ind, g_ret, coef, bias, w_a, w_b, w_o, p, g_mlp, g_ple, g_fin,
                 w_up, w_down, w_pg, w_pp, dec_seq):
    t = x.shape[0] * dec_seq
    tile = TILE
    row = lambda w: pl.BlockSpec((tile, w), lambda i: (i, 0))
    by_batch = lambda w: pl.BlockSpec((tile // dec_seq, dec_seq, w), lambda i: (i, 0, 0))
    cols = lambda off, w: pl.BlockSpec((tile, w), lambda i: (i, off // w))
    once = lambda off, w: pl.BlockSpec((tile, w), lambda i: (i, off // w), pipeline_mode=pl.Buffered(1))
    return pl.pallas_call(
        functools.partial(_sample_tail_kernel, dec_seq=dec_seq),
        grid=(t // tile,),
        in_specs=[by_batch(D_MODEL), row(V_W), cols(OFF_Q, Q_W), cols(OFF_K, Q_W), cols(OFF_V, V_W),
                  once(OFF_G, V_W), once(OFF_U, GM_WIDTH),
                  cols(OFF_GV, GM_WIDTH), cols(OFF_GA, D_MODEL), cols(OFF_GB, D_MODEL),
                  _resident(dm.shape), _resident(ind.shape),
                  _resident(g_ret.shape), _resident(coef.shape), _resident(bias.shape),
                  _resident(w_a.shape), _resident(w_b.shape), _resident(w_o.shape)]
                 + _tail_param_specs(by_batch(PLE_DIM), g_mlp, g_ple, g_fin, w_up, w_down, w_pg, w_pp),
        out_specs=by_batch(D_MODEL),
        out_shape=jax.ShapeDtypeStruct(x.shape, F32),
        compiler_params=pltpu.CompilerParams(
            dimension_semantics=("arbitrary",), vmem_limit_bytes=VMEM_LIMIT),
        name="sample_tail",
    )(x, cross, zb_s, zb_s, zb_s, z_s, z_s, z_s, z_s, z_s, dm, ind, g_ret, coef, bias, w_a, w_b, w_o,
      p, g_mlp, g_ple, g_fin, w_up, w_down, w_pg, w_pp)


def _rope_tables(pos):
    half = DK // 2
    inv = ROPE_BASE ** (-jnp.arange(half, dtype=F32) / half)
    ang = pos[:, None] * inv[None, :]
    return jnp.cos(ang), jnp.sin(ang)


def _log_decay():
    return np.log(1.0 - 2.0 ** (-5.0 - np.arange(H_A, dtype=np.float64)))


def _as_f32(a):
    return np.ascontiguousarray(a, dtype=np.float32)


def _decay_tables(c):
    lg = _log_decay()
    idx = np.arange(c, dtype=np.float64)
    diff = idx[:, None] - idx[None, :]
    dmat = np.where(diff[None] >= 0, np.exp(np.maximum(diff, 0.0)[None] * lg[:, None, None]), 0.0)
    in_decay = np.exp((idx[None, :] + 1.0) * lg[:, None])
    st_decay = np.exp((c - 1.0 - idx[None, :]) * lg[:, None])
    rows = lambda v: np.broadcast_to(v[:, :, None], (H_A, c, DK))
    return _as_f32(dmat), _as_f32(rows(in_decay)), _as_f32(rows(st_decay))


def _sample_decay_tables(c, rows, period):
    lg = _log_decay()
    t_of = np.arange(rows) % c
    same = (np.arange(rows)[:, None] // c) == (np.arange(rows)[None, :] // c)
    causal = t_of[:, None] >= t_of[None, :]
    mask = np.where((same & causal)[None],
                    np.exp((t_of[:, None] - (c - 1.0))[None] * lg[:, None, None]), 0.0)
    tp = np.arange(period) % c
    key = np.exp((c - 1.0 - tp)[:, None] * lg[None, :])
    cross = np.exp((tp + 1.0)[:, None] * lg[None, :])
    return (_as_f32(mask), _as_f32(np.repeat(key, DK, axis=1)), _as_f32(np.repeat(cross, DV, axis=1)))


def _chunk_decay(c):
    return tuple(float((1.0 - 2.0 ** (-5.0 - h)) ** c) for h in range(H_A))


def kernel(x_prompt, x_sample, p_prompt, p_sample, state_ret, g_mix, w_in, g_ret, w_s, b_s, g_gm,
           w_br_a, w_br_b, w_o, g_mlp, w_up, w_down, g_ple, w_pg, w_pp, g_final):
    depth = w_in.shape[0]
    assert depth == 1
    batch, seq, _ = x_prompt.shape
    dec_batch, dec_seq, _ = x_sample.shape
    assert SC_TOKENS % dec_seq == 0 and (dec_batch * dec_seq) % TILE == 0
    i = 0

    w_a_b = w_br_a[i].astype(BF16)
    w_b_b = w_br_b[i].astype(BF16)
    w_o_b = w_o[i].astype(BF16)
    w_pp_b = w_pp[i].astype(BF16)
    gmix = g_mix[i][None]
    gret = g_ret[i][None]
    ggm = g_gm[i][None]
    gmlp = g_mlp[i][None]
    gple = g_ple[i][None]
    gfin = g_final[None]

    ts = dec_batch * dec_seq
    sub = 2 * dec_seq
    pos_s = PAST_LEN + jnp.arange(dec_seq, dtype=F32)
    cos_s, sin_s = _rope_tables(pos_s)
    cos_s = jnp.tile(cos_s, (dec_batch, 1))
    sin_s = jnp.tile(sin_s, (dec_batch, 1))
    dm_s, kdec_s, ind_s = _sample_decay_tables(dec_seq, TILE, sub)
    z_s, zb_s, gv_s, w_in_b = _sample_proj(x_sample, cos_s, sin_s, kdec_s, gmix, ggm, w_in[i], dec_seq)

    cos_p, sin_p = _rope_tables(jnp.arange(seq, dtype=F32))
    dmat, ind_p, std_p = _decay_tables(TILE)
    tril = np.tril(np.ones((GM_CHUNK, GM_CHUNK), dtype=bool))
    ws_tril = jnp.where(tril[None], w_s[i], 0.0).astype(BF16)
    bias_p = jnp.repeat(jnp.transpose(b_s[i]), GM_GC, axis=1)
    h_p, s_p, w_up_b, w_down_b, w_pg_b = _prompt_mix(
        x_prompt, cos_p, sin_p, dmat, ind_p, std_p, _chunk_decay(TILE),
        gmix, gret, ggm, w_in_b, ws_tril, bias_p, w_a_b, w_b_b, w_o_b, w_up[i], w_down[i], w_pg[i])
    y_p = _tail(h_p.reshape(batch * seq, D_MODEL), p_prompt[i].reshape(batch * seq, PLE_DIM),
                gmlp, gple, gfin, w_up_b, w_down_b, w_pg_b, w_pp_b)

    s_s, cross_s = _sample_ret_sc(
        z_s, state_ret[i].reshape(dec_batch * H_A, DK, DV), _chunk_decay(dec_seq), dec_seq)

    w_small = jnp.where(tril[None, :dec_seq, :dec_seq], w_s[i][:, :dec_seq, :dec_seq], 0.0)
    tpos = jnp.arange(sub) % dec_seq
    coef = []
    for s in range(dec_seq):
        src = tpos - s
        c = jnp.where((src >= 0)[None, :], w_small[:, tpos, jnp.maximum(src, 0)], 0.0)
        coef.append(jnp.repeat(jnp.transpose(c), GM_GC, axis=1))
    coef = jnp.stack(coef, axis=0)
    bias_s = jnp.repeat(jnp.transpose(b_s[i][:, tpos]), GM_GC, axis=1)
    y_s = _sample_tail(x_sample, cross_s, z_s, zb_s, dm_s, ind_s, gret, coef, bias_s, w_a_b, w_b_b, w_o_b,
                       p_sample[i], gmlp, gple, gfin,
                       w_up_b, w_down_b, w_pg_b, w_pp_b, dec_seq)

    return (y_p.reshape(batch, seq, D_MODEL),
            y_s,
            s_p[None],
            s_s.reshape(dec_batch, H_A, DK, DV)[None],
            gv_s[None])
```

```python
import functools

import numpy as np
import jax
import jax.numpy as jnp
from jax import lax
from jax.experimental import pallas as pl
from jax.experimental.pallas import tpu as pltpu
from jax.experimental.pallas import tpu_sc as plsc

D_MODEL = 1024
H_A = 4
DK = 256
DV = 512
Q_W = H_A * DK
V_W = H_A * DV
GM_WIDTH = 1024
GM_GROUPS = 8
GM_GC = GM_WIDTH // GM_GROUPS
GM_CHUNK = 128
D_FF = 4 * D_MODEL
PLE_DIM = 256
EPS = 1e-6
ROPE_BASE = 10000.0
PAST_LEN = 16384

OFF_Q = 0
OFF_K = OFF_Q + Q_W
OFF_V = OFF_K + Q_W
OFF_G = OFF_V + V_W
OFF_U = OFF_G + V_W
OFF_GV = OFF_U + GM_WIDTH
OFF_GA = OFF_GV + GM_WIDTH
OFF_GB = OFF_GA + D_MODEL
IN_W = OFF_GB + D_MODEL
SECTION = 1024
SEC_K, SEC_V, SEC_G, SEC_U, SEC_GV, SEC_GA = (
    OFF_K // SECTION, OFF_V // SECTION, OFF_G // SECTION, OFF_U // SECTION, OFF_GV // SECTION,
    OFF_GA // SECTION)

TILE = 256
TAIL_TILE = 1024
TAIL_FF_CHUNKS = 4
VMEM_LIMIT = 56 * 1024 * 1024

SC_CORES = 2
SC_SUBCORES = 16
SC_LANES = 16
SC_ROWS = 16
SC_COLS = 4
SC_UNROLL = 2
SC_ROW_BATCH = 2
SC_TOKENS = 8

BF16 = jnp.bfloat16
F32 = jnp.float32


def _mm(a, b):
    return jnp.dot(a, b, preferred_element_type=F32)


def _mm_nt(a, b):
    return lax.dot_general(a, b, (((1,), (1,)), ((), ())), preferred_element_type=F32)


def _mm_tn(a, b):
    return lax.dot_general(a, b, (((0,), (0,)), ((), ())), preferred_element_type=F32)


def _rms(x, g):
    return x * lax.rsqrt(jnp.mean(x * x, axis=-1, keepdims=True) + EPS) * g


def _sigmoid(x):
    return 1.0 / (1.0 + jnp.exp(-x))


def _gelu(x):
    return 0.5 * x * (1.0 + lax.erf(x * (2.0 ** -0.5)))


def _layernorm(x, g):
    mu = jnp.mean(x, axis=-1, keepdims=True)
    xc = x - mu
    return xc * lax.rsqrt(jnp.mean(xc * xc, axis=-1, keepdims=True) + EPS) * g


def _rotary(z, cos, sin):
    half = DK // 2
    z1, z2 = z[:, :half], z[:, half:]
    return jnp.concatenate([z1 * cos - z2 * sin, z2 * cos + z1 * sin], axis=-1)


def _group_rms(o, g):
    return o * lax.rsqrt(jnp.mean(o * o, axis=-1, keepdims=True) + EPS) * g


def _times_row_pattern(x, pattern):
    t, w = x.shape
    p = pattern.shape[0]
    return (x.reshape(t // p, p, w) * pattern[None]).reshape(t, w)


def _prompt_mix_kernel(x_ref, cos_ref, sin_ref, dmat_ref, ind_ref, std_ref, gmix_ref, gret_ref,
                       ggm_ref, win_ref, ws_ref, bias_ref, wa_ref, wb_ref, wo_ref,
                       wup_f_ref, wdown_f_ref, wpg_f_ref,
                       h_ref, s_ref, wup_b_ref, wdown_b_ref, wpg_b_ref, ya_ref, yb_ref, *, ch_decay):
    j = pl.program_id(1)

    @pl.when(j == 0)
    def _():
        s_ref[...] = jnp.zeros_like(s_ref)

    x = x_ref[0]
    n = _rms(x, gmix_ref[...]).astype(BF16)
    cos = cos_ref[...]
    sin = sin_ref[...]

    u = _gelu(_mm(n, win_ref[:, OFF_U:OFF_U + GM_WIDTH]))
    gv = _layernorm(_gelu(_mm(n, win_ref[:, OFF_GV:OFF_GV + GM_WIDTH])), ggm_ref[...])
    gvb = gv.astype(BF16)

    zq = _mm(n, win_ref[:, OFF_Q:OFF_Q + Q_W])
    zk = _mm(n, win_ref[:, OFF_K:OFF_K + Q_W])
    zg = _mm(n, win_ref[:, OFF_G:OFF_G + V_W])

    qb, kb, qd, kd = [], [], [], []
    for h in range(H_A):
        qr = _rotary(zq[:, h * DK:(h + 1) * DK], cos, sin)
        kr = _rotary(zk[:, h * DK:(h + 1) * DK], cos, sin)
        qb.append(qr.astype(BF16))
        kb.append(kr.astype(BF16))
        qd.append(qb[h] * ind_ref[h])
        kd.append(kb[h] * std_ref[h])

    zv = _mm(n, win_ref[:, OFF_V:OFF_V + V_W]).astype(BF16)
    out_gate = zg * _sigmoid(zg)

    for g in range(GM_GROUPS):
        cols = slice(g * GM_GC, (g + 1) * GM_GC)
        pair = jnp.concatenate([gvb[:GM_CHUNK, cols], gvb[GM_CHUNK:, cols]], axis=-1)
        sg = _mm(ws_ref[g], pair)
        for c in range(TILE // GM_CHUNK):
            rows = slice(c * GM_CHUNK, (c + 1) * GM_CHUNK)
            sg_c = sg[:, c * GM_GC:(c + 1) * GM_GC] + bias_ref[:, cols]
            yb_ref[rows, cols] = (u[rows, cols] * sg_c).astype(BF16)

    ga = gb = None
    for h in range(H_A):
        vb = zv[:, h * DV:(h + 1) * DV]
        sc = _mm_nt(qb[h], kb[h]).astype(BF16) * dmat_ref[h]
        s_old = s_ref[0, h]
        o = _mm(sc, vb) + _mm(qd[h], s_old.astype(BF16))
        s_ref[0, h] = s_old * ch_decay[h] + _mm_tn(kd[h], vb)
        on = _group_rms(o, gret_ref[:, h * DV:(h + 1) * DV])
        ya_ref[:, h * DV:(h + 1) * DV] = (out_gate[:, h * DV:(h + 1) * DV] * on).astype(BF16)
        if h == 0:
            ga = _sigmoid(_mm(n, win_ref[:, OFF_GA:OFF_GA + D_MODEL]))
        if h == 2:
            gb = _sigmoid(_mm(n, win_ref[:, OFF_GB:OFF_GB + D_MODEL]))
    wup_b_ref[...] = wup_f_ref[...].astype(BF16)
    wdown_b_ref[...] = wdown_f_ref[...].astype(BF16)
    wpg_b_ref[...] = wpg_f_ref[...].astype(BF16)

    merged = ga * _mm(ya_ref[...], wa_ref[...]) + gb * _mm(yb_ref[...], wb_ref[...])
    h_ref[0] = x + _mm(merged.astype(BF16), wo_ref[...])


def _resident(shape):
    zeros = (0,) * len(shape)
    return pl.BlockSpec(shape, lambda *_: zeros, pipeline_mode=pl.Buffered(1))


def _prompt_mix(x, cos, sin, dmat, ind, std, ch_decay, g_mix, g_ret, g_gm, w_in, ws_tril, bias, w_a, w_b, w_o,
                w_up, w_down, w_pg):
    b, L, _ = x.shape
    steps = L // TILE
    total = b * steps

    def slab(w):
        rows = w.shape[0] // total
        assert rows * total == w.shape[0] and rows % 16 == 0
        return pl.BlockSpec((rows, w.shape[1]), lambda i, j: (i * steps + j, 0))

    return pl.pallas_call(
        functools.partial(_prompt_mix_kernel, ch_decay=ch_decay),
        grid=(b, steps),
        in_specs=[
            pl.BlockSpec((1, TILE, D_MODEL), lambda i, j: (i, j, 0)),
            pl.BlockSpec((TILE, DK // 2), lambda i, j: (j, 0)),
            pl.BlockSpec((TILE, DK // 2), lambda i, j: (j, 0)),
            _resident(dmat.shape),
            _resident(ind.shape),
            _resident(std.shape),
            _resident(g_mix.shape),
            _resident(g_ret.shape),
            _resident(g_gm.shape),
            _resident(w_in.shape),
            _resident(ws_tril.shape),
            _resident(bias.shape),
            _resident(w_a.shape),
            _resident(w_b.shape),
            _resident(w_o.shape),
            slab(w_up),
            slab(w_down),
            slab(w_pg),
        ],
        out_specs=[
            pl.BlockSpec((1, TILE, D_MODEL), lambda i, j: (i, j, 0)),
            pl.BlockSpec((1, H_A, DK, DV), lambda i, j: (i, 0, 0, 0)),
            slab(w_up),
            slab(w_down),
            slab(w_pg),
        ],
        out_shape=[
            jax.ShapeDtypeStruct((b, L, D_MODEL), F32),
            jax.ShapeDtypeStruct((b, H_A, DK, DV), F32),
            jax.ShapeDtypeStruct(w_up.shape, BF16),
            jax.ShapeDtypeStruct(w_down.shape, BF16),
            jax.ShapeDtypeStruct(w_pg.shape, BF16),
        ],
        scratch_shapes=[
            pltpu.VMEM((TILE, V_W), BF16),
            pltpu.VMEM((TILE, GM_WIDTH), BF16),
        ],
        compiler_params=pltpu.CompilerParams(
            dimension_semantics=("arbitrary", "arbitrary"), vmem_limit_bytes=VMEM_LIMIT),
        name="prompt_mix",
    )(x, cos, sin, dmat, ind, std, g_mix, g_ret, g_gm, w_in, ws_tril, bias, w_a, w_b, w_o,
      w_up, w_down, w_pg)


def _tail_body(h, p_ref, gmlp_ref, gple_ref, gfin_ref, wup_ref, wdown_ref, wpg_ref, wpp_ref, y_ref,
               ff_chunks=1):
    emb = _mm(p_ref[...].reshape(h.shape[0], PLE_DIM).astype(BF16), wpp_ref[...])
    n2 = _rms(h, gmlp_ref[...]).astype(BF16)
    width = D_FF // ff_chunks
    for c in range(ff_chunks):
        cols = slice(c * width, (c + 1) * width)
        a = jnp.maximum(_mm(n2, wup_ref[:, cols]), 0.0)
        h = h + _mm((a * a).astype(BF16), wdown_ref[cols, :])
    n3 =_rms(h, gple_ref[...]).astype(BF16)
    gate = _sigmoid(_mm(n3, wpg_ref[...]))
    h = h + gate * emb
    y_ref[...] = _rms(h, gfin_ref[...]).reshape(y_ref.shape)


def _tail_kernel(h_ref, *refs):
    _tail_body(h_ref[...], *refs, ff_chunks=TAIL_FF_CHUNKS)


def _tail_param_specs(p_spec, g_mlp, g_ple, g_fin, w_up, w_down, w_pg, w_pp):
    return [
        p_spec,
        _resident(g_mlp.shape),
        _resident(g_ple.shape),
        _resident(g_fin.shape),
        _resident(w_up.shape),
        _resident(w_down.shape),
        _resident(w_pg.shape),
        _resident(w_pp.shape),
    ]


def _tail(h, p, g_mlp, g_ple, g_fin, w_up, w_down, w_pg, w_pp):
    t = h.shape[0]
    tile = TAIL_TILE
    assert t % tile == 0
    return pl.pallas_call(
        _tail_kernel,
        grid=(t // tile,),
        in_specs=[pl.BlockSpec((tile, D_MODEL), lambda i: (i, 0))] + _tail_param_specs(
            pl.BlockSpec((tile, PLE_DIM), lambda i: (i, 0)),
            g_mlp, g_ple, g_fin, w_up, w_down, w_pg, w_pp),
        out_specs=pl.BlockSpec((tile, D_MODEL), lambda i: (i, 0)),
        out_shape=jax.ShapeDtypeStruct((t, D_MODEL), F32),
        compiler_params=pltpu.CompilerParams(
            dimension_semantics=("arbitrary",), vmem_limit_bytes=VMEM_LIMIT),
        name="tail",
    )(h, p, g_mlp, g_ple, g_fin, w_up, w_down, w_pg, w_pp)


def _sample_proj_kernel(x_ref, cos_ref, sin_ref, kdec_ref, gmix_ref, ggm_ref, win_f_ref,
                        z_ref, zb_ref, gv_ref, win_b_ref, n_ref):
    s = pl.program_id(0)

    @pl.when(s == 0)
    def _():
        x = x_ref[...].reshape(n_ref.shape)
        n_ref[...] = _rms(x, gmix_ref[...]).astype(BF16)

    scale = jnp.where(s == SEC_K, DK ** -0.5, 1.0).astype(F32)
    wb = (win_f_ref[...] * scale).astype(BF16)
    win_b_ref[...] = wb
    z = _mm(n_ref[...], wb)

    def rotated():
        cos = cos_ref[...]
        sin = sin_ref[...]
        return jnp.concatenate(
            [_rotary(z[:, h * DK:(h + 1) * DK], cos, sin) for h in range(H_A)], axis=-1)

    @pl.when(s < SEC_K)
    def _():
        q = rotated()
        z_ref[...] = q
        zb_ref[...] = q.astype(BF16)

    @pl.when(s == SEC_K)
    def _():
        kd = _times_row_pattern(rotated(), kdec_ref[...])
        z_ref[...] = kd
        zb_ref[...] = kd.astype(BF16)

    @pl.when((s >= SEC_V) & (s < SEC_G))
    def _():
        z_ref[...] = z
        zb_ref[...] = z.astype(BF16)

    @pl.when((s >= SEC_G) & (s < SEC_U))
    def _():
        z_ref[...] = z * _sigmoid(z)

    @pl.when(s == SEC_U)
    def _():
        z_ref[...] = _gelu(z)

    @pl.when(s == SEC_GV)
    def _():
        gv = _layernorm(_gelu(z), ggm_ref[...])
        z_ref[...] = gv
        gv_ref[...] = gv.reshape(gv_ref.shape)

    @pl.when(s >= SEC_GA)
    def _():
        z_ref[...] = _sigmoid(z)


def _sample_proj(x, cos, sin, kdec, g_mix, g_gm, w_in, dec_seq):
    t = x.shape[0] * x.shape[1]
    section = pl.BlockSpec((D_MODEL, SECTION), lambda s: (0, s))
    return pl.pallas_call(
        _sample_proj_kernel,
        grid=(IN_W // SECTION,),
        in_specs=[_resident(x.shape), _resident(cos.shape), _resident(sin.shape),
                  _resident(kdec.shape), _resident(g_mix.shape), _resident(g_gm.shape), section],
        out_specs=[pl.BlockSpec((t, SECTION), lambda s: (0, s)),
                   pl.BlockSpec((t, SECTION), lambda s: (0, jnp.minimum(s, SEC_G - 1))),
                   pl.BlockSpec((t // dec_seq, dec_seq, GM_WIDTH), lambda s: (0, 0, 0)),
                   section],
        out_shape=[jax.ShapeDtypeStruct((t, IN_W), F32),
                   jax.ShapeDtypeStruct((t, OFF_G), BF16),
                   jax.ShapeDtypeStruct((t // dec_seq, dec_seq, GM_WIDTH), F32),
                   jax.ShapeDtypeStruct(w_in.shape, BF16)],
        scratch_shapes=[pltpu.VMEM((t, D_MODEL), BF16)],
        compiler_params=pltpu.CompilerParams(
            dimension_semantics=("arbitrary",), vmem_limit_bytes=VMEM_LIMIT),
        name="sample_proj",
    )(x, cos, sin, kdec, g_mix, g_gm, w_in)


def _sample_ret_sc(z_s, state_u, ch_decay, dec_seq):
    ts = z_s.shape[0]
    n_tok = dec_seq
    pair = SC_TOKENS // n_tok
    n_units = (ts // SC_TOKENS) * H_A
    row_blocks = DK // SC_ROWS
    mesh = plsc.VectorSubcoreMesh(core_axis_name="core", subcore_axis_name="subcore",
                                  num_cores=SC_CORES, num_subcores=SC_SUBCORES)
    group = SC_LANES * SC_COLS

    def lanes(x, dtype):
        return jnp.full((SC_LANES,), x, dtype)

    def body(q, kd, v, s_in, s_out, cross):
        unit = pl.program_id(0)
        step = pl.program_id(1)
        head = unit % H_A
        tok0 = (step // row_blocks) * n_tok
        rc = step % row_blocks
        decay = jnp.float32(ch_decay[H_A - 1])
        for h in range(H_A - 2, -1, -1):
            decay = jnp.where(head == h, jnp.float32(ch_decay[h]), decay)
        dvec = lanes(decay, F32)
        tvecs = [lanes(tok0 + t, jnp.int32) for t in range(n_tok)]

        @pl.when(step == 0)
        def _():
            @pl.loop(0, DV, step=SC_LANES)
            def _(c):
                for t in range(SC_TOKENS):
                    cross.at[t, pl.ds(c, SC_LANES)][...] = jnp.zeros((SC_LANES,), F32)

        d0 = lanes(rc * SC_ROWS, jnp.int32)

        def bcast(ref, r):
            return [plsc.load_gather(ref, [tvecs[t], d0 + r]) for t in range(n_tok)]

        @plsc.parallel_loop(0, DV, step=group, unroll=SC_UNROLL)
        def _(c0):
            cols = [pl.ds(c0 + SC_LANES * j, SC_LANES) for j in range(SC_COLS)]
            acc = [cross.at[tok0 + t, cols[j]][...] for t in range(n_tok) for j in range(SC_COLS)]
            for r in range(SC_ROWS):
                qs = bcast(q, r)
                for j in range(SC_COLS):
                    s = s_in.at[0, r, cols[j]][...]
                    for t in range(n_tok):
                        acc[t * SC_COLS + j] = acc[t * SC_COLS + j] + qs[t] * s
            for t in range(n_tok):
                for j in range(SC_COLS):
                    cross.at[tok0 + t, cols[j]][...] = acc[t * SC_COLS + j]

        @plsc.parallel_loop(0, DV, step=group, unroll=SC_UNROLL)
        def _(c0):
            cols = [pl.ds(c0 + SC_LANES * j, SC_LANES) for j in range(SC_COLS)]
            vrow = [[v.at[tok0 + t, cols[j]][...] for j in range(SC_COLS)] for t in range(n_tok)]
            for rb in range(0, SC_ROWS, SC_ROW_BATCH):
                batch = range(rb, rb + SC_ROW_BATCH)
                ks = {r: bcast(kd, r) for r in batch}
                s = {(r, j): s_in.at[0, r, cols[j]][...] for r in batch for j in range(SC_COLS)}
                for r in batch:
                    for j in range(SC_COLS):
                        terms = [s[r, j] * dvec] + [ks[r][t] * vrow[t][j] for t in range(n_tok)]
                        while len(terms) > 1:
                            terms = [terms[i] + terms[i + 1] if i + 1 < len(terms) else terms[i]
                                     for i in range(0, len(terms), 2)]
                        s_out.at[0, r, cols[j]][...] = terms[0]

    @functools.partial(
        pl.kernel, mesh=mesh,
        out_type=[jax.ShapeDtypeStruct(state_u.shape, F32),
                  jax.ShapeDtypeStruct((ts, V_W), F32)],
        compiler_params=pltpu.CompilerParams(needs_layout_passes=False),
        name="sample_ret")
    def call(q_hbm, kd_hbm, v_hbm, s_hbm, snew_hbm, cross_hbm):
        tok = lambda off, w: pl.BlockSpec((SC_TOKENS, w), lambda u, s: (u // H_A, off // w + u % H_A))
        state = pl.BlockSpec(
            (1, SC_ROWS, DV),
            lambda u, s: (((u // H_A) * pair + s // row_blocks) * H_A + u % H_A, s % row_blocks, 0))
        pltpu.emit_pipeline(
            body,
            grid=(n_units, pair * row_blocks),
            in_specs=[tok(OFF_Q, DK), tok(OFF_K, DK), tok(OFF_V, DV), state],
            out_specs=[state, tok(0, DV)],
            core_axis_name=("core", "subcore"),
            dimension_semantics=(pltpu.PARALLEL, pltpu.ARBITRARY),
        )(q_hbm, kd_hbm, v_hbm, s_hbm, snew_hbm, cross_hbm)

    return call(z_s, z_s, z_s, state_u)


N_MERGE_IN = 18


def _sample_tail_kernel(*refs, dec_seq):
    (x_ref, cross_ref, q_ref, kd_ref, v_ref, sg_ref, u_ref, gv_ref, ga_ref, gb_ref,
     dm_ref, ind_ref, gret_ref, coef_ref, bias_ref, wa_ref, wb_ref, wo_ref) = refs[:N_MERGE_IN]
    t = cross_ref.shape[0]
    ya = []
    for h in range(H_A):
        cols = slice(h * DV, (h + 1) * DV)
        qh = q_ref[:, h * DK:(h + 1) * DK]
        kh = kd_ref[:, h * DK:(h + 1) * DK]
        sc = _mm_nt(qh, kh) * dm_ref[h]
        o = _mm(sc.astype(BF16), v_ref[:, cols])
        o = o + _times_row_pattern(cross_ref[:, cols], ind_ref[:, cols])
        ya.append((sg_ref[:, cols] * _group_rms(o, gret_ref[:, cols])).astype(BF16))
    ya = jnp.concatenate(ya, axis=-1)

    gv = gv_ref[...]
    sub = coef_ref.shape[1]
    mix = None
    for s in range(dec_seq):
        shifted = gv if s == 0 else pltpu.roll(gv, s, 0)
        term = shifted.reshape(t // sub, sub, GM_WIDTH) * coef_ref[s][None]
        mix = term if mix is None else mix + term
    mix = (mix + bias_ref[...][None]).reshape(t, GM_WIDTH)
    yb = (u_ref[...] * mix).astype(BF16)

    merged = ga_ref[...] * _mm(ya, wa_ref[...]) + gb_ref[...] * _mm(yb, wb_ref[...])
    h_mix = x_ref[...].reshape(t, D_MODEL) + _mm(merged.astype(BF16), wo_ref[...])
    _tail_body(h_mix, *refs[N_MERGE_IN:])


def _sample_tail(x, cross, z_s, zb_s, dm, ind, g_ret, coef, bias, w_a, w_b, w_o, p, g_mlp, g_ple, g_fin,
                 w_up, w_down, w_pg, w_pp, dec_seq):
    t = x.shape[0] * dec_seq
    tile = TILE
    row = lambda w: pl.BlockSpec((tile, w), lambda i: (i, 0))
    by_batch = lambda w: pl.BlockSpec((tile // dec_seq, dec_seq, w), lambda i: (i, 0, 0))
    cols = lambda off, w: pl.BlockSpec((tile, w), lambda i: (i, off // w))
    once = lambda off, w: pl.BlockSpec((tile, w), lambda i: (i, off // w), pipeline_mode=pl.Buffered(1))
    return pl.pallas_call(
        functools.partial(_sample_tail_kernel, dec_seq=dec_seq),
        grid=(t // tile,),
        in_specs=[by_batch(D_MODEL), row(V_W), cols(OFF_Q, Q_W), cols(OFF_K, Q_W), cols(OFF_V, V_W),
                  once(OFF_G, V_W), once(OFF_U, GM_WIDTH),
                  cols(OFF_GV, GM_WIDTH), cols(OFF_GA, D_MODEL), cols(OFF_GB, D_MODEL),
                  _resident(dm.shape), _resident(ind.shape),
                  _resident(g_ret.shape), _resident(coef.shape), _resident(bias.shape),
                  _resident(w_a.shape), _resident(w_b.shape), _resident(w_o.shape)]
                 + _tail_param_specs(by_batch(PLE_DIM), g_mlp, g_ple, g_fin, w_up, w_down, w_pg, w_pp),
        out_specs=by_batch(D_MODEL),
        out_shape=jax.ShapeDtypeStruct(x.shape, F32),
        compiler_params=pltpu.CompilerParams(
            dimension_semantics=("arbitrary",), vmem_limit_bytes=VMEM_LIMIT),
        name="sample_tail",
    )(x, cross, zb_s, zb_s, zb_s, z_s, z_s, z_s, z_s, z_s, dm, ind, g_ret, coef, bias, w_a, w_b, w_o,
      p, g_mlp, g_ple, g_fin, w_up, w_down, w_pg, w_pp)


def _rope_tables(pos):
    half = DK // 2
    inv = ROPE_BASE ** (-jnp.arange(half, dtype=F32) / half)
    ang = pos[:, None] * inv[None, :]
    return jnp.cos(ang), jnp.sin(ang)


def _log_decay():
    return np.log(1.0 - 2.0 ** (-5.0 - np.arange(H_A, dtype=np.float64)))


def _as_f32(a):
    return np.ascontiguousarray(a, dtype=np.float32)


def _decay_tables(c):
    lg = _log_decay()
    idx = np.arange(c, dtype=np.float64)
    diff = idx[:, None] - idx[None, :]
    dmat = np.where(diff[None] >= 0, np.exp(np.maximum(diff, 0.0)[None] * lg[:, None, None]), 0.0)
    in_decay = np.exp((idx[None, :] + 1.0) * lg[:, None])
    st_decay = np.exp((c - 1.0 - idx[None, :]) * lg[:, None])
    rows = lambda v: np.broadcast_to(v[:, :, None], (H_A, c, DK))
    return _as_f32(dmat), _as_f32(rows(in_decay)), _as_f32(rows(st_decay))


def _sample_decay_tables(c, rows, period):
    lg = _log_decay()
    t_of = np.arange(rows) % c
    same = (np.arange(rows)[:, None] // c) == (np.arange(rows)[None, :] // c)
    causal = t_of[:, None] >= t_of[None, :]
    mask = np.where((same & causal)[None],
                    np.exp((t_of[:, None] - (c - 1.0))[None] * lg[:, None, None]), 0.0)
    tp = np.arange(period) % c
    key = np.exp((c - 1.0 - tp)[:, None] * lg[None, :])
    cross = np.exp((tp + 1.0)[:, None] * lg[None, :])
    return (_as_f32(mask), _as_f32(np.repeat(key, DK, axis=1)), _as_f32(np.repeat(cross, DV, axis=1)))


def _chunk_decay(c):
    return tuple(float((1.0 - 2.0 ** (-5.0 - h)) ** c) for h in range(H_A))


def kernel(x_prompt, x_sample, p_prompt, p_sample, state_ret, g_mix, w_in, g_ret, w_s, b_s, g_gm,
           w_br_a, w_br_b, w_o, g_mlp, w_up, w_down, g_ple, w_pg, w_pp, g_final):
    depth = w_in.shape[0]
    assert depth == 1
    batch, seq, _ = x_prompt.shape
    dec_batch, dec_seq, _ = x_sample.shape
    assert SC_TOKENS % dec_seq == 0 and (dec_batch * dec_seq) % TILE == 0
    i = 0

    w_a_b = w_br_a[i].astype(BF16)
    w_b_b = w_br_b[i].astype(BF16)
    w_o_b = w_o[i].astype(BF16)
    w_pp_b = w_pp[i].astype(BF16)
    gmix = g_mix[i][None]
    gret = g_ret[i][None]
    ggm = g_gm[i][None]
    gmlp = g_mlp[i][None]
    gple = g_ple[i][None]
    gfin = g_final[None]

    ts = dec_batch * dec_seq
    sub = 2 * dec_seq
    pos_s = PAST_LEN + jnp.arange(dec_seq, dtype=F32)
    cos_s, sin_s = _rope_tables(pos_s)
    cos_s = jnp.tile(cos_s, (dec_batch, 1))
    sin_s = jnp.tile(sin_s, (dec_batch, 1))
    dm_s, kdec_s, ind_s = _sample_decay_tables(dec_seq, TILE, sub)
    z_s, zb_s, gv_s, w_in_b = _sample_proj(x_sample, cos_s, sin_s, kdec_s, gmix, ggm, w_in[i], dec_seq)

    cos_p, sin_p = _rope_tables(jnp.arange(seq, dtype=F32))
    dmat, ind_p, std_p = _decay_tables(TILE)
    dmat = jnp.asarray(dmat, BF16)
    ind_p = jnp.asarray(ind_p, BF16)
    std_p = jnp.asarray(std_p, BF16)
    tril = np.tril(np.ones((GM_CHUNK, GM_CHUNK), dtype=bool))
    ws_tril = jnp.where(tril[None], w_s[i], 0.0).astype(BF16)
    bias_p = jnp.repeat(jnp.transpose(b_s[i]), GM_GC, axis=1)
    h_p, s_p, w_up_b, w_down_b, w_pg_b = _prompt_mix(
        x_prompt, cos_p, sin_p, dmat, ind_p, std_p, _chunk_decay(TILE),
        gmix, gret, ggm, w_in_b, ws_tril, bias_p, w_a_b, w_b_b, w_o_b, w_up[i], w_down[i], w_pg[i])
    y_p = _tail(h_p.reshape(batch * seq, D_MODEL), p_prompt[i].reshape(batch * seq, PLE_DIM),
                gmlp, gple, gfin, w_up_b, w_down_b, w_pg_b, w_pp_b)

    s_s, cross_s = _sample_ret_sc(
        z_s, state_ret[i].reshape(dec_batch * H_A, DK, DV), _chunk_decay(dec_seq), dec_seq)

    w_small = jnp.where(tril[None, :dec_seq, :dec_seq], w_s[i][:, :dec_seq, :dec_seq], 0.0)
    tpos = jnp.arange(sub) % dec_seq
    coef = []
    for s in range(dec_seq):
        src = tpos - s
        c = jnp.where((src >= 0)[None, :], w_small[:, tpos, jnp.maximum(src, 0)], 0.0)
        coef.append(jnp.repeat(jnp.transpose(c), GM_GC, axis=1))
    coef = jnp.stack(coef, axis=0)
    bias_s = jnp.repeat(jnp.transpose(b_s[i][:, tpos]), GM_GC, axis=1)
    y_s = _sample_tail(x_sample, cross_s, z_s, zb_s, dm_s, ind_s, gret, coef, bias_s, w_a_b, w_b_b, w_o_b,
                       p_sample[i], gmlp, gple, gfin,
                       w_up_b, w_down_b, w_pg_b, w_pp_b, dec_seq)

    return (y_p.reshape(batch, seq, D_MODEL),
            y_s,
            s_p[None],
            s_s.reshape(dec_batch, H_A, DK, DV)[None],
            gv_s[None])
```
